```python
import math
import jax
import jax.numpy as jnp
from jax import lax
import numpy as np

D_MODEL = 1024
BATCH = 4
SEQ = 4096
DEPTH = 4
DEC_BATCH = 32
DEC_SEQ = 8
PAST_LEN = 8192
PAGE_SIZE = 128

N_MIXERS = 3
MEM_LEN = 256
NORM_EPS = 1e-6

ML_INNER = 2 * D_MODEL
ML_HEADS = 4
ML_HD = ML_INNER // ML_HEADS
ML_CONV = 4
ML_CHUNK = 64

SB_HEADS = 8
SB_HD = D_MODEL // SB_HEADS
SB_QBLOCK = 128

SSD_INNER = 2 * D_MODEL
SSD_HD = 64
SSD_HEADS = SSD_INNER // SSD_HD
SSD_GROUPS = 8
SSD_STATE = 128
SSD_CONV = 4
SSD_CHUNK = 128
SSD_CONV_CH = SSD_INNER + 2 * SSD_GROUPS * SSD_STATE
SSD_IN_DIM = SSD_INNER + SSD_CONV_CH + SSD_HEADS

XA_HEADS = 4
XA_HD = D_MODEL // XA_HEADS

FFN_HIDDEN = ((8 * D_MODEL + 3 * 256 - 1) // (3 * 256)) * 256

LAYER_KINDS = tuple(layer % N_MIXERS for layer in range(DEPTH))
N_ML = LAYER_KINDS.count(0)
N_SB = LAYER_KINDS.count(1)
N_SSD = LAYER_KINDS.count(2)

kernel_name = 'hybrid_mlstm_stickbreak_ssd_step'


def _rmsnorm(x, g):
    x32 = x.astype(jnp.float32)
    y = x32 * lax.rsqrt(jnp.mean(x32 * x32, axis=-1, keepdims=True) + NORM_EPS)
    return y.astype(x.dtype) * g


def _causal_conv(x, buf, w, b):
    width = w.shape[0]
    seq = x.shape[1]
    xp = jnp.concatenate([buf.astype(x.dtype), x], axis=1)
    y = b + xp[:, 0:seq] * w[0]
    for j in range(1, width):
        y = y + xp[:, j:j + seq] * w[j]
    return y, xp[:, xp.shape[1] - (width - 1):]


def _stack(items, idx):
    return jnp.stack([it[idx] for it in items])


def _gather_pages(pool, page_table):
    g = pool[page_table]
    return g.reshape(page_table.shape[0], -1, *pool.shape[2:])


def _mlstm_chunkwise(q, k, v, i_pre, log_f, c0, n0, m0):
    bsz, seq, nh, dh = q.shape
    L = math.gcd(seq, ML_CHUNK)
    nc = seq // L

    def chunks(t):
        return jnp.swapaxes(t.reshape(bsz, nc, L, *t.shape[2:]), 0, 1)

    causal = jnp.tril(jnp.ones((L, L), dtype=bool))[None, :, :, None]

    def step(carry, inp):
        c, n, m = carry
        qq, kk, vv, ii, ff = inp
        b = jnp.cumsum(ff, axis=1)
        log_d = jnp.where(causal, b[:, :, None] - b[:, None] + ii[:, None], -jnp.inf)
        log_inter = b + m[:, None]
        m_t = jnp.maximum(log_inter, jnp.max(log_d, axis=2))
        d_mat = jnp.exp(log_d - m_t[:, :, None])
        w_inter = jnp.exp(log_inter - m_t)
        s = jnp.einsum('bthd,bshd->btsh', qq, kk) * d_mat
        num = jnp.einsum('btsh,bshd->bthd', s, vv) + w_inter[..., None] * jnp.einsum('bthk,bhkv->bthv', qq, c)
        den = jnp.sum(s, axis=2) + w_inter * jnp.einsum('bthk,bhk->bth', qq, n)
        h = num / jnp.maximum(jnp.abs(den), jnp.exp(-m_t))[..., None]
        w_end = d_mat[:, -1]
        f_end = w_inter[:, -1]
        kw = kk * w_end[..., None]
        c_new = f_end[..., None, None] * c + jnp.einsum('bshk,bshv->bhkv', kw, vv)
        n_new = f_end[..., None] * n + jnp.sum(kw, axis=1)
        return (c_new, n_new, m_t[:, -1]), h

    carry0 = (c0.astype(jnp.float32), n0.astype(jnp.float32), m0.astype(jnp.float32))
    (c1, n1, m1), hs = lax.scan(step, carry0, (chunks(q), chunks(k), chunks(v), chunks(i_pre), chunks(log_f)))
    return jnp.swapaxes(hs, 0, 1).reshape(bsz, seq, nh, dh), (c1, n1, m1)


def _mlstm_mixer(xn, state, w_up, conv_w, conv_b, w_qk, w_vog, b_gate, norm_g, w_down):
    c0, n0, m0, conv_buf = state
    bsz, seq, _ = xn.shape
    xm = xn @ w_up
    xc, conv_new = _causal_conv(xm, conv_buf, conv_w, conv_b)
    xc = jax.nn.silu(xc)
    q, k = jnp.split(xc @ w_qk, 2, axis=-1)
    vog = xm @ w_vog
    v = vog[..., :ML_INNER]
    o_gate = jax.nn.sigmoid(vog[..., ML_INNER:2 * ML_INNER])
    gates = vog[..., 2 * ML_INNER:].astype(jnp.float32) + b_gate
    i_pre = gates[..., :ML_HEADS]
    log_f = jax.nn.log_sigmoid(gates[..., ML_HEADS:])

    def heads(t):
        return t.reshape(bsz, seq, ML_HEADS, ML_HD).astype(jnp.float32)

    h, (c1, n1, m1) = _mlstm_chunkwise(heads(q) * ML_HD ** -0.5, heads(k), heads(v), i_pre, log_f, c0, n0, m0)
    h = h * lax.rsqrt(jnp.mean(h * h, axis=-1, keepdims=True) + NORM_EPS)
    h = h.reshape(bsz, seq, ML_INNER).astype(xn.dtype) * norm_g
    return (o_gate * h) @ w_down, (c1, n1, m1, conv_new)


def _stick_breaking(q, k, v, bias, q_pos0):
    bsz, sq, nh, dh = q.shape
    sk = k.shape[1]
    qb_len = math.gcd(sq, SB_QBLOCK)
    nb = sq // qb_len
    q_blocks = jnp.swapaxes(q.reshape(bsz, nb, qb_len, nh, dh), 0, 1)
    key_pos = jnp.arange(sk)
    scale = dh ** -0.5
    bias = bias.astype(jnp.float32)[None, :, None, None]

    def block(args):
        qblk, start = args
        z = jnp.einsum('bqhd,bkhd->bhqk', qblk, k) * scale + bias
        q_pos = q_pos0 + start + jnp.arange(qb_len)
        mask = key_pos[None, :] < q_pos[:, None]
        log_beta = jax.nn.log_sigmoid(z)
        log_rest = jnp.where(mask, jax.nn.log_sigmoid(-z), 0.0)
        after = lax.cumsum(log_rest, axis=3, reverse=True) - log_rest
        a = jnp.where(mask, jnp.exp(log_beta + after), 0.0)
        return jnp.einsum('bhqk,bkhd->bqhd', a, v)

    out = lax.map(block, (q_blocks, jnp.arange(nb) * qb_len))
    return jnp.swapaxes(out, 0, 1).reshape(bsz, sq, nh, dh)


def _sb_mixer(xn, past_kv, q_pos0, w_qkv, bias, w_o):
    bsz, seq, _ = xn.shape
    q, k, v = jnp.split(xn @ w_qkv, 3, axis=-1)
    shp = (bsz, seq, SB_HEADS, SB_HD)
    q, k, v = q.reshape(shp), k.reshape(shp), v.reshape(shp)
    if past_kv is None:
        k_all, v_all = k, v
    else:
        k_all = jnp.concatenate([past_kv[0].astype(k.dtype), k], axis=1)
        v_all = jnp.concatenate([past_kv[1].astype(v.dtype), v], axis=1)
    o = _stick_breaking(q.astype(jnp.float32), k_all.astype(jnp.float32), v_all.astype(jnp.float32), bias, q_pos0)
    return o.reshape(bsz, seq, D_MODEL).astype(xn.dtype) @ w_o, (k, v)


def _ssd_chunkwise(x, dt, a, bm, cm, h0):
    bsz, seq, nh, hp = x.shape
    ng, ns = bm.shape[2], bm.shape[3]
    r = nh // ng
    L = math.gcd(seq, SSD_CHUNK)
    nc = seq // L

    def chunks(t):
        return jnp.swapaxes(t.reshape(bsz, nc, L, *t.shape[2:]), 0, 1)

    xs = chunks(x.reshape(bsz, seq, ng, r, hp))
    dts = chunks(dt.reshape(bsz, seq, ng, r))
    a_g = a.reshape(ng, r)
    causal = jnp.tril(jnp.ones((L, L), dtype=bool))[None, :, :, None, None]

    def step(h, inp):
        xx, dd, bb, cc = inp
        cum = jnp.cumsum(dd * a_g, axis=1)
        seg = jnp.where(causal, cum[:, :, None] - cum[:, None], -jnp.inf)
        cb = jnp.einsum('btgn,bsgn->btsg', cc, bb)
        w = jnp.exp(seg) * cb[..., None] * dd[:, None]
        y = jnp.einsum('btsgr,bsgrp->btgrp', w, xx)
        y = y + jnp.exp(cum)[..., None] * jnp.einsum('btgn,bgrpn->btgrp', cc, h)
        w_end = jnp.exp(cum[:, -1:] - cum) * dd
        h_new = jnp.exp(cum[:, -1])[..., None, None] * h + jnp.einsum('bsgrp,bsgn->bgrpn', xx * w_end[..., None], bb)
        return h_new, y

    h0g = h0.astype(jnp.float32).reshape(bsz, ng, r, hp, ns)
    h1, ys = lax.scan(step, h0g, (xs, dts, chunks(bm), chunks(cm)))
    return jnp.swapaxes(ys, 0, 1).reshape(bsz, seq, nh, hp), h1.reshape(bsz, nh, hp, ns)


def _ssd_mixer(xn, state, w_in, conv_w, conv_b, dt_bias, a_log, d_skip, norm_g, w_out):
    h0, conv_buf = state
    bsz, seq, _ = xn.shape
    gn = SSD_GROUPS * SSD_STATE
    zxbcdt = xn @ w_in
    z = zxbcdt[..., :SSD_INNER]
    xbc = zxbcdt[..., SSD_INNER:SSD_INNER + SSD_CONV_CH]
    dt_raw = zxbcdt[..., SSD_INNER + SSD_CONV_CH:]
    xbc, conv_new = _causal_conv(xbc, conv_buf, conv_w, conv_b)
    xbc = jax.nn.silu(xbc).astype(jnp.float32)
    xs = xbc[..., :SSD_INNER].reshape(bsz, seq, SSD_HEADS, SSD_HD)
    bm = xbc[..., SSD_INNER:SSD_INNER + gn].reshape(bsz, seq, SSD_GROUPS, SSD_STATE)
    cm = xbc[..., SSD_INNER + gn:].reshape(bsz, seq, SSD_GROUPS, SSD_STATE)
    dt = jax.nn.softplus(dt_raw.astype(jnp.float32) + dt_bias.astype(jnp.float32))
    a = -jnp.exp(a_log.astype(jnp.float32))
    y, h1 = _ssd_chunkwise(xs, dt, a, bm, cm, h0)
    y = y + d_skip.astype(jnp.float32)[:, None] * xs
    y = y.reshape(bsz, seq, SSD_INNER) * jax.nn.silu(z.astype(jnp.float32))
    yg = y.reshape(bsz, seq, SSD_GROUPS, SSD_INNER // SSD_GROUPS)
    yg = yg * lax.rsqrt(jnp.mean(yg * yg, axis=-1, keepdims=True) + NORM_EPS)
    y = yg.reshape(bsz, seq, SSD_INNER).astype(xn.dtype) * norm_g
    return y @ w_out, (h1, conv_new)


def _memory_kv(mem, g, w_kv):
    mk, mv = jnp.split(_rmsnorm(mem, g) @ w_kv, 2, axis=-1)
    shp = mem.shape[:2] + (XA_HEADS, XA_HD)
    return mk.reshape(shp), mv.reshape(shp)


def _cross_attn(xn, mk, mv, w_q, w_o):
    bsz, seq, _ = xn.shape
    q = (xn @ w_q).reshape(bsz, seq, XA_HEADS, XA_HD).astype(jnp.float32)
    s = jnp.einsum('bshd,bmhd->bhsm', q, mk.astype(jnp.float32)) * XA_HD ** -0.5
    p = jax.nn.softmax(s, axis=-1)
    o = jnp.einsum('bhsm,bmhd->bshd', p, mv.astype(jnp.float32))
    return o.reshape(bsz, seq, D_MODEL).astype(xn.dtype) @ w_o


def _swiglu(xn, w_in, w_out):
    gate, up = jnp.split(xn @ w_in, 2, axis=-1)
    return (jax.nn.silu(gate) * up) @ w_out


def _trunk(x, mem_kv, ml_states, sb_past, ssd_states, q_pos0, p):
    ml_new, sb_new, ssd_new = [], [], []
    for layer in range(DEPTH):
        kind = LAYER_KINDS[layer]
        xn = _rmsnorm(x, p['norm_mix'][layer])
        if kind == 0:
            j = len(ml_new)
            out, st = _mlstm_mixer(xn, ml_states[j], p['ml_w_up'][j], p['ml_conv_w'][j], p['ml_conv_b'][j],
                                   p['ml_w_qk'][j], p['ml_w_vog'][j], p['ml_b_gate'][j], p['ml_norm'][j],
                                   p['ml_w_down'][j])
            ml_new.append(st)
        elif kind == 1:
            j = len(sb_new)
            past = None if sb_past is None else sb_past[j]
            out, st = _sb_mixer(xn, past, q_pos0, p['sb_w_qkv'][j], p['sb_bias'][j], p['sb_w_o'][j])
            sb_new.append(st)
        else:
            j = len(ssd_new)
            out, st = _ssd_mixer(xn, ssd_states[j], p['ssd_w_in'][j], p['ssd_conv_w'][j], p['ssd_conv_b'][j],
                                 p['ssd_dt_bias'][j], p['ssd_a_log'][j], p['ssd_d'][j], p['ssd_norm'][j],
                                 p['ssd_w_out'][j])
            ssd_new.append(st)
        x = x + out
        mk, mv = mem_kv[layer]
        x = x + _cross_attn(_rmsnorm(x, p['norm_xa'][layer]), mk, mv, p['xa_w_q'][layer], p['xa_w_o'][layer])
        x = x + _swiglu(_rmsnorm(x, p['norm_ffn'][layer]), p['ffn_w_in'][layer], p['ffn_w_out'][layer])
    return _rmsnorm(x, p['norm_final']), ml_new, sb_new, ssd_new


def setup_inputs(seed: int = 0) -> dict:
    key = jax.random.key(seed)
    ks = iter(jax.random.split(key, 64))
    f32 = jnp.float32

    def nrm(shape, scale=1.0):
        return jax.random.normal(next(ks), shape, f32) * scale

    def gain(shape):
        return 1.0 + nrm(shape, 0.02)

    n_pages = PAST_LEN // PAGE_SIZE
    n_used = DEC_BATCH * n_pages
    n_pool = n_used + max(1, n_used // 4)
    d = D_MODEL
    out_scale = 0.5

    inp = {}
    inp['x_prompt'] = nrm((BATCH, SEQ, d))
    inp['x_sample'] = nrm((DEC_BATCH, DEC_SEQ, d))
    inp['cache_mem_k'] = nrm((DEPTH, DEC_BATCH, MEM_LEN, XA_HEADS, XA_HD))
    inp['cache_mem_v'] = nrm((DEPTH, DEC_BATCH, MEM_LEN, XA_HEADS, XA_HD))
    inp['cache_sb_k'] = nrm((N_SB, n_pool, PAGE_SIZE, SB_HEADS, SB_HD))
    inp['cache_sb_v'] = nrm((N_SB, n_pool, PAGE_SIZE, SB_HEADS, SB_HD))
    inp['state_ml_c'] = nrm((N_ML, DEC_BATCH, ML_HEADS, ML_HD, ML_HD))
    inp['state_ml_n'] = nrm((N_ML, DEC_BATCH, ML_HEADS, ML_HD))
    inp['state_ml_m'] = nrm((N_ML, DEC_BATCH, ML_HEADS))
    inp['state_ml_conv'] = nrm((N_ML, DEC_BATCH, ML_CONV - 1, ML_INNER))
    inp['state_ssd_h'] = nrm((N_SSD, DEC_BATCH, SSD_HEADS, SSD_HD, SSD_STATE), 0.5)
    inp['state_ssd_conv'] = nrm((N_SSD, DEC_BATCH, SSD_CONV - 1, SSD_CONV_CH))
    perm = jax.random.permutation(next(ks), n_pool)[:n_used]
    inp['page_table'] = perm.reshape(DEC_BATCH, n_pages).astype(jnp.int32)
    inp['mem_prompt'] = nrm((BATCH, MEM_LEN, d))
    inp['norm_mix'] = gain((DEPTH, d))
    inp['norm_xa'] = gain((DEPTH, d))
    inp['norm_ffn'] = gain((DEPTH, d))
    inp['norm_mem'] = gain((DEPTH, d))
    inp['norm_final'] = gain((d,))
    inp['ml_w_up'] = nrm((N_ML, d, ML_INNER), d ** -0.5)
    inp['ml_conv_w'] = nrm((N_ML, ML_CONV, ML_INNER), ML_CONV ** -0.5)
    inp['ml_conv_b'] = nrm((N_ML, ML_INNER), 0.02)
    inp['ml_w_qk'] = nrm((N_ML, ML_INNER, 2 * ML_INNER), ML_INNER ** -0.5)
    inp['ml_w_vog'] = nrm((N_ML, ML_INNER, 2 * ML_INNER + 2 * ML_HEADS), ML_INNER ** -0.5)
    inp['ml_b_gate'] = jnp.concatenate(
        [nrm((N_ML, ML_HEADS), 0.1),
         jax.random.uniform(next(ks), (N_ML, ML_HEADS), f32, minval=3.0, maxval=6.0)], axis=-1)
    inp['ml_norm'] = gain((N_ML, ML_INNER))
    inp['ml_w_down'] = nrm((N_ML, ML_INNER, d), out_scale * ML_INNER ** -0.5)
    inp['sb_w_qkv'] = nrm((N_SB, d, 3 * d), d ** -0.5)
    inp['sb_bias'] = -jax.random.uniform(next(ks), (N_SB, SB_HEADS), f32, minval=4.0, maxval=9.0)
    inp['sb_w_o'] = nrm((N_SB, d, d), out_scale * d ** -0.5)
    inp['ssd_w_in'] = nrm((N_SSD, d, SSD_IN_DIM), d ** -0.5)
    inp['ssd_conv_w'] = nrm((N_SSD, SSD_CONV, SSD_CONV_CH), SSD_CONV ** -0.5)
    inp['ssd_conv_b'] = nrm((N_SSD, SSD_CONV_CH), 0.02)
    dt0 = jnp.exp(jax.random.uniform(next(ks), (N_SSD, SSD_HEADS), f32,
                                     minval=math.log(1e-3), maxval=math.log(1e-1)))
    inp['ssd_dt_bias'] = dt0 + jnp.log(-jnp.expm1(-dt0))
    inp['ssd_a_log'] = jnp.log(jax.random.uniform(next(ks), (N_SSD, SSD_HEADS), f32, minval=1.0, maxval=16.0))
    inp['ssd_d'] = 1.0 + nrm((N_SSD, SSD_HEADS), 0.02)
    inp['ssd_norm'] = gain((N_SSD, SSD_INNER))
    inp['ssd_w_out'] = nrm((N_SSD, SSD_INNER, d), out_scale * SSD_INNER ** -0.5)
    inp['xa_w_q'] = nrm((DEPTH, d, d), d ** -0.5)
    inp['xa_w_kv'] = nrm((DEPTH, d, 2 * d), d ** -0.5)
    inp['xa_w_o'] = nrm((DEPTH, d, d), out_scale * d ** -0.5)
    inp['ffn_w_in'] = nrm((DEPTH, d, 2 * FFN_HIDDEN), d ** -0.5)
    inp['ffn_w_out'] = nrm((DEPTH, FFN_HIDDEN, d), out_scale * FFN_HIDDEN ** -0.5)
    return inp


def reference(x_prompt, x_sample, cache_mem_k, cache_mem_v, cache_sb_k, cache_sb_v,
              state_ml_c, state_ml_n, state_ml_m, state_ml_conv, state_ssd_h, state_ssd_conv,
              page_table, mem_prompt,
              norm_mix, norm_xa, norm_ffn, norm_mem, norm_final,
              ml_w_up, ml_conv_w, ml_conv_b, ml_w_qk, ml_w_vog, ml_b_gate, ml_norm, ml_w_down,
              sb_w_qkv, sb_bias, sb_w_o,
              ssd_w_in, ssd_conv_w, ssd_conv_b, ssd_dt_bias, ssd_a_log, ssd_d, ssd_norm, ssd_w_out,
              xa_w_q, xa_w_kv, xa_w_o, ffn_w_in, ffn_w_out):
    p = dict(norm_mix=norm_mix, norm_xa=norm_xa, norm_ffn=norm_ffn, norm_final=norm_final,
             ml_w_up=ml_w_up, ml_conv_w=ml_conv_w, ml_conv_b=ml_conv_b, ml_w_qk=ml_w_qk,
             ml_w_vog=ml_w_vog, ml_b_gate=ml_b_gate, ml_norm=ml_norm, ml_w_down=ml_w_down,
             sb_w_qkv=sb_w_qkv, sb_bias=sb_bias, sb_w_o=sb_w_o,
             ssd_w_in=ssd_w_in, ssd_conv_w=ssd_conv_w, ssd_conv_b=ssd_conv_b, ssd_dt_bias=ssd_dt_bias,
             ssd_a_log=ssd_a_log, ssd_d=ssd_d, ssd_norm=ssd_norm, ssd_w_out=ssd_w_out,
             xa_w_q=xa_w_q, xa_w_o=xa_w_o, ffn_w_in=ffn_w_in, ffn_w_out=ffn_w_out)
    f32 = jnp.float32

    bp = x_prompt.shape[0]
    mem_kv_prompt = [_memory_kv(mem_prompt, norm_mem[l], xa_w_kv[l]) for l in range(DEPTH)]
    ml_zero = [(jnp.zeros((bp, ML_HEADS, ML_HD, ML_HD), f32), jnp.zeros((bp, ML_HEADS, ML_HD), f32),
                jnp.zeros((bp, ML_HEADS), f32), jnp.zeros((bp, ML_CONV - 1, ML_INNER), x_prompt.dtype))
               for _ in range(N_ML)]
    ssd_zero = [(jnp.zeros((bp, SSD_HEADS, SSD_HD, SSD_STATE), f32),
                 jnp.zeros((bp, SSD_CONV - 1, SSD_CONV_CH), x_prompt.dtype)) for _ in range(N_SSD)]
    y_prompt, ml_p, sb_p, ssd_p = _trunk(x_prompt, mem_kv_prompt, ml_zero, None, ssd_zero, 0, p)

    past_len = page_table.shape[1] * cache_sb_k.shape[2]
    mem_kv_sample = [(cache_mem_k[l], cache_mem_v[l]) for l in range(DEPTH)]
    sb_past = [(_gather_pages(cache_sb_k[j], page_table), _gather_pages(cache_sb_v[j], page_table))
               for j in range(N_SB)]
    ml_cache = [(state_ml_c[j], state_ml_n[j], state_ml_m[j], state_ml_conv[j]) for j in range(N_ML)]
    ssd_cache = [(state_ssd_h[j], state_ssd_conv[j]) for j in range(N_SSD)]
    y_sample, ml_s, sb_s, ssd_s = _trunk(x_sample, mem_kv_sample, ml_cache, sb_past, ssd_cache, past_len, p)

    return (y_prompt, y_sample,
            _stack(mem_kv_prompt, 0), _stack(mem_kv_prompt, 1),
            _stack(sb_p, 0), _stack(sb_p, 1),
            _stack(ml_p, 0), _stack(ml_p, 1), _stack(ml_p, 2), _stack(ml_p, 3),
            _stack(ssd_p, 0), _stack(ssd_p, 1),
            _stack(sb_s, 0), _stack(sb_s, 1),
            _stack(ml_s, 0), _stack(ml_s, 1), _stack(ml_s, 2), _stack(ml_s, 3),
            _stack(ssd_s, 0), _stack(ssd_s, 1))
```

```python
import functools
import math

import jax
import jax.numpy as jnp
from jax import lax
from jax.experimental import pallas as pl
from jax.experimental.pallas import tpu as pltpu

F32 = jnp.float32
BF16 = jnp.bfloat16
NORM_EPS = 1e-6
NEG = -1e30
VMEM_LIMIT_BYTES = 56 * 1024 * 1024
WEIGHT_TILE_BYTES = 8 * 1024 * 1024
ROW_TILE = 256
SEQ_PAD = 128
ML_CHUNK = 256
SSD_CHUNK = 128
SB_BLOCK = 256
XA_QBLOCK = 256


def _params(sem):
    return pltpu.CompilerParams(dimension_semantics=sem, vmem_limit_bytes=VMEM_LIMIT_BYTES)


def _dot(a, b):
    return jnp.dot(a, b, preferred_element_type=F32)


def _dot_nt(a, b):
    return lax.dot_general(a, b, (((1,), (1,)), ((), ())), preferred_element_type=F32)


def _dot_tn(a, b):
    return lax.dot_general(a, b, (((0,), (0,)), ((), ())), preferred_element_type=F32)


def _split3(x):
    hi = x.astype(BF16)
    r = x - hi.astype(F32)
    mid = r.astype(BF16)
    lo = (r - mid.astype(F32)).astype(BF16)
    return hi, mid, lo


def _split2(x):
    hi = x.astype(BF16)
    lo = (x - hi.astype(F32)).astype(BF16)
    return hi, lo


def _cumsum_rows(x):
    n = x.shape[0]
    t = lax.broadcasted_iota(jnp.int32, (n, n), 0)
    s = lax.broadcasted_iota(jnp.int32, (n, n), 1)
    m = jnp.where(s <= t, 1.0, 0.0).astype(BF16)
    hi, mid, lo = _split3(x)
    return _dot(m, hi) + _dot(m, mid) + _dot(m, lo)


def _cumsum_lanes(x):
    n = x.shape[1]
    j = lax.broadcasted_iota(jnp.int32, (n, n), 0)
    s = lax.broadcasted_iota(jnp.int32, (n, n), 1)
    m = jnp.where(j <= s, 1.0, 0.0).astype(BF16)
    hi, mid, lo = _split3(x)
    return _dot(hi, m) + _dot(mid, m) + _dot(lo, m)


def _softplus(x):
    return jnp.maximum(x, 0.0) + jnp.log1p(jnp.exp(-jnp.abs(x)))


def _pad_front(x, pad):
    if pad == 0:
        return x
    return jnp.concatenate([jnp.zeros((pad, x.shape[1]), x.dtype), x], axis=0)


def _expand_lanes(cols, width):
    n, k = cols.shape
    lane = lax.broadcasted_iota(jnp.int32, (n, k * width), 1)
    out = jnp.broadcast_to(cols[:, k - 1:k], (n, k * width))
    for r in range(k - 2, -1, -1):
        out = jnp.where(lane < (r + 1) * width, cols[:, r:r + 1], out)
    return out


def _expand_rows(rows, height):
    k, n = rows.shape
    sub = lax.broadcasted_iota(jnp.int32, (k * height, n), 0)
    out = jnp.broadcast_to(rows[k - 1:k, :], (k * height, n))
    for r in range(k - 2, -1, -1):
        out = jnp.where(sub < (r + 1) * height, rows[r:r + 1, :], out)
    return out


def _mm_body(*refs, has_norm, swiglu, has_bias, has_res):
    it = iter(refs)
    x_ref = next(it)
    g_ref = next(it) if has_norm else None
    w_ref = next(it)
    w2_ref = next(it) if swiglu else None
    b_ref = next(it) if has_bias else None
    r_ref = next(it) if has_res else None
    o_ref = next(it)
    x = x_ref[...]
    if has_norm:
        x32 = x.astype(F32)
        x = x32 * lax.rsqrt(jnp.mean(x32 * x32, axis=-1, keepdims=True) + NORM_EPS) * g_ref[...]
    xb = x.astype(BF16)
    acc = _dot(xb, w_ref[...])
    if swiglu:
        acc = acc * jax.nn.sigmoid(acc) * _dot(xb, w2_ref[...])
    if has_bias:
        acc = acc + b_ref[...]
    if has_res:
        acc = r_ref[...] + acc
    o_ref[...] = acc.astype(o_ref.dtype)


def _col_tile(k, n, n_weights):
    tn = n
    while k * tn * 2 * n_weights > WEIGHT_TILE_BYTES and tn % 256 == 0:
        tn //= 2
    return tn


def _matmul(x, w, *, g=None, w2=None, bias=None, res=None, out_dtype=F32):
    t, k = x.shape
    n = w.shape[1]
    tm = min(ROW_TILE, t)
    assert t % tm == 0
    tn = _col_tile(k, n, 2 if w2 is not None else 1)
    grid = (n // tn, t // tm)
    ops = [x]
    specs = [pl.BlockSpec((tm, k), lambda j, i: (i, 0))]
    if g is not None:
        ops.append(g.reshape(1, k))
        specs.append(pl.BlockSpec((1, k), lambda j, i: (0, 0)))
    ops.append(w)
    specs.append(pl.BlockSpec((k, tn), lambda j, i: (0, j)))
    if w2 is not None:
        ops.append(w2)
        specs.append(pl.BlockSpec((k, tn), lambda j, i: (0, j)))
    if bias is not None:
        ops.append(bias.reshape(1, n))
        specs.append(pl.BlockSpec((1, tn), lambda j, i: (0, j)))
    if res is not None:
        ops.append(res)
        specs.append(pl.BlockSpec((tm, tn), lambda j, i: (i, j)))
    body = functools.partial(_mm_body, has_norm=g is not None, swiglu=w2 is not None,
                             has_bias=bias is not None, has_res=res is not None)
    return pl.pallas_call(
        body, grid=grid, in_specs=specs,
        out_specs=pl.BlockSpec((tm, tn), lambda j, i: (i, j)),
        out_shape=jax.ShapeDtypeStruct((t, n), out_dtype),
        compiler_params=_params(("parallel", "parallel")), name="matmul",
    )(*ops)


def _mlstm_body(*refs, lr, lp, has_state, dh):
    if has_state:
        (q_ref, k_ref, v_ref, o_ref, gc_ref, gr_ref, ng_ref, c0_ref, n0_ref, m0_ref,
         hg_ref, c1_ref, n1_ref, m1_ref, c_s, n_s, m_s) = refs
    else:
        (q_ref, k_ref, v_ref, o_ref, gc_ref, gr_ref, ng_ref,
         hg_ref, c1_ref, n1_ref, m1_ref, c_s, n_s, m_s) = refs
    ci = pl.program_id(2)
    pad = lp - lr

    @pl.when(ci == 0)
    def _():
        if has_state:
            c_s[...] = c0_ref[0, 0]
            n_s[...] = n0_ref[0, 0]
            m_s[...] = m0_ref[0, 0]
        else:
            c_s[...] = jnp.zeros_like(c_s)
            n_s[...] = jnp.zeros_like(n_s)
            m_s[...] = jnp.zeros_like(m_s)

    q = _pad_front(q_ref[...], pad) * (dh ** -0.5)
    k = _pad_front(k_ref[...], pad)
    v = _pad_front(v_ref[...], pad)
    gc = gc_ref[0, 0]
    gr = gr_ref[0, 0]
    ii_c, ii_r = gc[:, 0:1], gr[0:1, :]
    ff_c, ff_r = -_softplus(-gc[:, 1:2]), -_softplus(-gr[1:2, :])
    if pad:
        ok_c = lax.broadcasted_iota(jnp.int32, (lp, 1), 0) >= pad
        ok_r = lax.broadcasted_iota(jnp.int32, (1, lp), 1) >= pad
        ii_c, ii_r = jnp.where(ok_c, ii_c, NEG), jnp.where(ok_r, ii_r, NEG)
        ff_c, ff_r = jnp.where(ok_c, ff_c, 0.0), jnp.where(ok_r, ff_r, 0.0)
    b_c = _cumsum_rows(ff_c)
    b_r = _cumsum_lanes(ff_r)
    m_prev = m_s[...]
    t_i = lax.broadcasted_iota(jnp.int32, (lp, lp), 0)
    s_i = lax.broadcasted_iota(jnp.int32, (lp, lp), 1)
    log_d = jnp.where(s_i <= t_i, b_c + (ii_r - b_r), NEG)
    log_inter = b_c + m_prev
    m_t = jnp.maximum(log_inter, jnp.max(log_d, axis=1, keepdims=True))
    d_mat = jnp.exp(log_d - m_t)
    w_inter = jnp.exp(log_inter - m_t)
    qb, kb, vb = q.astype(BF16), k.astype(BF16), v.astype(BF16)
    s = _dot_nt(qb, kb) * d_mat
    c = c_s[...]
    n = n_s[...]
    num = _dot(s.astype(BF16), vb) + w_inter * _dot(qb, c.astype(BF16))
    den = jnp.sum(s, axis=1, keepdims=True) + w_inter * jnp.sum(q * n, axis=1, keepdims=True)
    h = num / jnp.maximum(jnp.abs(den), jnp.exp(-m_t))
    h = h * lax.rsqrt(jnp.mean(h * h, axis=1, keepdims=True) + NORM_EPS) * ng_ref[...]
    hg_ref[...] = (jax.nn.sigmoid(o_ref[...]) * h[pad:, :]).astype(hg_ref.dtype)

    b_last = b_c[lp - 1:lp, :]
    m_last = m_t[lp - 1:lp, :]
    w_end = jnp.exp(b_last - b_c + ii_c - m_last)
    f_end = jnp.exp(b_last + m_prev - m_last)
    kw = k * w_end
    c_s[...] = f_end * c + _dot_tn(kw.astype(BF16), vb)
    n_s[...] = f_end * n + jnp.sum(kw, axis=0, keepdims=True)
    m_s[...] = m_last

    @pl.when(ci == pl.num_programs(2) - 1)
    def _():
        c1_ref[0, 0] = c_s[...]
        n1_ref[0, 0] = n_s[...]
        m1_ref[0, 0] = m_s[...]


def _mlstm_core(qk, vo, gates, norm_g, bsz, seq, heads, state, out_dtype):
    inner = qk.shape[1] // 2
    dh = inner // heads
    lr = math.gcd(seq, ML_CHUNK)
    lp = lr if lr % SEQ_PAD == 0 else SEQ_PAD
    nc = seq // lr
    pad = lp - lr
    gi = gates[:, :heads].reshape(bsz, seq, heads)
    gf = gates[:, heads:2 * heads].reshape(bsz, seq, heads)
    g = jnp.stack([gi, gf], axis=-1)
    g = g.reshape(bsz, nc, lr, heads, 2)
    g = jnp.pad(g, ((0, 0), (0, 0), (pad, 0), (0, 0), (0, 0)))
    gc = jnp.transpose(g, (3, 0, 1, 2, 4)).reshape(heads, bsz, nc * lp, 2)
    gr = jnp.transpose(g, (3, 0, 4, 1, 2)).reshape(heads, bsz, 2, nc * lp)
    has_state = state is not None
    ops = [qk, qk, vo, vo, gc, gr, norm_g.reshape(1, inner)]
    specs = [
        pl.BlockSpec((lr, dh), lambda b, h, c: (b * nc + c, h)),
        pl.BlockSpec((lr, dh), lambda b, h, c: (b * nc + c, heads + h)),
        pl.BlockSpec((lr, dh), lambda b, h, c: (b * nc + c, h)),
        pl.BlockSpec((lr, dh), lambda b, h, c: (b * nc + c, heads + h)),
        pl.BlockSpec((1, 1, lp, 2), lambda b, h, c: (h, b, c, 0)),
        pl.BlockSpec((1, 1, 2, lp), lambda b, h, c: (h, b, 0, c)),
        pl.BlockSpec((1, dh), lambda b, h, c: (0, h)),
    ]
    if has_state:
        c0, n0, m0 = state
        ops += [c0, n0.reshape(bsz, heads, 1, dh), m0.reshape(bsz, heads, 1, 1)]
        specs += [
            pl.BlockSpec((1, 1, dh, dh), lambda b, h, c: (b, h, 0, 0)),
            pl.BlockSpec((1, 1, 1, dh), lambda b, h, c: (b, h, 0, 0)),
            pl.BlockSpec((1, 1, 1, 1), lambda b, h, c: (b, h, 0, 0)),
        ]
    body = functools.partial(_mlstm_body, lr=lr, lp=lp, has_state=has_state, dh=dh)
    hg, c1, n1, m1 = pl.pallas_call(
        body, grid=(bsz, heads, nc), in_specs=specs,
        out_specs=[
            pl.BlockSpec((lr, dh), lambda b, h, c: (b * nc + c, h)),
            pl.BlockSpec((1, 1, dh, dh), lambda b, h, c: (b, h, 0, 0)),
            pl.BlockSpec((1, 1, 1, dh), lambda b, h, c: (b, h, 0, 0)),
            pl.BlockSpec((1, 1, 1, 1), lambda b, h, c: (b, h, 0, 0)),
        ],
        out_shape=[
            jax.ShapeDtypeStruct((bsz * seq, inner), out_dtype),
            jax.ShapeDtypeStruct((bsz, heads, dh, dh), F32),
            jax.ShapeDtypeStruct((bsz, heads, 1, dh), F32),
            jax.ShapeDtypeStruct((bsz, heads, 1, 1), F32),
        ],
        scratch_shapes=[pltpu.VMEM((dh, dh), F32), pltpu.VMEM((1, dh), F32), pltpu.VMEM((1, 1), F32)],
        compiler_params=_params(("parallel", "parallel", "arbitrary")), name="mlstm_core",
    )(*ops)
    return hg, c1, n1.reshape(bsz, heads, dh), m1.reshape(bsz, heads)


def _ssd_body(*refs, lr, lp, has_state, hpg, hd):
    if has_state:
        (x_ref, bm_ref, cm_ref, z_ref, dtc_ref, dtr_ref, pr_ref, pc_ref, ng_ref, h0_ref,
         y_ref, h1_ref, h_s) = refs
    else:
        (x_ref, bm_ref, cm_ref, z_ref, dtc_ref, dtr_ref, pr_ref, pc_ref, ng_ref,
         y_ref, h1_ref, h_s) = refs
    ci = pl.program_id(2)
    pad = lp - lr

    @pl.when(ci == 0)
    def _():
        if has_state:
            h_s[...] = h0_ref[0, 0]
        else:
            h_s[...] = jnp.zeros_like(h_s)

    x = _pad_front(x_ref[...], pad)
    bm = _pad_front(bm_ref[...], pad).astype(BF16)
    cm = _pad_front(cm_ref[...], pad).astype(BF16)
    pr = pr_ref[0]
    pc = pc_ref[0]
    dd_c = _softplus(dtc_ref[0, 0] + pr[0:1, :])
    dd_r = _softplus(dtr_ref[0, 0] + pc[:, 0:1])
    if pad:
        dd_c = jnp.where(lax.broadcasted_iota(jnp.int32, (lp, hpg), 0) >= pad, dd_c, 0.0)
        dd_r = jnp.where(lax.broadcasted_iota(jnp.int32, (hpg, lp), 1) >= pad, dd_r, 0.0)
    cum_c = _cumsum_rows(dd_c * pr[1:2, :])
    cum_r = _cumsum_lanes(dd_r * pc[:, 1:2])
    cb = _dot_nt(cm, bm)
    t_i = lax.broadcasted_iota(jnp.int32, (lp, lp), 0)
    s_i = lax.broadcasted_iota(jnp.int32, (lp, lp), 1)
    tri = s_i <= t_i
    head_of_lane = lax.broadcasted_iota(jnp.int32, (1, hpg * hd), 1) // hd
    y = jnp.zeros((lp, hpg * hd), F32)
    for r in range(hpg):
        seg = jnp.where(tri, cum_c[:, r:r + 1] - cum_r[r:r + 1, :], NEG)
        w = jnp.exp(seg) * cb * dd_r[r:r + 1, :]
        xr = jnp.where(head_of_lane == r, x, 0.0).astype(BF16)
        y = y + _dot(w.astype(BF16), xr)
    h = h_s[...]
    y = y + _expand_lanes(jnp.exp(cum_c), hd) * _dot_nt(cm, h.astype(BF16))
    y = y + _expand_lanes(pr[2:3, :], hd) * x
    z = z_ref[...]
    yv = y[pad:, :] * (z * jax.nn.sigmoid(z))
    yv = yv * lax.rsqrt(jnp.mean(yv * yv, axis=1, keepdims=True) + NORM_EPS) * ng_ref[...]
    y_ref[...] = yv.astype(y_ref.dtype)

    w_end = jnp.exp(cum_c[lp - 1:lp, :] - cum_c) * dd_c
    xw = x * _expand_lanes(w_end, hd)
    decay = _expand_rows(jnp.exp(cum_r[:, lp - 1:lp]), hd)
    h_s[...] = decay * h + _dot_tn(xw.astype(BF16), bm)

    @pl.when(ci == pl.num_programs(2) - 1)
    def _():
        h1_ref[0, 0] = h_s[...]


def _ssd_core(zx, xbc, dt_raw, dt_bias, a_neg, d_skip, norm_g, bsz, seq, groups, nstate, hd, h0, out_dtype):
    inner = norm_g.shape[0]
    heads = inner // hd
    hpg = heads // groups
    gw = hpg * hd
    lr = math.gcd(seq, SSD_CHUNK)
    lp = lr if lr % SEQ_PAD == 0 else SEQ_PAD
    nc = seq // lr
    pad = lp - lr
    d = dt_raw[:, :heads].reshape(bsz, nc, lr, groups, hpg)
    d = jnp.pad(d, ((0, 0), (0, 0), (pad, 0), (0, 0), (0, 0)))
    dtc = jnp.transpose(d, (3, 0, 1, 2, 4)).reshape(groups, bsz, nc * lp, hpg)
    dtr = jnp.transpose(d, (3, 0, 4, 1, 2)).reshape(groups, bsz, hpg, nc * lp)
    par = jnp.stack([dt_bias, a_neg, d_skip]).astype(F32).reshape(3, groups, hpg)
    pr = jnp.transpose(par, (1, 0, 2))
    pc = jnp.transpose(par, (1, 2, 0))
    b_off = inner // nstate
    c_off = b_off + groups
    has_state = h0 is not None
    ops = [xbc, xbc, xbc, zx, dtc, dtr, pr, pc, norm_g.reshape(1, inner)]
    specs = [
        pl.BlockSpec((lr, gw), lambda b, g, c: (b * nc + c, g)),
        pl.BlockSpec((lr, nstate), lambda b, g, c: (b * nc + c, b_off + g)),
        pl.BlockSpec((lr, nstate), lambda b, g, c: (b * nc + c, c_off + g)),
        pl.BlockSpec((lr, gw), lambda b, g, c: (b * nc + c, g)),
        pl.BlockSpec((1, 1, lp, hpg), lambda b, g, c: (g, b, c, 0)),
        pl.BlockSpec((1, 1, hpg, lp), lambda b, g, c: (g, b, 0, c)),
        pl.BlockSpec((1, 3, hpg), lambda b, g, c: (g, 0, 0)),
        pl.BlockSpec((1, hpg, 3), lambda b, g, c: (g, 0, 0)),
        pl.BlockSpec((1, gw), lambda b, g, c: (0, g)),
    ]
    if has_state:
        ops.append(h0.reshape(bsz, groups, gw, nstate))
        specs.append(pl.BlockSpec((1, 1, gw, nstate), lambda b, g, c: (b, g, 0, 0)))
    body = functools.partial(_ssd_body, lr=lr, lp=lp, has_state=has_state, hpg=hpg, hd=hd)
    y, h1 = pl.pallas_call(
        body, grid=(bsz, groups, nc), in_specs=specs,
        out_specs=[
            pl.BlockSpec((lr, gw), lambda b, g, c: (b * nc + c, g)),
            pl.BlockSpec((1, 1, gw, nstate), lambda b, g, c: (b, g, 0, 0)),
        ],
        out_shape=[
            jax.ShapeDtypeStruct((bsz * seq, inner), out_dtype),
            jax.ShapeDtypeStruct((bsz, groups, gw, nstate), F32),
        ],
        scratch_shapes=[pltpu.VMEM((gw, nstate), F32)],
        compiler_params=_params(("parallel", "parallel", "arbitrary")), name="ssd_core",
    )(*ops)
    return y, h1.reshape(bsz, heads, hd, nstate)


def _sb_weights(z, run, tri_after, mask, after_axis):
    lb = -_softplus(-z)
    lr = lb - z
    if mask is not None:
        lr = jnp.where(mask, lr, 0.0)
    hi, lo = _split2(lr)
    if after_axis == 1:
        local = _dot(hi, tri_after) + _dot(lo, tri_after)
        total = local[:, 0:1] + lr[:, 0:1]
    else:
        local = _dot(tri_after, hi) + _dot(tri_after, lo)
        total = local[0:1, :] + lr[0:1, :]
    a = jnp.exp(lb + (run + local))
    if mask is not None:
        a = jnp.where(mask, a, 0.0)
    return a, run + total


def _sbp_body(bias_ref, q_ref, k_ref, v_ref, o_ref, *, blk, hd):
    h = pl.program_id(1)
    qi = pl.program_id(2)
    bias = bias_ref[h]
    scale = hd ** -0.5
    qb = q_ref[...].astype(BF16)
    j_i = lax.broadcasted_iota(jnp.int32, (blk, blk), 0)
    s_i = lax.broadcasted_iota(jnp.int32, (blk, blk), 1)
    tri_after = jnp.where(j_i > s_i, 1.0, 0.0).astype(BF16)
    strict = s_i < j_i

    def block(kb, run, acc, mask):
        rows = pl.ds(pl.multiple_of(kb * blk, blk), blk)
        kk = k_ref[rows, :].astype(BF16)
        vv = v_ref[rows, :].astype(BF16)
        z = _dot_nt(qb, kk) * scale + bias
        a, run = _sb_weights(z, run, tri_after, mask, 1)
        return run, acc + _dot(a.astype(BF16), vv)

    run, acc = block(qi, jnp.zeros((blk, 1), F32), jnp.zeros((blk, hd), F32), strict)

    def step(j, carry):
        return block(qi - 1 - j, carry[0], carry[1], None)

    run, acc = lax.fori_loop(0, qi, step, (run, acc))
    o_ref[...] = acc.astype(o_ref.dtype)


def _sb_prompt(qkv, bias, bsz, seq, heads, hd, out_dtype):
    blk = math.gcd(seq, SB_BLOCK)
    nq = seq // blk
    body = functools.partial(_sbp_body, blk=blk, hd=hd)
    return pl.pallas_call(
        body, grid=(bsz, heads, nq),
        in_specs=[
            pl.BlockSpec(memory_space=pltpu.SMEM),
            pl.BlockSpec((blk, hd), lambda b, h, i: (b * nq + i, h)),
            pl.BlockSpec((seq, hd), lambda b, h, i: (b, heads + h)),
            pl.BlockSpec((seq, hd), lambda b, h, i: (b, 2 * heads + h)),
        ],
        out_specs=pl.BlockSpec((blk, hd), lambda b, h, i: (b * nq + i, h)),
        out_shape=jax.ShapeDtypeStruct((bsz * seq, heads * hd), out_dtype),
        compiler_params=_params(("parallel", "parallel", "arbitrary")), name="sb_prompt",
    )(bias.astype(F32), qkv, qkv, qkv)


def _sbs_body(pt_ref, q_ref, kn_ref, vn_ref, kp_ref, vp_ref, bias_ref, o_ref, run_s, acc_s, qbd_s,
              *, sq, heads, hd, page):
    del pt_ref
    p = pl.program_id(1)
    width = heads * hd
    cols = qbd_s.shape[0]
    scale = hd ** -0.5

    def own_head():
        row_i = lax.broadcasted_iota(jnp.int32, (cols, width), 0)
        lane_i = lax.broadcasted_iota(jnp.int32, (cols, width), 1)
        return (row_i // sq) == (lane_i // hd)

    j_i = lax.broadcasted_iota(jnp.int32, (page, page), 1)
    s_i = lax.broadcasted_iota(jnp.int32, (page, page), 0)
    tri_after = jnp.where(j_i > s_i, 1.0, 0.0).astype(BF16)

    def process(kk, vv, mask):
        z = _dot_nt(kk.astype(BF16), qbd_s[...]) * scale + bias_ref[...]
        a, run = _sb_weights(z, run_s[...], tri_after, mask, 0)
        run_s[...] = run
        acc_s[...] += _dot_tn(a.astype(BF16), vv.astype(BF16))

    @pl.when(p == 0)
    def _():
        run_s[...] = jnp.zeros_like(run_s)
        acc_s[...] = jnp.zeros_like(acc_s)
        q = q_ref[...]
        tiled = jnp.concatenate([q] * heads + [jnp.zeros((cols - heads * sq, width), F32)], axis=0)
        qbd_s[...] = jnp.where(own_head(), tiled, 0.0).astype(BF16)
        key_i = lax.broadcasted_iota(jnp.int32, (page, cols), 0)
        qry_i = lax.broadcasted_iota(jnp.int32, (page, cols), 1) % sq
        zeros = jnp.zeros((page - sq, width), F32)
        process(jnp.concatenate([kn_ref[...], zeros], axis=0),
                jnp.concatenate([vn_ref[...], zeros], axis=0), key_i < qry_i)

    @pl.when(p > 0)
    def _():
        process(kp_ref[0], vp_ref[0], None)

    @pl.when(p == pl.num_programs(1) - 1)
    def _():
        acc = jnp.where(own_head(), acc_s[...], 0.0)
        out = acc[0:sq, :]
        for h in range(1, heads):
            out = out + acc[h * sq:(h + 1) * sq, :]
        o_ref[...] = out.astype(o_ref.dtype)


def _sb_sample(qkv, pool_k, pool_v, page_table, bias, bsz, sq, heads, hd, out_dtype):
    width = heads * hd
    n_pages = page_table.shape[1]
    page = pool_k.shape[1]
    cols = SEQ_PAD
    assert heads * sq <= cols and sq <= page
    kp = pool_k.reshape(pool_k.shape[0], page, width)
    vp = pool_v.reshape(pool_v.shape[0], page, width)
    bias_cols = jnp.pad(jnp.repeat(bias.astype(F32), sq), (0, cols - heads * sq)).reshape(1, cols)

    def page_idx(b, p, pt):
        return (pt[b, n_pages - 1 - jnp.maximum(p - 1, 0)], 0, 0)

    body = functools.partial(_sbs_body, sq=sq, heads=heads, hd=hd, page=page)
    grid_spec = pltpu.PrefetchScalarGridSpec(
        num_scalar_prefetch=1, grid=(bsz, n_pages + 1),
        in_specs=[
            pl.BlockSpec((sq, width), lambda b, p, pt: (b, 0)),
            pl.BlockSpec((sq, width), lambda b, p, pt: (b, 1)),
            pl.BlockSpec((sq, width), lambda b, p, pt: (b, 2)),
            pl.BlockSpec((1, page, width), page_idx),
            pl.BlockSpec((1, page, width), page_idx),
            pl.BlockSpec((1, cols), lambda b, p, pt: (0, 0)),
        ],
        out_specs=pl.BlockSpec((sq, width), lambda b, p, pt: (b, 0)),
        scratch_shapes=[pltpu.VMEM((1, cols), F32), pltpu.VMEM((cols, width), F32),
                        pltpu.VMEM((cols, width), BF16)],
    )
    return pl.pallas_call(
        body, grid_spec=grid_spec,
        out_shape=jax.ShapeDtypeStruct((bsz * sq, width), out_dtype),
        compiler_params=_params(("parallel", "arbitrary")), name="sb_sample",
    )(page_table, qkv, qkv, qkv, kp, vp, bias_cols)


def _xa_body(q_ref, mk_ref, mv_ref, o_ref, *, heads, hd):
    scale = hd ** -0.5
    for h in range(heads):
        cs = slice(h * hd, (h + 1) * hd)
        s = _dot_nt(q_ref[:, cs].astype(BF16), mk_ref[:, cs].astype(BF16)) * scale
        e = jnp.exp(s - jnp.max(s, axis=1, keepdims=True))
        pr = e / jnp.sum(e, axis=1, keepdims=True)
        o_ref[:, cs] = _dot(pr.astype(BF16), mv_ref[:, cs].astype(BF16)).astype(o_ref.dtype)


def _cross_attn(q, mk, mv, mk_col, mv_col, bsz, seq, mem_len, heads, hd, out_dtype):
    width = heads * hd
    tq = math.gcd(seq, XA_QBLOCK)
    nq = seq // tq
    body = functools.partial(_xa_body, heads=heads, hd=hd)
    return pl.pallas_call(
        body, grid=(bsz, nq),
        in_specs=[
            pl.BlockSpec((tq, width), lambda b, i: (b * nq + i, 0)),
            pl.BlockSpec((mem_len, width), lambda b, i: (b, mk_col)),
            pl.BlockSpec((mem_len, width), lambda b, i: (b, mv_col)),
        ],
        out_specs=pl.BlockSpec((tq, width), lambda b, i: (b * nq + i, 0)),
        out_shape=jax.ShapeDtypeStruct((bsz * seq, width), out_dtype),
        compiler_params=_params(("parallel", "parallel")), name="cross_attn",
    )(q, mk, mv)


def _causal_conv_silu(x, buf, w, b):
    width = w.shape[0]
    seq = x.shape[1]
    if buf is None:
        buf = jnp.zeros((x.shape[0], width - 1, x.shape[2]), x.dtype)
    xp = jnp.concatenate([buf, x], axis=1)
    y = b + xp[:, 0:seq] * w[0]
    for j in range(1, width):
        y = y + xp[:, j:j + seq] * w[j]
    return jax.nn.silu(y), xp[:, xp.shape[1] - (width - 1):]


def _pad_cols(w, n):
    return jnp.pad(w, ((0, 0), (0, n - w.shape[1])))


def _prep_weights(p):
    w = {}
    ml_inner = p['ml_w_up'].shape[2]
    ssd_inner = p['ssd_norm'].shape[1]
    ssd_heads = p['ssd_dt_bias'].shape[1]
    ssd_main = p['ssd_w_in'].shape[2] - ssd_heads
    ffn_hidden = p['ffn_w_out'].shape[1]
    w['ml_w_up'] = p['ml_w_up'].astype(BF16)
    w['ml_w_qk'] = p['ml_w_qk'].astype(BF16)
    w['ml_w_vo'] = p['ml_w_vog'][:, :, :2 * ml_inner].astype(BF16)
    w['ml_w_g'] = jnp.stack([_pad_cols(m[:, 2 * ml_inner:], 128) for m in p['ml_w_vog']]).astype(BF16)
    w['ml_b_g'] = jnp.stack([jnp.pad(b, (0, 128 - b.shape[0])) for b in p['ml_b_gate']]).astype(F32)
    w['ml_w_down'] = p['ml_w_down'].astype(BF16)
    w['sb_w_qkv'] = p['sb_w_qkv'].astype(BF16)
    w['sb_w_o'] = p['sb_w_o'].astype(BF16)
    w['ssd_w_main'] = p['ssd_w_in'][:, :, :ssd_main].astype(BF16)
    w['ssd_w_dt'] = jnp.stack([_pad_cols(m[:, ssd_main:], 128) for m in p['ssd_w_in']]).astype(BF16)
    w['ssd_w_out'] = p['ssd_w_out'].astype(BF16)
    w['ssd_a'] = -jnp.exp(p['ssd_a_log'].astype(F32))
    w['xa_w_q'] = p['xa_w_q'].astype(BF16)
    w['xa_w_kv'] = p['xa_w_kv'].astype(BF16)
    w['xa_w_o'] = p['xa_w_o'].astype(BF16)
    w['ffn_w_gate'] = p['ffn_w_in'][:, :, :ffn_hidden].astype(BF16)
    w['ffn_w_up'] = p['ffn_w_in'][:, :, ffn_hidden:].astype(BF16)
    w['ffn_w_out'] = p['ffn_w_out'].astype(BF16)
    del ssd_inner
    return w


def _trunk(x3, mem_kv, ml_states, sb_past, ssd_states, page_table, p, w, dims, act_dtype):
    bsz, seq, d = x3.shape
    depth = p['norm_mix'].shape[0]
    ml_heads, sb_heads, xa_heads, ssd_groups, ssd_state, ssd_hd, mem_len = dims
    x = x3.reshape(bsz * seq, d)
    ml_new, sb_new, ssd_new = [], [], []
    n_ml = n_sb = n_ssd = 0
    for layer in range(depth):
        kind = layer % 3
        if kind == 0:
            j = n_ml
            n_ml += 1
            inner = p['ml_w_up'].shape[2]
            xm = _matmul(x, w['ml_w_up'][j], g=p['norm_mix'][layer])
            st = None if ml_states is None else ml_states[j]
            xc, conv_new = _causal_conv_silu(xm.reshape(bsz, seq, inner), None if st is None else st[3],
                                             p['ml_conv_w'][j], p['ml_conv_b'][j])
            qk = _matmul(xc.reshape(bsz * seq, inner), w['ml_w_qk'][j])
            vo = _matmul(xm, w['ml_w_vo'][j])
            gates = _matmul(xm, w['ml_w_g'][j], bias=w['ml_b_g'][j])
            hg, c1, n1, m1 = _mlstm_core(qk, vo, gates, p['ml_norm'][j], bsz, seq, ml_heads,
                                         None if st is None else st[:3], act_dtype)
            x = _matmul(hg, w['ml_w_down'][j], res=x)
            ml_new.append((c1, n1, m1, conv_new))
        elif kind == 1:
            j = n_sb
            n_sb += 1
            hd = d // sb_heads
            qkv = _matmul(x, w['sb_w_qkv'][j], g=p['norm_mix'][layer])
            if sb_past is None:
                o = _sb_prompt(qkv, p['sb_bias'][j], bsz, seq, sb_heads, hd, act_dtype)
            else:
                o = _sb_sample(qkv, sb_past[j][0], sb_past[j][1], page_table, p['sb_bias'][j],
                               bsz, seq, sb_heads, hd, act_dtype)
            x = _matmul(o, w['sb_w_o'][j], res=x)
            shp = (bsz, seq, sb_heads, hd)
            sb_new.append((qkv[:, d:2 * d].reshape(shp), qkv[:, 2 * d:].reshape(shp)))
        else:
            j = n_ssd
            n_ssd += 1
            inner = p['ssd_norm'].shape[1]
            conv_ch = p['ssd_conv_w'].shape[2]
            zx = _matmul(x, w['ssd_w_main'][j], g=p['norm_mix'][layer])
            dt_raw = _matmul(x, w['ssd_w_dt'][j], g=p['norm_mix'][layer])
            st = None if ssd_states is None else ssd_states[j]
            xbc, conv_new = _causal_conv_silu(zx[:, inner:].reshape(bsz, seq, conv_ch),
                                              None if st is None else st[1],
                                              p['ssd_conv_w'][j], p['ssd_conv_b'][j])
            y, h1 = _ssd_core(zx, xbc.reshape(bsz * seq, conv_ch), dt_raw, p['ssd_dt_bias'][j], w['ssd_a'][j],
                              p['ssd_d'][j], p['ssd_norm'][j], bsz, seq, ssd_groups, ssd_state, ssd_hd,
                              None if st is None else st[0], act_dtype)
            x = _matmul(y, w['ssd_w_out'][j], res=x)
            ssd_new.append((h1, conv_new))
        mk, mv, mk_col, mv_col = mem_kv[layer]
        q = _matmul(x, w['xa_w_q'][layer], g=p['norm_xa'][layer])
        o = _cross_attn(q, mk, mv, mk_col, mv_col, bsz, seq, mem_len, xa_heads, d // xa_heads, act_dtype)
        x = _matmul(o, w['xa_w_o'][layer], res=x)
        hid = _matmul(x, w['ffn_w_gate'][layer], g=p['norm_ffn'][layer], w2=w['ffn_w_up'][layer],
                      out_dtype=act_dtype)
        x = _matmul(hid, w['ffn_w_out'][layer], res=x)
    y = _final_norm(x, p['norm_final'])
    return y.reshape(bsz, seq, d), ml_new, sb_new, ssd_new


def _norm_body(x_ref, g_ref, o_ref):
    x = x_ref[...]
    o_ref[...] = x * lax.rsqrt(jnp.mean(x * x, axis=-1, keepdims=True) + NORM_EPS) * g_ref[...]


def _final_norm(x, g):
    t, d = x.shape
    tm = min(ROW_TILE, t)
    return pl.pallas_call(
        _norm_body, grid=(t // tm,),
        in_specs=[pl.BlockSpec((tm, d), lambda i: (i, 0)), pl.BlockSpec((1, d), lambda i: (0, 0))],
        out_specs=pl.BlockSpec((tm, d), lambda i: (i, 0)),
        out_shape=jax.ShapeDtypeStruct((t, d), F32),
        compiler_params=_params(("parallel",)), name="final_norm",
    )(x, g.reshape(1, d))


def _stack(items, idx):
    return jnp.stack([it[idx] for it in items])


def kernel(x_prompt, x_sample, cache_mem_k, cache_mem_v, cache_sb_k, cache_sb_v, state_ml_c, state_ml_n, state_ml_m, state_ml_conv, state_ssd_h, state_ssd_conv, page_table, mem_prompt, norm_mix, norm_xa, norm_ffn, norm_mem, norm_final, ml_w_up, ml_conv_w, ml_conv_b, ml_w_qk, ml_w_vog, ml_b_gate, ml_norm, ml_w_down, sb_w_qkv, sb_bias, sb_w_o, ssd_w_in, ssd_conv_w, ssd_conv_b, ssd_dt_bias, ssd_a_log, ssd_d, ssd_norm, ssd_w_out, xa_w_q, xa_w_kv, xa_w_o, ffn_w_in, ffn_w_out):
    p = dict(norm_mix=norm_mix, norm_xa=norm_xa, norm_ffn=norm_ffn, norm_final=norm_final,
             ml_w_up=ml_w_up, ml_conv_w=ml_conv_w, ml_conv_b=ml_conv_b, ml_w_qk=ml_w_qk,
             ml_w_vog=ml_w_vog, ml_b_gate=ml_b_gate, ml_norm=ml_norm, ml_w_down=ml_w_down,
             sb_w_qkv=sb_w_qkv, sb_bias=sb_bias, sb_w_o=sb_w_o,
             ssd_w_in=ssd_w_in, ssd_conv_w=ssd_conv_w, ssd_conv_b=ssd_conv_b, ssd_dt_bias=ssd_dt_bias,
             ssd_a_log=ssd_a_log, ssd_d=ssd_d, ssd_norm=ssd_norm, ssd_w_out=ssd_w_out,
             xa_w_q=xa_w_q, xa_w_kv=xa_w_kv, xa_w_o=xa_w_o, ffn_w_in=ffn_w_in, ffn_w_out=ffn_w_out)
    w = _prep_weights(p)
    depth = norm_mix.shape[0]
    d = x_prompt.shape[2]
    bp, mem_len = mem_prompt.shape[0], mem_prompt.shape[1]
    bd = x_sample.shape[0]
    ml_heads = state_ml_c.shape[2]
    sb_heads = cache_sb_k.shape[3]
    xa_heads = cache_mem_k.shape[3]
    ssd_heads, ssd_hd, ssd_state = state_ssd_h.shape[2], state_ssd_h.shape[3], state_ssd_h.shape[4]
    ssd_groups = (ssd_conv_w.shape[2] - ssd_norm.shape[1]) // (2 * ssd_state)
    n_ml, n_sb, n_ssd = state_ml_c.shape[0], cache_sb_k.shape[0], state_ssd_h.shape[0]
    dims = (ml_heads, sb_heads, xa_heads, ssd_groups, ssd_state, ssd_hd, mem_len)
    del ssd_heads

    mem2 = mem_prompt.reshape(bp * mem_len, d)
    mem_kv_p = [_matmul(mem2, w['xa_w_kv'][l], g=norm_mem[l]) for l in range(depth)]
    y_prompt, ml_p, sb_p, ssd_p = _trunk(x_prompt, [(kv, kv, 0, 1) for kv in mem_kv_p], None, None, None,
                                         None, p, w, dims, BF16)
    shp = (bp, mem_len, xa_heads, d // xa_heads)
    mem_k_p = jnp.stack([kv[:, :d].reshape(shp) for kv in mem_kv_p])
    mem_v_p = jnp.stack([kv[:, d:].reshape(shp) for kv in mem_kv_p])

    mem_kv_s = [(cache_mem_k[l].reshape(bd * mem_len, d), cache_mem_v[l].reshape(bd * mem_len, d), 0, 0)
                for l in range(depth)]
    sb_past = [(cache_sb_k[j], cache_sb_v[j]) for j in range(n_sb)]
    ml_cache = [(state_ml_c[j], state_ml_n[j], state_ml_m[j], state_ml_conv[j]) for j in range(n_ml)]
    ssd_cache = [(state_ssd_h[j], state_ssd_conv[j]) for j in range(n_ssd)]
    y_sample, ml_s, sb_s, ssd_s = _trunk(x_sample, mem_kv_s, ml_cache, sb_past, ssd_cache, page_table, p, w,
                                         dims, F32)

    return (y_prompt, y_sample, mem_k_p, mem_v_p,
            _stack(sb_p, 0), _stack(sb_p, 1),
            _stack(ml_p, 0), _stack(ml_p, 1), _stack(ml_p, 2), _stack(ml_p, 3),
            _stack(ssd_p, 0), _stack(ssd_p, 1),
            _stack(sb_s, 0), _stack(sb_s, 1),
            _stack(ml_s, 0), _stack(ml_s, 1), _stack(ml_s, 2), _stack(ml_s, 3),
            _stack(ssd_s, 0), _stack(ssd_s, 1))
```

```python
import functools
import math

import jax
import jax.numpy as jnp
from jax import lax
from jax.experimental import pallas as pl
from jax.experimental.pallas import tpu as pltpu

F32 = jnp.float32
BF16 = jnp.bfloat16
NORM_EPS = 1e-6
NEG = -1e30
LOG2E = 1.4426950408889634
SUBLANES = 8
VMEM_LIMIT_BYTES = 56 * 1024 * 1024
WEIGHT_TILE_BYTES = 8 * 1024 * 1024
ROW_TILE = 256
ROW_TILE_BYTES = 2 * 1024 * 1024
SEQ_PAD = 128
ML_CHUNK = 256
SSD_CHUNK = 128
SSD_GROUPS_PER_STEP = 4
SB_BLOCK = 256
SB_HEADS_PER_STEP = 2
SB_PAGES_PER_STEP = 8
XA_QBLOCK = 256
CONV_COLS = 2048


def _params(sem):
    return pltpu.CompilerParams(dimension_semantics=sem, vmem_limit_bytes=VMEM_LIMIT_BYTES)


def _dot(a, b):
    return jnp.dot(a, b, preferred_element_type=F32)


def _dot_nt(a, b):
    return lax.dot_general(a, b, (((1,), (1,)), ((), ())), preferred_element_type=F32)


def _dot_tn(a, b):
    return lax.dot_general(a, b, (((0,), (0,)), ((), ())), preferred_element_type=F32)


def _split3(x):
    hi = x.astype(BF16)
    r = x - hi.astype(F32)
    mid = r.astype(BF16)
    lo = (r - mid.astype(F32)).astype(BF16)
    return hi, mid, lo


def _split2(x):
    hi = x.astype(BF16)
    lo = (x - hi.astype(F32)).astype(BF16)
    return hi, lo


def _cumsum_rows(x):
    n = x.shape[0]
    t = lax.broadcasted_iota(jnp.int32, (n, n), 0)
    s = lax.broadcasted_iota(jnp.int32, (n, n), 1)
    m = jnp.where(s <= t, 1.0, 0.0).astype(BF16)
    hi, mid, lo = _split3(x)
    return _dot(m, hi) + _dot(m, mid) + _dot(m, lo)


def _cumsum_lanes(x):
    n = x.shape[1]
    j = lax.broadcasted_iota(jnp.int32, (n, n), 0)
    s = lax.broadcasted_iota(jnp.int32, (n, n), 1)
    m = jnp.where(j <= s, 1.0, 0.0).astype(BF16)
    hi, mid, lo = _split3(x)
    return _dot(hi, m) + _dot(mid, m) + _dot(lo, m)


def _softplus(x):
    return jnp.maximum(x, 0.0) + jnp.log1p(jnp.exp(-jnp.abs(x)))


def _pad_front(x, pad):
    if pad == 0:
        return x
    return jnp.concatenate([jnp.zeros((pad, x.shape[1]), x.dtype), x], axis=0)


def _expand_lanes(cols, width):
    n, k = cols.shape
    lane = lax.broadcasted_iota(jnp.int32, (n, k * width), 1)
    out = jnp.broadcast_to(cols[:, k - 1:k], (n, k * width))
    for r in range(k - 2, -1, -1):
        out = jnp.where(lane < (r + 1) * width, cols[:, r:r + 1], out)
    return out


def _expand_rows(rows, height):
    k, n = rows.shape
    sub = lax.broadcasted_iota(jnp.int32, (k * height, n), 0)
    out = jnp.broadcast_to(rows[k - 1:k, :], (k * height, n))
    for r in range(k - 2, -1, -1):
        out = jnp.where(sub < (r + 1) * height, rows[r:r + 1, :], out)
    return out


def _mm_body(*refs, has_norm, swiglu, has_bias, has_res):
    it = iter(refs)
    x_ref = next(it)
    g_ref = next(it) if has_norm else None
    w_ref = next(it)
    w2_ref = next(it) if swiglu else None
    b_ref = next(it) if has_bias else None
    r_ref = next(it) if has_res else None
    o_ref = next(it)
    x = x_ref[...]
    if has_norm:
        x32 = x.astype(F32)
        x = x32 * lax.rsqrt(jnp.mean(x32 * x32, axis=-1, keepdims=True) + NORM_EPS) * g_ref[...]
    xb = x.astype(BF16)
    acc = _dot(xb, w_ref[...])
    if swiglu:
        acc = acc * jax.nn.sigmoid(acc) * _dot(xb, w2_ref[...])
    if has_bias:
        acc = acc + b_ref[...]
    if has_res:
        acc = r_ref[...] + acc
    o_ref[...] = acc.astype(o_ref.dtype)


def _col_tile(k, n, n_weights):
    tn = n
    while k * tn * 2 * n_weights > WEIGHT_TILE_BYTES and tn % 256 == 0:
        tn //= 2
    return tn


def _row_tile(t, k):
    tm = min(ROW_TILE, t)
    while tm * 2 * k * 4 <= ROW_TILE_BYTES and t % (tm * 2) == 0:
        tm *= 2
    return tm


def _matmul(x, w, *, g=None, w2=None, bias=None, res=None, out_dtype=F32):
    t, k = x.shape
    n = w.shape[1]
    tm = _row_tile(t, k)
    assert t % tm == 0
    tn = _col_tile(k, n, 2 if w2 is not None else 1)
    grid = (n // tn, t // tm)
    ops = [x]
    specs = [pl.BlockSpec((tm, k), lambda j, i: (i, 0))]
    if g is not None:
        ops.append(g.reshape(1, k))
        specs.append(pl.BlockSpec((1, k), lambda j, i: (0, 0)))
    ops.append(w)
    specs.append(pl.BlockSpec((k, tn), lambda j, i: (0, j)))
    if w2 is not None:
        ops.append(w2)
        specs.append(pl.BlockSpec((k, tn), lambda j, i: (0, j)))
    if bias is not None:
        ops.append(bias.reshape(1, n))
        specs.append(pl.BlockSpec((1, tn), lambda j, i: (0, j)))
    if res is not None:
        ops.append(res)
        specs.append(pl.BlockSpec((tm, tn), lambda j, i: (i, j)))
    body = functools.partial(_mm_body, has_norm=g is not None, swiglu=w2 is not None,
                             has_bias=bias is not None, has_res=res is not None)
    return pl.pallas_call(
        body, grid=grid, in_specs=specs,
        out_specs=pl.BlockSpec((tm, tn), lambda j, i: (i, j)),
        out_shape=jax.ShapeDtypeStruct((t, n), out_dtype),
        compiler_params=_params(("parallel", "parallel")), name="matmul",
    )(*ops)


def _conv_body(x_ref, prev_ref, buf_ref, w_ref, b_ref, o_ref, *, width):
    ts = x_ref.shape[0]
    x = x_ref[...]
    hist = jnp.where(pl.program_id(1) == 0, buf_ref[0], prev_ref[...])
    xx = jnp.concatenate([hist, x], axis=0)
    w = w_ref[...]
    y = b_ref[...]
    for j in range(width):
        lo = SUBLANES - (width - 1) + j
        y = y + xx[lo:lo + ts, :] * w[j:j + 1, :]
    o_ref[...] = (y * jax.nn.sigmoid(y)).astype(o_ref.dtype)


def _causal_conv_silu(x2, col0, chans, buf, w, b, bsz, seq, out_dtype):
    width = w.shape[0]
    assert width - 1 <= SUBLANES
    ts = math.gcd(seq, ROW_TILE)
    nt = seq // ts
    tc = math.gcd(chans, CONV_COLS, col0)
    assert ts % SUBLANES == 0
    cb0 = col0 // tc
    rb = ts // SUBLANES
    if buf is None:
        buf8 = jnp.zeros((bsz, SUBLANES, chans), x2.dtype)
    else:
        buf8 = jnp.pad(buf.astype(x2.dtype), ((0, 0), (SUBLANES - (width - 1), 0), (0, 0)))
    return pl.pallas_call(
        functools.partial(_conv_body, width=width), grid=(bsz, nt, chans // tc),
        in_specs=[
            pl.BlockSpec((ts, tc), lambda bi, i, c: (bi * nt + i, cb0 + c)),
            pl.BlockSpec((SUBLANES, tc), lambda bi, i, c: (jnp.maximum((bi * nt + i) * rb - 1, 0), cb0 + c)),
            pl.BlockSpec((1, SUBLANES, tc), lambda bi, i, c: (bi, 0, c)),
            pl.BlockSpec((width, tc), lambda bi, i, c: (0, c)),
            pl.BlockSpec((1, tc), lambda bi, i, c: (0, c)),
        ],
        out_specs=pl.BlockSpec((ts, tc), lambda bi, i, c: (bi * nt + i, c)),
        out_shape=jax.ShapeDtypeStruct((bsz * seq, chans), out_dtype),
        compiler_params=_params(("parallel", "parallel", "parallel")), name="conv_silu",
    )(x2, x2, buf8, w, b.reshape(1, chans))


def _conv_state(x2, col0, chans, buf, bsz, seq, width):
    tail = x2.reshape(bsz, seq, -1)[:, max(seq - (width - 1), 0):, col0:col0 + chans]
    if seq >= width - 1:
        return tail
    if buf is None:
        buf = jnp.zeros((bsz, width - 1, chans), x2.dtype)
    return jnp.concatenate([buf.astype(x2.dtype), tail], axis=1)[:, -(width - 1):]


def _mlstm_body(*refs, lr, lp, has_state, dh):
    if has_state:
        (q_ref, k_ref, v_ref, o_ref, gc_ref, gr_ref, ng_ref, c0_ref, n0_ref, m0_ref,
         hg_ref, c1_ref, n1_ref, m1_ref, c_s, n_s, m_s) = refs
    else:
        (q_ref, k_ref, v_ref, o_ref, gc_ref, gr_ref, ng_ref,
         hg_ref, c1_ref, n1_ref, m1_ref, c_s, n_s, m_s) = refs
    ci = pl.program_id(2)
    pad = lp - lr

    @pl.when(ci == 0)
    def _():
        if has_state:
            c_s[...] = c0_ref[0, 0]
            n_s[...] = n0_ref[0, 0]
            m_s[...] = m0_ref[0, 0]
        else:
            c_s[...] = jnp.zeros_like(c_s)
            n_s[...] = jnp.zeros_like(n_s)
            m_s[...] = jnp.zeros_like(m_s)

    q = _pad_front(q_ref[...].astype(F32), pad) * (dh ** -0.5)
    k = _pad_front(k_ref[...].astype(F32), pad)
    v = _pad_front(v_ref[...].astype(F32), pad)
    gc = gc_ref[0, 0]
    gr = gr_ref[0, 0]
    ii_c, ii_r = gc[:, 0:1], gr[0:1, :]
    ff_c, ff_r = -_softplus(-gc[:, 1:2]), -_softplus(-gr[1:2, :])
    if pad:
        ok_c = lax.broadcasted_iota(jnp.int32, (lp, 1), 0) >= pad
        ok_r = lax.broadcasted_iota(jnp.int32, (1, lp), 1) >= pad
        ii_c, ii_r = jnp.where(ok_c, ii_c, NEG), jnp.where(ok_r, ii_r, NEG)
        ff_c, ff_r = jnp.where(ok_c, ff_c, 0.0), jnp.where(ok_r, ff_r, 0.0)
    b_c = _cumsum_rows(ff_c)
    b_r = _cumsum_lanes(ff_r)
    m_prev = m_s[...]
    t_i = lax.broadcasted_iota(jnp.int32, (lp, lp), 0)
    s_i = lax.broadcasted_iota(jnp.int32, (lp, lp), 1)
    log_d = jnp.where(s_i <= t_i, b_c + (ii_r - b_r), NEG)
    log_inter = b_c + m_prev
    m_t = jnp.maximum(log_inter, jnp.max(log_d, axis=1, keepdims=True))
    d_mat = jnp.exp(log_d - m_t)
    w_inter = jnp.exp(log_inter - m_t)
    qb, kb, vb = q.astype(BF16), k.astype(BF16), v.astype(BF16)
    s = _dot_nt(qb, kb) * d_mat
    c = c_s[...]
    n = n_s[...]
    num = _dot(s.astype(BF16), vb) + w_inter * _dot(qb, c.astype(BF16))
    den = jnp.sum(s, axis=1, keepdims=True) + w_inter * jnp.sum(q * n, axis=1, keepdims=True)
    h = num / jnp.maximum(jnp.abs(den), jnp.exp(-m_t))
    h = h * lax.rsqrt(jnp.mean(h * h, axis=1, keepdims=True) + NORM_EPS) * ng_ref[...]
    hg_ref[...] = (jax.nn.sigmoid(o_ref[...]) * h[pad:, :]).astype(hg_ref.dtype)

    b_last = b_c[lp - 1:lp, :]
    m_last = m_t[lp - 1:lp, :]
    w_end = jnp.exp(b_last - b_c + ii_c - m_last)
    f_end = jnp.exp(b_last + m_prev - m_last)
    kw = k * w_end
    c_s[...] = f_end * c + _dot_tn(kw.astype(BF16), vb)
    n_s[...] = f_end * n + jnp.sum(kw, axis=0, keepdims=True)
    m_s[...] = m_last

    @pl.when(ci == pl.num_programs(2) - 1)
    def _():
        c1_ref[0, 0] = c_s[...]
        n1_ref[0, 0] = n_s[...]
        m1_ref[0, 0] = m_s[...]


def _mlstm_core(qk, vo, gates, norm_g, bsz, seq, heads, state, out_dtype):
    inner = qk.shape[1] // 2
    dh = inner // heads
    lr = math.gcd(seq, ML_CHUNK)
    lp = lr if lr % SEQ_PAD == 0 else SEQ_PAD
    nc = seq // lr
    pad = lp - lr
    gi = gates[:, :heads].reshape(bsz, seq, heads)
    gf = gates[:, heads:2 * heads].reshape(bsz, seq, heads)
    g = jnp.stack([gi, gf], axis=-1)
    g = g.reshape(bsz, nc, lr, heads, 2)
    g = jnp.pad(g, ((0, 0), (0, 0), (pad, 0), (0, 0), (0, 0)))
    gc = jnp.transpose(g, (3, 0, 1, 2, 4)).reshape(heads, bsz, nc * lp, 2)
    gr = jnp.transpose(g, (3, 0, 4, 1, 2)).reshape(heads, bsz, 2, nc * lp)
    has_state = state is not None
    ops = [qk, qk, vo, vo, gc, gr, norm_g.reshape(1, inner)]
    specs = [
        pl.BlockSpec((lr, dh), lambda b, h, c: (b * nc + c, h)),
        pl.BlockSpec((lr, dh), lambda b, h, c: (b * nc + c, heads + h)),
        pl.BlockSpec((lr, dh), lambda b, h, c: (b * nc + c, h)),
        pl.BlockSpec((lr, dh), lambda b, h, c: (b * nc + c, heads + h)),
        pl.BlockSpec((1, 1, lp, 2), lambda b, h, c: (h, b, c, 0)),
        pl.BlockSpec((1, 1, 2, lp), lambda b, h, c: (h, b, 0, c)),
        pl.BlockSpec((1, dh), lambda b, h, c: (0, h)),
    ]
    if has_state:
        c0, n0, m0 = state
        ops += [c0, n0.reshape(bsz, heads, 1, dh), m0.reshape(bsz, heads, 1, 1)]
        specs += [
            pl.BlockSpec((1, 1, dh, dh), lambda b, h, c: (b, h, 0, 0)),
            pl.BlockSpec((1, 1, 1, dh), lambda b, h, c: (b, h, 0, 0)),
            pl.BlockSpec((1, 1, 1, 1), lambda b, h, c: (b, h, 0, 0)),
        ]
    body = functools.partial(_mlstm_body, lr=lr, lp=lp, has_state=has_state, dh=dh)
    hg, c1, n1, m1 = pl.pallas_call(
        body, grid=(bsz, heads, nc), in_specs=specs,
        out_specs=[
            pl.BlockSpec((lr, dh), lambda b, h, c: (b * nc + c, h)),
            pl.BlockSpec((1, 1, dh, dh), lambda b, h, c: (b, h, 0, 0)),
            pl.BlockSpec((1, 1, 1, dh), lambda b, h, c: (b, h, 0, 0)),
            pl.BlockSpec((1, 1, 1, 1), lambda b, h, c: (b, h, 0, 0)),
        ],
        out_shape=[
            jax.ShapeDtypeStruct((bsz * seq, inner), out_dtype),
            jax.ShapeDtypeStruct((bsz, heads, dh, dh), F32),
            jax.ShapeDtypeStruct((bsz, heads, 1, dh), F32),
            jax.ShapeDtypeStruct((bsz, heads, 1, 1), F32),
        ],
        scratch_shapes=[pltpu.VMEM((dh, dh), F32), pltpu.VMEM((1, dh), F32), pltpu.VMEM((1, 1), F32)],
        compiler_params=_params(("parallel", "parallel", "arbitrary")), name="mlstm_core",
    )(*ops)
    return hg, c1, n1.reshape(bsz, heads, dh), m1.reshape(bsz, heads)


def _ssd_group(x, bm, cm, z, dtc, dtr, pr, pc, ng, h, *, lp, pad, hpg, hd):
    dd_c = _softplus(dtc + pr[0:1, :])
    dd_r = _softplus(dtr + pc[:, 0:1])
    if pad:
        dd_c = jnp.where(lax.broadcasted_iota(jnp.int32, (lp, hpg), 0) >= pad, dd_c, 0.0)
        dd_r = jnp.where(lax.broadcasted_iota(jnp.int32, (hpg, lp), 1) >= pad, dd_r, 0.0)
    cum_c = _cumsum_rows(dd_c * pr[1:2, :])
    cum_r = _cumsum_lanes(dd_r * pc[:, 1:2])
    cb = _dot_nt(cm, bm)
    t_i = lax.broadcasted_iota(jnp.int32, (lp, lp), 0)
    s_i = lax.broadcasted_iota(jnp.int32, (lp, lp), 1)
    tri = s_i <= t_i
    head_of_lane = lax.broadcasted_iota(jnp.int32, (1, hpg * hd), 1) // hd
    y = jnp.zeros((lp, hpg * hd), F32)
    for r in range(hpg):
        seg = jnp.where(tri, cum_c[:, r:r + 1] - cum_r[r:r + 1, :], NEG)
        w = jnp.exp(seg) * cb * dd_r[r:r + 1, :]
        xr = jnp.where(head_of_lane == r, x, 0.0).astype(BF16)
        y = y + _dot(w.astype(BF16), xr)
    y = y + _expand_lanes(jnp.exp(cum_c), hd) * _dot_nt(cm, h.astype(BF16))
    y = y + _expand_lanes(pr[2:3, :], hd) * x
    yv = y[pad:, :] * (z * jax.nn.sigmoid(z))
    yv = yv * lax.rsqrt(jnp.mean(yv * yv, axis=1, keepdims=True) + NORM_EPS) * ng
    w_end = jnp.exp(cum_c[lp - 1:lp, :] - cum_c) * dd_c
    xw = x * _expand_lanes(w_end, hd)
    decay = _expand_rows(jnp.exp(cum_r[:, lp - 1:lp]), hd)
    return yv, decay * h + _dot_tn(xw.astype(BF16), bm)


def _ssd_body(*refs, lr, lp, has_state, hpg, hd, gps):
    if has_state:
        (x_ref, bm_ref, cm_ref, z_ref, dtc_ref, dtr_ref, pr_ref, pc_ref, ng_ref, h0_ref,
         y_ref, h1_ref, h_s) = refs
    else:
        (x_ref, bm_ref, cm_ref, z_ref, dtc_ref, dtr_ref, pr_ref, pc_ref, ng_ref,
         y_ref, h1_ref, h_s) = refs
    ci = pl.program_id(2)
    pad = lp - lr
    gw = hpg * hd
    ns = bm_ref.shape[1] // gps

    @pl.when(ci == 0)
    def _():
        if has_state:
            h_s[...] = h0_ref[0]
        else:
            h_s[...] = jnp.zeros_like(h_s)

    for gi in range(gps):
        xs = slice(gi * gw, (gi + 1) * gw)
        bs = slice(gi * ns, (gi + 1) * ns)
        yv, h_new = _ssd_group(
            _pad_front(x_ref[:, xs], pad), _pad_front(bm_ref[:, bs], pad).astype(BF16),
            _pad_front(cm_ref[:, bs], pad).astype(BF16), z_ref[:, xs], dtc_ref[gi, 0], dtr_ref[gi, 0],
            pr_ref[gi], pc_ref[gi], ng_ref[:, xs], h_s[gi], lp=lp, pad=pad, hpg=hpg, hd=hd)
        y_ref[:, xs] = yv.astype(y_ref.dtype)
        h_s[gi] = h_new

    @pl.when(ci == pl.num_programs(2) - 1)
    def _():
        h1_ref[0] = h_s[...]


def _ssd_core(zx, xbc, dt_raw, dt_bias, a_neg, d_skip, norm_g, bsz, seq, groups, nstate, hd, h0, out_dtype):
    inner = norm_g.shape[0]
    heads = inner // hd
    hpg = heads // groups
    gw = hpg * hd
    gps = math.gcd(groups, SSD_GROUPS_PER_STEP)
    ngs = groups // gps
    lr = math.gcd(seq, SSD_CHUNK)
    lp = lr if lr % SEQ_PAD == 0 else SEQ_PAD
    nc = seq // lr
    pad = lp - lr
    d = dt_raw[:, :heads].reshape(bsz, nc, lr, groups, hpg)
    d = jnp.pad(d, ((0, 0), (0, 0), (pad, 0), (0, 0), (0, 0)))
    dtc = jnp.transpose(d, (3, 0, 1, 2, 4)).reshape(groups, bsz, nc * lp, hpg)
    dtr = jnp.transpose(d, (3, 0, 4, 1, 2)).reshape(groups, bsz, hpg, nc * lp)
    par = jnp.stack([dt_bias, a_neg, d_skip]).astype(F32).reshape(3, groups, hpg)
    pr = jnp.transpose(par, (1, 0, 2))
    pc = jnp.transpose(par, (1, 2, 0))
    b_off = inner // (gps * nstate)
    c_off = b_off + ngs
    has_state = h0 is not None
    ops = [xbc, xbc, xbc, zx, dtc, dtr, pr, pc, norm_g.reshape(1, inner)]
    specs = [
        pl.BlockSpec((lr, gps * gw), lambda b, g, c: (b * nc + c, g)),
        pl.BlockSpec((lr, gps * nstate), lambda b, g, c: (b * nc + c, b_off + g)),
        pl.BlockSpec((lr, gps * nstate), lambda b, g, c: (b * nc + c, c_off + g)),
        pl.BlockSpec((lr, gps * gw), lambda b, g, c: (b * nc + c, g)),
        pl.BlockSpec((gps, 1, lp, hpg), lambda b, g, c: (g, b, c, 0)),
        pl.BlockSpec((gps, 1, hpg, lp), lambda b, g, c: (g, b, 0, c)),
        pl.BlockSpec((gps, 3, hpg), lambda b, g, c: (g, 0, 0)),
        pl.BlockSpec((gps, hpg, 3), lambda b, g, c: (g, 0, 0)),
        pl.BlockSpec((1, gps * gw), lambda b, g, c: (0, g)),
    ]
    if has_state:
        ops.append(h0.reshape(bsz, groups, gw, nstate))
        specs.append(pl.BlockSpec((1, gps, gw, nstate), lambda b, g, c: (b, g, 0, 0)))
    body = functools.partial(_ssd_body, lr=lr, lp=lp, has_state=has_state, hpg=hpg, hd=hd, gps=gps)
    y, h1 = pl.pallas_call(
        body, grid=(bsz, ngs, nc), in_specs=specs,
        out_specs=[
            pl.BlockSpec((lr, gps * gw), lambda b, g, c: (b * nc + c, g)),
            pl.BlockSpec((1, gps, gw, nstate), lambda b, g, c: (b, g, 0, 0)),
        ],
        out_shape=[
            jax.ShapeDtypeStruct((bsz * seq, inner), out_dtype),
            jax.ShapeDtypeStruct((bsz, groups, gw, nstate), F32),
        ],
        scratch_shapes=[pltpu.VMEM((gps, gw, nstate), F32)],
        compiler_params=_params(("parallel", "parallel", "arbitrary")), name="ssd_core",
    )(*ops)
    return y, h1.reshape(bsz, heads, hd, nstate)


def _sb_logs(z):
    t = jnp.log(1.0 + jnp.exp2(jnp.abs(z) * (-LOG2E)))
    lb = jnp.minimum(z, 0.0) - t
    return lb, lb - z


def _sbp_body(bias_ref, q_ref, k_ref, v_ref, o_ref, *, blk, hd, hps):
    hg = pl.program_id(1)
    qi = pl.program_id(2)
    scale = hd ** -0.5
    bias = [bias_ref[hg * hps + u] for u in range(hps)]
    qb = [q_ref[:, u * hd:(u + 1) * hd].astype(BF16) for u in range(hps)]
    j_i = lax.broadcasted_iota(jnp.int32, (2 * blk, blk), 0)
    s_i = lax.broadcasted_iota(jnp.int32, (2 * blk, blk), 1)
    tri2 = jnp.where((j_i % blk) > s_i, 1.0, 0.0).astype(BF16)
    t_i = lax.broadcasted_iota(jnp.int32, (blk, blk), 0)
    strict = lax.broadcasted_iota(jnp.int32, (blk, blk), 1) < t_i

    def rows(kb):
        return pl.ds(pl.multiple_of(kb * blk, blk), blk)

    def scores(kb, u):
        return _dot_nt(qb[u], k_ref[rows(kb), u * hd:(u + 1) * hd].astype(BF16))

    def weights(s, run, u, mask):
        lb, lr = _sb_logs(s * scale + bias[u])
        if mask is not None:
            lr = jnp.where(mask, lr, 0.0)
        hi, lo = _split2(lr)
        local = _dot(jnp.concatenate([hi, lo], axis=1), tri2)
        a = jnp.exp(lb + (run + local))
        if mask is not None:
            a = jnp.where(mask, a, 0.0)
        return a.astype(BF16), run + (local[:, 0:1] + lr[:, 0:1])

    def weighted_values(a, kb, u):
        return _dot(a, v_ref[rows(kb), u * hd:(u + 1) * hd].astype(BF16))

    def step(j, carry):
        kb = qi - 1 - j
        out = []
        for u in range(hps):
            run, acc, s, a_prev = carry[u]
            s_next = scores(jnp.maximum(kb - 1, 0), u)
            acc = acc + weighted_values(a_prev, kb + 1, u)
            a, run = weights(s, run, u, None)
            out.append((run, acc, s_next, a))
        return tuple(out)

    carry = []
    for u in range(hps):
        a, run = weights(scores(qi, u), jnp.zeros((blk, 1), F32), u, strict)
        carry.append((run, jnp.zeros((blk, hd), F32), scores(jnp.maximum(qi - 1, 0), u), a))
    carry = lax.fori_loop(0, qi, step, tuple(carry))
    for u in range(hps):
        acc = carry[u][1] + weighted_values(carry[u][3], 0, u)
        o_ref[:, u * hd:(u + 1) * hd] = acc.astype(o_ref.dtype)


def _sb_prompt(qkv, bias, bsz, seq, heads, hd, out_dtype):
    blk = math.gcd(seq, SB_BLOCK)
    nq = seq // blk
    hps = math.gcd(heads, SB_HEADS_PER_STEP)
    nhg = heads // hps
    body = functools.partial(_sbp_body, blk=blk, hd=hd, hps=hps)
    return pl.pallas_call(
        body, grid=(bsz, nhg, nq),
        in_specs=[
            pl.BlockSpec(memory_space=pltpu.SMEM),
            pl.BlockSpec((blk, hps * hd), lambda b, h, i: (b * nq + i, h)),
            pl.BlockSpec((seq, hps * hd), lambda b, h, i: (b, nhg + h)),
            pl.BlockSpec((seq, hps * hd), lambda b, h, i: (b, 2 * nhg + h)),
        ],
        out_specs=pl.BlockSpec((blk, hps * hd), lambda b, h, i: (b * nq + i, h)),
        out_shape=jax.ShapeDtypeStruct((bsz * seq, heads * hd), out_dtype),
        compiler_params=_params(("parallel", "parallel", "arbitrary")), name="sb_prompt",
    )(bias.astype(F32), qkv, qkv, qkv)


def _sbs_body(pt_ref, q_ref, kn_ref, vn_ref, *rest, sq, heads, hd, page, ppg):
    del pt_ref
    kp_refs, vp_refs = rest[:ppg], rest[ppg:2 * ppg]
    bias_ref, o_ref, run_s, acc_s, qbd_s = rest[2 * ppg:]
    p = pl.program_id(1)
    width = heads * hd
    cols = qbd_s.shape[0]
    scale = hd ** -0.5

    def own_head():
        row_i = lax.broadcasted_iota(jnp.int32, (cols, width), 0)
        lane_i = lax.broadcasted_iota(jnp.int32, (cols, width), 1)
        return (row_i // sq) == (lane_i // hd)

    def process(kks, vvs, mask):
        n = len(kks)
        j_i = lax.broadcasted_iota(jnp.int32, (page, 2 * page), 1)
        s_i = lax.broadcasted_iota(jnp.int32, (page, 2 * page), 0)
        tri2 = jnp.where((j_i % page) > s_i, 1.0, 0.0).astype(BF16)
        kcat = kks[0] if n == 1 else jnp.concatenate(kks, axis=0)
        z = _dot_nt(kcat, qbd_s[...]) * scale + bias_ref[...]
        lb, lr = _sb_logs(z)
        if mask is not None:
            lr = jnp.where(mask, lr, 0.0)
        run = run_s[...]
        parts = []
        for i in range(n):
            rs = slice(i * page, (i + 1) * page)
            hi, lo = _split2(lr[rs])
            local = _dot(tri2, jnp.concatenate([hi, lo], axis=0))
            a = jnp.exp(lb[rs] + (run + local))
            if mask is not None:
                a = jnp.where(mask, a, 0.0)
            parts.append(a.astype(BF16))
            run = run + (local[0:1, :] + lr[rs][0:1, :])
        run_s[...] = run
        acat = parts[0] if n == 1 else jnp.concatenate(parts, axis=0)
        vcat = vvs[0] if n == 1 else jnp.concatenate(vvs, axis=0)
        acc_s[...] += _dot_tn(acat, vcat)

    def load_page(ref):
        return jnp.concatenate([ref[0, 0, pl.ds(h, page, stride=heads), :] for h in range(heads)],
                               axis=1).astype(BF16)

    @pl.when(p == 0)
    def _():
        run_s[...] = jnp.zeros_like(run_s)
        acc_s[...] = jnp.zeros_like(acc_s)
        q = q_ref[...]
        tiled = jnp.concatenate([q] * heads + [jnp.zeros((cols - heads * sq, width), F32)], axis=0)
        qbd_s[...] = jnp.where(own_head(), tiled, 0.0).astype(BF16)
        key_i = lax.broadcasted_iota(jnp.int32, (page, cols), 0)
        qry_i = lax.broadcasted_iota(jnp.int32, (page, cols), 1) % sq
        zeros = jnp.zeros((page - sq, width), F32)
        process([jnp.concatenate([kn_ref[...], zeros], axis=0).astype(BF16)],
                [jnp.concatenate([vn_ref[...], zeros], axis=0).astype(BF16)], key_i < qry_i)

    @pl.when(p > 0)
    def _():
        process([load_page(r) for r in kp_refs], [load_page(r) for r in vp_refs], None)

    @pl.when(p == pl.num_programs(1) - 1)
    def _():
        acc = jnp.where(own_head(), acc_s[...], 0.0)
        out = acc[0:sq, :]
        for h in range(1, heads):
            out = out + acc[h * sq:(h + 1) * sq, :]
        o_ref[...] = out.astype(o_ref.dtype)


def _sb_sample(qkv, pool_k, pool_v, layer, page_table, bias, bsz, sq, heads, hd, out_dtype):
    width = heads * hd
    n_pages = page_table.shape[1]
    page = pool_k.shape[2]
    cols = SEQ_PAD
    ppg = math.gcd(n_pages, SB_PAGES_PER_STEP)
    assert heads * sq <= cols and sq <= page
    bias_cols = jnp.pad(jnp.repeat(bias.astype(F32), sq), (0, cols - heads * sq)).reshape(1, cols)
    pool_k = pool_k.reshape(pool_k.shape[0], pool_k.shape[1], page * heads, hd)
    pool_v = pool_v.reshape(pool_v.shape[0], pool_v.shape[1], page * heads, hd)

    def page_spec(i):
        def idx(b, s, pt):
            return (layer, pt[b, n_pages - 1 - (jnp.maximum(s - 1, 0) * ppg + i)], 0, 0)
        return pl.BlockSpec((1, 1, page * heads, hd), idx)

    body = functools.partial(_sbs_body, sq=sq, heads=heads, hd=hd, page=page, ppg=ppg)
    grid_spec = pltpu.PrefetchScalarGridSpec(
        num_scalar_prefetch=1, grid=(bsz, n_pages // ppg + 1),
        in_specs=[
            pl.BlockSpec((sq, width), lambda b, s, pt: (b, 0)),
            pl.BlockSpec((sq, width), lambda b, s, pt: (b, 1)),
            pl.BlockSpec((sq, width), lambda b, s, pt: (b, 2)),
            *[page_spec(i) for i in range(ppg)],
            *[page_spec(i) for i in range(ppg)],
            pl.BlockSpec((1, cols), lambda b, s, pt: (0, 0)),
        ],
        out_specs=pl.BlockSpec((sq, width), lambda b, s, pt: (b, 0)),
        scratch_shapes=[pltpu.VMEM((1, cols), F32), pltpu.VMEM((cols, width), F32),
                        pltpu.VMEM((cols, width), BF16)],
    )
    return pl.pallas_call(
        body, grid_spec=grid_spec,
        out_shape=jax.ShapeDtypeStruct((bsz * sq, width), out_dtype),
        compiler_params=_params(("parallel", "arbitrary")), name="sb_sample",
    )(page_table, qkv, qkv, qkv, *([pool_k] * ppg), *([pool_v] * ppg), bias_cols)


def _xa_body(q_ref, mk_ref, mv_ref, o_ref, *, heads, hd):
    scale = hd ** -0.5
    for h in range(heads):
        cs = slice(h * hd, (h + 1) * hd)
        s = _dot_nt(q_ref[:, cs].astype(BF16), mk_ref[:, cs].astype(BF16)) * scale
        e = jnp.exp(s - jnp.max(s, axis=1, keepdims=True))
        pr = e / jnp.sum(e, axis=1, keepdims=True)
        o_ref[:, cs] = _dot(pr.astype(BF16), mv_ref[:, cs].astype(BF16)).astype(o_ref.dtype)


def _cross_attn(q, mk, mv, mk_col, mv_col, bsz, seq, mem_len, heads, hd, out_dtype):
    width = heads * hd
    tq = math.gcd(seq, XA_QBLOCK)
    nq = seq // tq
    body = functools.partial(_xa_body, heads=heads, hd=hd)
    return pl.pallas_call(
        body, grid=(bsz, nq),
        in_specs=[
            pl.BlockSpec((tq, width), lambda b, i: (b * nq + i, 0)),
            pl.BlockSpec((mem_len, width), lambda b, i: (b, mk_col)),
            pl.BlockSpec((mem_len, width), lambda b, i: (b, mv_col)),
        ],
        out_specs=pl.BlockSpec((tq, width), lambda b, i: (b * nq + i, 0)),
        out_shape=jax.ShapeDtypeStruct((bsz * seq, width), out_dtype),
        compiler_params=_params(("parallel", "parallel")), name="cross_attn",
    )(q, mk, mv)


def _pad_cols(w, n):
    return jnp.pad(w, ((0, 0), (0, n - w.shape[1])))


def _prep_weights(p):
    w = {}
    ml_inner = p['ml_w_up'].shape[2]
    ssd_heads = p['ssd_dt_bias'].shape[1]
    ssd_main = p['ssd_w_in'].shape[2] - ssd_heads
    ffn_hidden = p['ffn_w_out'].shape[1]
    w['ml_w_up'] = p['ml_w_up'].astype(BF16)
    w['ml_w_qk'] = p['ml_w_qk'].astype(BF16)
    w['ml_w_vo'] = p['ml_w_vog'][:, :, :2 * ml_inner].astype(BF16)
    w['ml_w_g'] = jnp.stack([_pad_cols(m[:, 2 * ml_inner:], 128) for m in p['ml_w_vog']]).astype(BF16)
    w['ml_b_g'] = jnp.stack([jnp.pad(b, (0, 128 - b.shape[0])) for b in p['ml_b_gate']]).astype(F32)
    w['ml_w_down'] = p['ml_w_down'].astype(BF16)
    w['sb_w_qkv'] = p['sb_w_qkv'].astype(BF16)
    w['sb_w_o'] = p['sb_w_o'].astype(BF16)
    w['ssd_w_main'] = p['ssd_w_in'][:, :, :ssd_main].astype(BF16)
    w['ssd_w_dt'] = jnp.stack([_pad_cols(m[:, ssd_main:], 128) for m in p['ssd_w_in']]).astype(BF16)
    w['ssd_w_out'] = p['ssd_w_out'].astype(BF16)
    w['ssd_a'] = -jnp.exp(p['ssd_a_log'].astype(F32))
    w['xa_w_q'] = p['xa_w_q'].astype(BF16)
    w['xa_w_kv'] = p['xa_w_kv'].astype(BF16)
    w['xa_w_o'] = p['xa_w_o'].astype(BF16)
    w['ffn_w_gate'] = p['ffn_w_in'][:, :, :ffn_hidden].astype(BF16)
    w['ffn_w_up'] = p['ffn_w_in'][:, :, ffn_hidden:].astype(BF16)
    w['ffn_w_out'] = p['ffn_w_out'].astype(BF16)
    return w


def _trunk(x3, mem_kv, ml_states, sb_pools, ssd_states, page_table, p, w, dims, act_dtype):
    bsz, seq, d = x3.shape
    depth = p['norm_mix'].shape[0]
    ml_heads, sb_heads, xa_heads, ssd_groups, ssd_state, ssd_hd, mem_len = dims
    x = x3.reshape(bsz * seq, d)
    ml_new, sb_new, ssd_new = [], [], []
    n_ml = n_sb = n_ssd = 0
    for layer in range(depth):
        kind = layer % 3
        if kind == 0:
            j = n_ml
            n_ml += 1
            inner = p['ml_w_up'].shape[2]
            width = p['ml_conv_w'].shape[1]
            xm = _matmul(x, w['ml_w_up'][j], g=p['norm_mix'][layer])
            st = None if ml_states is None else ml_states[j]
            buf = None if st is None else st[3]
            xc = _causal_conv_silu(xm, 0, inner, buf, p['ml_conv_w'][j], p['ml_conv_b'][j], bsz, seq, act_dtype)
            qk = _matmul(xc, w['ml_w_qk'][j])
            vo = _matmul(xm, w['ml_w_vo'][j])
            gates = _matmul(xm, w['ml_w_g'][j], bias=w['ml_b_g'][j])
            hg, c1, n1, m1 = _mlstm_core(qk, vo, gates, p['ml_norm'][j], bsz, seq, ml_heads,
                                         None if st is None else st[:3], act_dtype)
            x = _matmul(hg, w['ml_w_down'][j], res=x)
            ml_new.append((c1, n1, m1, _conv_state(xm, 0, inner, buf, bsz, seq, width)))
        elif kind == 1:
            j = n_sb
            n_sb += 1
            hd = d // sb_heads
            qkv = _matmul(x, w['sb_w_qkv'][j], g=p['norm_mix'][layer])
            if sb_pools is None:
                o = _sb_prompt(qkv, p['sb_bias'][j], bsz, seq, sb_heads, hd, act_dtype)
            else:
                o = _sb_sample(qkv, sb_pools[0], sb_pools[1], j, page_table, p['sb_bias'][j],
                               bsz, seq, sb_heads, hd, act_dtype)
            x = _matmul(o, w['sb_w_o'][j], res=x)
            shp = (bsz, seq, sb_heads, hd)
            sb_new.append((qkv[:, d:2 * d].reshape(shp), qkv[:, 2 * d:].reshape(shp)))
        else:
            j = n_ssd
            n_ssd += 1
            inner = p['ssd_norm'].shape[1]
            conv_ch = p['ssd_conv_w'].shape[2]
            width = p['ssd_conv_w'].shape[1]
            zx = _matmul(x, w['ssd_w_main'][j], g=p['norm_mix'][layer])
            dt_raw = _matmul(x, w['ssd_w_dt'][j], g=p['norm_mix'][layer])
            st = None if ssd_states is None else ssd_states[j]
            buf = None if st is None else st[1]
            xbc = _causal_conv_silu(zx, inner, conv_ch, buf, p['ssd_conv_w'][j], p['ssd_conv_b'][j],
                                    bsz, seq, F32)
            y, h1 = _ssd_core(zx, xbc, dt_raw, p['ssd_dt_bias'][j], w['ssd_a'][j],
                              p['ssd_d'][j], p['ssd_norm'][j], bsz, seq, ssd_groups, ssd_state, ssd_hd,
                              None if st is None else st[0], act_dtype)
            x = _matmul(y, w['ssd_w_out'][j], res=x)
            ssd_new.append((h1, _conv_state(zx, inner, conv_ch, buf, bsz, seq, width)))
        mk, mv, mk_col, mv_col = mem_kv[layer]
        q = _matmul(x, w['xa_w_q'][layer], g=p['norm_xa'][layer])
        o = _cross_attn(q, mk, mv, mk_col, mv_col, bsz, seq, mem_len, xa_heads, d // xa_heads, act_dtype)
        x = _matmul(o, w['xa_w_o'][layer], res=x)
        hid = _matmul(x, w['ffn_w_gate'][layer], g=p['norm_ffn'][layer], w2=w['ffn_w_up'][layer],
                      out_dtype=act_dtype)
        x = _matmul(hid, w['ffn_w_out'][layer], res=x)
    y = _final_norm(x, p['norm_final'])
    return y.reshape(bsz, seq, d), ml_new, sb_new, ssd_new


def _norm_body(x_ref, g_ref, o_ref):
    x = x_ref[...]
    o_ref[...] = x * lax.rsqrt(jnp.mean(x * x, axis=-1, keepdims=True) + NORM_EPS) * g_ref[...]


def _final_norm(x, g):
    t, d = x.shape
    tm = min(ROW_TILE, t)
    return pl.pallas_call(
        _norm_body, grid=(t // tm,),
        in_specs=[pl.BlockSpec((tm, d), lambda i: (i, 0)), pl.BlockSpec((1, d), lambda i: (0, 0))],
        out_specs=pl.BlockSpec((tm, d), lambda i: (i, 0)),
        out_shape=jax.ShapeDtypeStruct((t, d), F32),
        compiler_params=_params(("parallel",)), name="final_norm",
    )(x, g.reshape(1, d))


def _stack(items, idx):
    return jnp.stack([it[idx] for it in items])


def kernel(x_prompt, x_sample, cache_mem_k, cache_mem_v, cache_sb_k, cache_sb_v, state_ml_c, state_ml_n, state_ml_m, state_ml_conv, state_ssd_h, state_ssd_conv, page_table, mem_prompt, norm_mix, norm_xa, norm_ffn, norm_mem, norm_final, ml_w_up, ml_conv_w, ml_conv_b, ml_w_qk, ml_w_vog, ml_b_gate, ml_norm, ml_w_down, sb_w_qkv, sb_bias, sb_w_o, ssd_w_in, ssd_conv_w, ssd_conv_b, ssd_dt_bias, ssd_a_log, ssd_d, ssd_norm, ssd_w_out, xa_w_q, xa_w_kv, xa_w_o, ffn_w_in, ffn_w_out):
    p = dict(norm_mix=norm_mix, norm_xa=norm_xa, norm_ffn=norm_ffn, norm_final=norm_final,
             ml_w_up=ml_w_up, ml_conv_w=ml_conv_w, ml_conv_b=ml_conv_b, ml_w_qk=ml_w_qk,
             ml_w_vog=ml_w_vog, ml_b_gate=ml_b_gate, ml_norm=ml_norm, ml_w_down=ml_w_down,
             sb_w_qkv=sb_w_qkv, sb_bias=sb_bias, sb_w_o=sb_w_o,
             ssd_w_in=ssd_w_in, ssd_conv_w=ssd_conv_w, ssd_conv_b=ssd_conv_b, ssd_dt_bias=ssd_dt_bias,
             ssd_a_log=ssd_a_log, ssd_d=ssd_d, ssd_norm=ssd_norm, ssd_w_out=ssd_w_out,
             xa_w_q=xa_w_q, xa_w_kv=xa_w_kv, xa_w_o=xa_w_o, ffn_w_in=ffn_w_in, ffn_w_out=ffn_w_out)
    w = _prep_weights(p)
    depth = norm_mix.shape[0]
    d = x_prompt.shape[2]
    bp, mem_len = mem_prompt.shape[0], mem_prompt.shape[1]
    bd = x_sample.shape[0]
    ml_heads = state_ml_c.shape[2]
    sb_heads = cache_sb_k.shape[3]
    xa_heads = cache_mem_k.shape[3]
    ssd_hd, ssd_state = state_ssd_h.shape[3], state_ssd_h.shape[4]
    ssd_groups = (ssd_conv_w.shape[2] - ssd_norm.shape[1]) // (2 * ssd_state)
    n_ml, n_ssd = state_ml_c.shape[0], state_ssd_h.shape[0]
    dims = (ml_heads, sb_heads, xa_heads, ssd_groups, ssd_state, ssd_hd, mem_len)

    mem2 = mem_prompt.reshape(bp * mem_len, d)
    mem_kv_p = [_matmul(mem2, w['xa_w_kv'][l], g=norm_mem[l]) for l in range(depth)]
    y_prompt, ml_p, sb_p, ssd_p = _trunk(x_prompt, [(kv, kv, 0, 1) for kv in mem_kv_p], None, None, None,
                                         None, p, w, dims, BF16)
    shp = (bp, mem_len, xa_heads, d // xa_heads)
    mem_k_p = jnp.stack([kv[:, :d].reshape(shp) for kv in mem_kv_p])
    mem_v_p = jnp.stack([kv[:, d:].reshape(shp) for kv in mem_kv_p])

    mem_kv_s = [(cache_mem_k[l].reshape(bd * mem_len, d), cache_mem_v[l].reshape(bd * mem_len, d), 0, 0)
                for l in range(depth)]
    ml_cache = [(state_ml_c[j], state_ml_n[j], state_ml_m[j], state_ml_conv[j]) for j in range(n_ml)]
    ssd_cache = [(state_ssd_h[j], state_ssd_conv[j]) for j in range(n_ssd)]
    y_sample, ml_s, sb_s, ssd_s = _trunk(x_sample, mem_kv_s, ml_cache, (cache_sb_k, cache_sb_v), ssd_cache,
                                         page_table, p, w, dims, F32)

    return (y_prompt, y_sample, mem_k_p, mem_v_p,
            _stack(sb_p, 0), _stack(sb_p, 1),
            _stack(ml_p, 0), _stack(ml_p, 1), _stack(ml_p, 2), _stack(ml_p, 3),
            _stack(ssd_p, 0), _stack(ssd_p, 1),
            _stack(sb_s, 0), _stack(sb_s, 1),
            _stack(ml_s, 0), _stack(ml_s, 1), _stack(ml_s, 2), _stack(ml_s, 3),
            _stack(ssd_s, 0), _stack(ssd_s, 1))
```

```python
import functools
import math

import jax
import jax.numpy as jnp
from jax import lax
from jax.experimental import pallas as pl
from jax.experimental.pallas import tpu as pltpu

F32 = jnp.float32
BF16 = jnp.bfloat16
NORM_EPS = 1e-6
NEG = -1e30
LOG2E = 1.4426950408889634
SUBLANES = 8
VMEM_LIMIT_BYTES = 56 * 1024 * 1024
LANES = 128
MM_VMEM_BUDGET_BYTES = 44 * 1024 * 1024
MXU_FLOPS = 1.0e15
HBM_BYTES_PER_S = 3.0e12
STEP_OVERHEAD_S = 0.35e-6
ROW_TILE = 256
ML_HEADS_PER_STEP = 2
SEQ_PAD = 128
ML_CHUNK = 256
SSD_CHUNK = 128
SSD_GROUPS_PER_STEP = 4
SB_BLOCK = 256
SB_HEADS_PER_STEP = 2
SB_PAGES_PER_STEP = 8
XA_QBLOCK = 256
CONV_COLS = 2048


def _params(sem):
    return pltpu.CompilerParams(dimension_semantics=sem, vmem_limit_bytes=VMEM_LIMIT_BYTES)


def _dot(a, b):
    return jnp.dot(a, b, preferred_element_type=F32)


def _dot_nt(a, b):
    return lax.dot_general(a, b, (((1,), (1,)), ((), ())), preferred_element_type=F32)


def _dot_tn(a, b):
    return lax.dot_general(a, b, (((0,), (0,)), ((), ())), preferred_element_type=F32)


def _split3(x):
    hi = x.astype(BF16)
    r = x - hi.astype(F32)
    mid = r.astype(BF16)
    lo = (r - mid.astype(F32)).astype(BF16)
    return hi, mid, lo


def _split2(x):
    hi = x.astype(BF16)
    lo = (x - hi.astype(F32)).astype(BF16)
    return hi, lo


def _cumsum_rows(x):
    n = x.shape[0]
    t = lax.broadcasted_iota(jnp.int32, (n, n), 0)
    s = lax.broadcasted_iota(jnp.int32, (n, n), 1)
    m = jnp.where(s <= t, 1.0, 0.0).astype(BF16)
    hi, mid, lo = _split3(x)
    return _dot(m, hi) + _dot(m, mid) + _dot(m, lo)


def _cumsum_lanes(x):
    n = x.shape[1]
    j = lax.broadcasted_iota(jnp.int32, (n, n), 0)
    s = lax.broadcasted_iota(jnp.int32, (n, n), 1)
    m = jnp.where(j <= s, 1.0, 0.0).astype(BF16)
    hi, mid, lo = _split3(x)
    return _dot(hi, m) + _dot(mid, m) + _dot(lo, m)


def _softplus(x):
    return jnp.maximum(x, 0.0) + jnp.log1p(jnp.exp(-jnp.abs(x)))


def _pad_front(x, pad):
    if pad == 0:
        return x
    return jnp.concatenate([jnp.zeros((pad, x.shape[1]), x.dtype), x], axis=0)


def _expand_lanes(cols, width):
    n, k = cols.shape
    lane = lax.broadcasted_iota(jnp.int32, (n, k * width), 1)
    out = jnp.broadcast_to(cols[:, k - 1:k], (n, k * width))
    for r in range(k - 2, -1, -1):
        out = jnp.where(lane < (r + 1) * width, cols[:, r:r + 1], out)
    return out


def _expand_rows(rows, height):
    k, n = rows.shape
    sub = lax.broadcasted_iota(jnp.int32, (k * height, n), 0)
    out = jnp.broadcast_to(rows[k - 1:k, :], (k * height, n))
    for r in range(k - 2, -1, -1):
        out = jnp.where(sub < (r + 1) * height, rows[r:r + 1, :], out)
    return out


def _mm_body(*refs, has_norm, swiglu, has_res, has_extra):
    it = iter(refs)
    x_ref = next(it)
    g_ref = next(it) if has_norm else None
    w_ref = next(it)
    w2_ref = next(it) if swiglu else None
    r_ref = next(it) if has_res else None
    we_ref, be_ref = (next(it), next(it)) if has_extra else (None, None)
    o_ref = next(it)
    x = x_ref[...]
    if has_norm:
        x32 = x.astype(F32)
        x = x32 * lax.rsqrt(jnp.mean(x32 * x32, axis=-1, keepdims=True) + NORM_EPS) * g_ref[...]
    xb = x.astype(BF16)
    acc = _dot(xb, w_ref[...])
    if swiglu:
        acc = acc * jax.nn.sigmoid(acc) * _dot(xb, w2_ref[...])
    if has_res:
        acc = r_ref[...] + acc
    o_ref[...] = acc.reshape(o_ref.shape).astype(o_ref.dtype)
    if has_extra:
        next(it)[...] = _dot(xb, we_ref[...]) + be_ref[...]


def _mm_tiles(t, k, ncols, x_bytes, out_bytes, n_w, has_res, n_passes):
    best = None
    for tm in sorted({min(t, m) for m in (256, 512, 1024, 2048)}):
        if t % tm:
            continue
        for d in range(1, ncols // LANES + 1):
            tn = ncols // d
            if ncols % d or tn % LANES:
                continue
            vmem = 2 * (tm * k * x_bytes + k * tn * 2 * n_w + tm * tn * (out_bytes + 4 * has_res)
                        + tm * LANES * 4 + k * LANES * 2)
            if vmem > MM_VMEM_BUDGET_BYTES:
                continue
            col_tiles = d * n_passes
            steps = col_tiles * (t // tm)
            hbm = col_tiles * t * k * x_bytes + t * ncols * n_passes * (out_bytes + 4 * has_res)
            cost = (max(2.0 * t * k * ncols * n_passes * n_w / MXU_FLOPS, hbm / HBM_BYTES_PER_S)
                    + steps * STEP_OVERHEAD_S)
            if best is None or cost < best[0]:
                best = (cost, tm, tn)
    assert best is not None
    return best[1], best[2]


def _matmul(x, w, *, g=None, w2=None, res=None, extra=None, splits=1, out_dtype=F32):
    t, k = x.shape
    n = w.shape[1]
    ncols = n // splits
    tm, tn = _mm_tiles(t, k, ncols, x.dtype.itemsize, jnp.dtype(out_dtype).itemsize,
                       2 if w2 is not None else 1, res is not None, splits)
    cps = ncols // tn
    grid = (n // tn, t // tm)
    ops = [x]
    specs = [pl.BlockSpec((tm, k), lambda j, i: (i, 0))]
    if g is not None:
        ops.append(g.reshape(1, k))
        specs.append(pl.BlockSpec((1, k), lambda j, i: (0, 0)))
    ops.append(w)
    specs.append(pl.BlockSpec((k, tn), lambda j, i: (0, j)))
    if w2 is not None:
        ops.append(w2)
        specs.append(pl.BlockSpec((k, tn), lambda j, i: (0, j)))
    if res is not None:
        ops.append(res)
        specs.append(pl.BlockSpec((tm, tn), lambda j, i: (i, j)))
    if extra is not None:
        ops += [extra[0], extra[1].reshape(1, LANES)]
        specs += [pl.BlockSpec((k, LANES), lambda j, i: (0, 0)), pl.BlockSpec((1, LANES), lambda j, i: (0, 0))]
    if splits == 1:
        out_specs = [pl.BlockSpec((tm, tn), lambda j, i: (i, j))]
        out_shape = [jax.ShapeDtypeStruct((t, n), out_dtype)]
    else:
        out_specs = [pl.BlockSpec((1, tm, tn), lambda j, i: (j // cps, i, j % cps))]
        out_shape = [jax.ShapeDtypeStruct((splits, t, ncols), out_dtype)]
    if extra is not None:
        out_specs.append(pl.BlockSpec((tm, LANES), lambda j, i: (i, 0)))
        out_shape.append(jax.ShapeDtypeStruct((t, LANES), F32))
    body = functools.partial(_mm_body, has_norm=g is not None, swiglu=w2 is not None,
                             has_res=res is not None, has_extra=extra is not None)
    out = pl.pallas_call(
        body, grid=grid, in_specs=specs, out_specs=out_specs, out_shape=out_shape,
        compiler_params=_params(("arbitrary", "arbitrary")), name="matmul",
    )(*ops)
    return out if extra is not None else out[0]


def _conv_body(x_ref, prev_ref, buf_ref, w_ref, b_ref, o_ref, *, width):
    ts = x_ref.shape[0]
    x = x_ref[...]
    hist = jnp.where(pl.program_id(1) == 0, buf_ref[0], prev_ref[...])
    xx = jnp.concatenate([hist, x], axis=0)
    w = w_ref[...]
    y = b_ref[...]
    for j in range(width):
        lo = SUBLANES - (width - 1) + j
        y = y + xx[lo:lo + ts, :] * w[j:j + 1, :]
    o_ref[...] = (y * jax.nn.sigmoid(y)).astype(o_ref.dtype)


def _causal_conv_silu(x2, col0, chans, buf, w, b, bsz, seq, out_dtype):
    width = w.shape[0]
    assert width - 1 <= SUBLANES
    ts = math.gcd(seq, ROW_TILE)
    nt = seq // ts
    tc = math.gcd(chans, CONV_COLS, col0)
    assert ts % SUBLANES == 0
    cb0 = col0 // tc
    rb = ts // SUBLANES
    if buf is None:
        buf8 = jnp.zeros((bsz, SUBLANES, chans), x2.dtype)
    else:
        buf8 = jnp.pad(buf.astype(x2.dtype), ((0, 0), (SUBLANES - (width - 1), 0), (0, 0)))
    return pl.pallas_call(
        functools.partial(_conv_body, width=width), grid=(bsz, nt, chans // tc),
        in_specs=[
            pl.BlockSpec((ts, tc), lambda bi, i, c: (bi * nt + i, cb0 + c)),
            pl.BlockSpec((SUBLANES, tc), lambda bi, i, c: (jnp.maximum((bi * nt + i) * rb - 1, 0), cb0 + c)),
            pl.BlockSpec((1, SUBLANES, tc), lambda bi, i, c: (bi, 0, c)),
            pl.BlockSpec((width, tc), lambda bi, i, c: (0, c)),
            pl.BlockSpec((1, tc), lambda bi, i, c: (0, c)),
        ],
        out_specs=pl.BlockSpec((ts, tc), lambda bi, i, c: (bi * nt + i, c)),
        out_shape=jax.ShapeDtypeStruct((bsz * seq, chans), out_dtype),
        compiler_params=_params(("parallel", "parallel", "parallel")), name="conv_silu",
    )(x2, x2, buf8, w, b.reshape(1, chans))


def _conv_state(x2, col0, chans, buf, bsz, seq, width):
    tail = x2.reshape(bsz, seq, -1)[:, max(seq - (width - 1), 0):, col0:col0 + chans]
    if seq >= width - 1:
        return tail
    if buf is None:
        buf = jnp.zeros((bsz, width - 1, chans), x2.dtype)
    return jnp.concatenate([buf.astype(x2.dtype), tail], axis=1)[:, -(width - 1):]


def _mlstm_head(q, k, v, o, gc, gr, ng, c, n, m_prev, *, lp, pad, dh):
    q = q * (dh ** -0.5)
    ii_c, ii_r = gc[:, 0:1], gr[0:1, :]
    ff_c, ff_r = -_softplus(-gc[:, 1:2]), -_softplus(-gr[1:2, :])
    if pad:
        ok_c = lax.broadcasted_iota(jnp.int32, (lp, 1), 0) >= pad
        ok_r = lax.broadcasted_iota(jnp.int32, (1, lp), 1) >= pad
        ii_c, ii_r = jnp.where(ok_c, ii_c, NEG), jnp.where(ok_r, ii_r, NEG)
        ff_c, ff_r = jnp.where(ok_c, ff_c, 0.0), jnp.where(ok_r, ff_r, 0.0)
    b_c = _cumsum_rows(ff_c)
    b_r = _cumsum_lanes(ff_r)
    t_i = lax.broadcasted_iota(jnp.int32, (lp, lp), 0)
    s_i = lax.broadcasted_iota(jnp.int32, (lp, lp), 1)
    log_d = jnp.where(s_i <= t_i, b_c + (ii_r - b_r), NEG)
    log_inter = b_c + m_prev
    m_t = jnp.maximum(log_inter, jnp.max(log_d, axis=1, keepdims=True))
    d_mat = jnp.exp(log_d - m_t)
    w_inter = jnp.exp(log_inter - m_t)
    qb, kb, vb = q.astype(BF16), k.astype(BF16), v.astype(BF16)
    s = _dot_nt(qb, kb) * d_mat
    num = _dot(s.astype(BF16), vb) + w_inter * _dot(qb, c.astype(BF16))
    den = jnp.sum(s, axis=1, keepdims=True) + w_inter * jnp.sum(q * n, axis=1, keepdims=True)
    h = num / jnp.maximum(jnp.abs(den), jnp.exp(-m_t))
    h = h * lax.rsqrt(jnp.mean(h * h, axis=1, keepdims=True) + NORM_EPS) * ng
    hg = jax.nn.sigmoid(o) * h[pad:, :]

    b_last = b_c[lp - 1:lp, :]
    m_last = m_t[lp - 1:lp, :]
    w_end = jnp.exp(b_last - b_c + ii_c - m_last)
    f_end = jnp.exp(b_last + m_prev - m_last)
    kw = k * w_end
    c_new = f_end * c + _dot_tn(kw.astype(BF16), vb)
    n_new = f_end * n + jnp.sum(kw, axis=0, keepdims=True)
    return hg, c_new, n_new, m_last


def _mlstm_body(*refs, lr, lp, has_state, dh, hps):
    if has_state:
        (q_ref, k_ref, v_ref, o_ref, gc_ref, gr_ref, ng_ref, c0_ref, n0_ref, m0_ref,
         hg_ref, c1_ref, n1_ref, m1_ref, c_s, n_s, m_s) = refs
    else:
        (q_ref, k_ref, v_ref, o_ref, gc_ref, gr_ref, ng_ref,
         hg_ref, c1_ref, n1_ref, m1_ref, c_s, n_s, m_s) = refs
    ci = pl.program_id(2)
    pad = lp - lr

    @pl.when(ci == 0)
    def _():
        if has_state:
            c_s[...] = c0_ref[0]
            n_s[...] = n0_ref[0]
            m_s[...] = m0_ref[0]
        else:
            c_s[...] = jnp.zeros_like(c_s)
            n_s[...] = jnp.zeros_like(n_s)
            m_s[...] = jnp.zeros_like(m_s)

    for u in range(hps):
        cs = slice(u * dh, (u + 1) * dh)
        hg, c_new, n_new, m_new = _mlstm_head(
            _pad_front(q_ref[:, cs].astype(F32), pad), _pad_front(k_ref[:, cs].astype(F32), pad),
            _pad_front(v_ref[:, cs].astype(F32), pad), o_ref[:, cs], gc_ref[u, 0], gr_ref[u, 0], ng_ref[:, cs],
            c_s[u], n_s[u], m_s[u], lp=lp, pad=pad, dh=dh)
        hg_ref[:, cs] = hg.astype(hg_ref.dtype)
        c_s[u] = c_new
        n_s[u] = n_new
        m_s[u] = m_new

    @pl.when(ci == pl.num_programs(2) - 1)
    def _():
        c1_ref[0] = c_s[...]
        n1_ref[0] = n_s[...]
        m1_ref[0] = m_s[...]


def _mlstm_core(qk, vo, gates, norm_g, bsz, seq, heads, state, out_dtype):
    inner = qk.shape[1] // 2
    dh = inner // heads
    hps = math.gcd(heads, ML_HEADS_PER_STEP)
    nhg = heads // hps
    lr = math.gcd(seq, ML_CHUNK)
    lp = lr if lr % SEQ_PAD == 0 else SEQ_PAD
    nc = seq // lr
    pad = lp - lr
    gi = gates[:, :heads].reshape(bsz, seq, heads)
    gf = gates[:, heads:2 * heads].reshape(bsz, seq, heads)
    g = jnp.stack([gi, gf], axis=-1)
    g = g.reshape(bsz, nc, lr, heads, 2)
    g = jnp.pad(g, ((0, 0), (0, 0), (pad, 0), (0, 0), (0, 0)))
    gc = jnp.transpose(g, (3, 0, 1, 2, 4)).reshape(heads, bsz, nc * lp, 2)
    gr = jnp.transpose(g, (3, 0, 4, 1, 2)).reshape(heads, bsz, 2, nc * lp)
    has_state = state is not None
    ops = [qk, qk, vo, vo, gc, gr, norm_g.reshape(1, inner)]
    specs = [
        pl.BlockSpec((lr, hps * dh), lambda b, h, c: (b * nc + c, h)),
        pl.BlockSpec((lr, hps * dh), lambda b, h, c: (b * nc + c, nhg + h)),
        pl.BlockSpec((lr, hps * dh), lambda b, h, c: (b * nc + c, h)),
        pl.BlockSpec((lr, hps * dh), lambda b, h, c: (b * nc + c, nhg + h)),
        pl.BlockSpec((hps, 1, lp, 2), lambda b, h, c: (h, b, c, 0)),
        pl.BlockSpec((hps, 1, 2, lp), lambda b, h, c: (h, b, 0, c)),
        pl.BlockSpec((1, hps * dh), lambda b, h, c: (0, h)),
    ]
    if has_state:
        c0, n0, m0 = state
        ops += [c0, n0.reshape(bsz, heads, 1, dh), m0.reshape(bsz, heads, 1, 1)]
        specs += [
            pl.BlockSpec((1, hps, dh, dh), lambda b, h, c: (b, h, 0, 0)),
            pl.BlockSpec((1, hps, 1, dh), lambda b, h, c: (b, h, 0, 0)),
            pl.BlockSpec((1, hps, 1, 1), lambda b, h, c: (b, h, 0, 0)),
        ]
    body = functools.partial(_mlstm_body, lr=lr, lp=lp, has_state=has_state, dh=dh, hps=hps)
    hg, c1, n1, m1 = pl.pallas_call(
        body, grid=(bsz, nhg, nc), in_specs=specs,
        out_specs=[
            pl.BlockSpec((lr, hps * dh), lambda b, h, c: (b * nc + c, h)),
            pl.BlockSpec((1, hps, dh, dh), lambda b, h, c: (b, h, 0, 0)),
            pl.BlockSpec((1, hps, 1, dh), lambda b, h, c: (b, h, 0, 0)),
            pl.BlockSpec((1, hps, 1, 1), lambda b, h, c: (b, h, 0, 0)),
        ],
        out_shape=[
            jax.ShapeDtypeStruct((bsz * seq, inner), out_dtype),
            jax.ShapeDtypeStruct((bsz, heads, dh, dh), F32),
            jax.ShapeDtypeStruct((bsz, heads, 1, dh), F32),
            jax.ShapeDtypeStruct((bsz, heads, 1, 1), F32),
        ],
        scratch_shapes=[pltpu.VMEM((hps, dh, dh), F32), pltpu.VMEM((hps, 1, dh), F32),
                        pltpu.VMEM((hps, 1, 1), F32)],
        compiler_params=_params(("parallel", "parallel", "arbitrary")), name="mlstm_core",
    )(*ops)
    return hg, c1, n1.reshape(bsz, heads, dh), m1.reshape(bsz, heads)


def _ssd_group(x, bm, cm, z, dtc, dtr, pr, pc, ng, h, *, lp, pad, hpg, hd):
    dd_c = _softplus(dtc + pr[0:1, :])
    dd_r = _softplus(dtr + pc[:, 0:1])
    if pad:
        dd_c = jnp.where(lax.broadcasted_iota(jnp.int32, (lp, hpg), 0) >= pad, dd_c, 0.0)
        dd_r = jnp.where(lax.broadcasted_iota(jnp.int32, (hpg, lp), 1) >= pad, dd_r, 0.0)
    cum_c = _cumsum_rows(dd_c * pr[1:2, :])
    cum_r = _cumsum_lanes(dd_r * pc[:, 1:2])
    cb = _dot_nt(cm, bm)
    t_i = lax.broadcasted_iota(jnp.int32, (lp, lp), 0)
    s_i = lax.broadcasted_iota(jnp.int32, (lp, lp), 1)
    tri = s_i <= t_i
    head_of_lane = lax.broadcasted_iota(jnp.int32, (1, hpg * hd), 1) // hd
    y = jnp.zeros((lp, hpg * hd), F32)
    for r in range(hpg):
        seg = jnp.where(tri, cum_c[:, r:r + 1] - cum_r[r:r + 1, :], NEG)
        w = jnp.exp(seg) * cb * dd_r[r:r + 1, :]
        xr = jnp.where(head_of_lane == r, x, 0.0).astype(BF16)
        y = y + _dot(w.astype(BF16), xr)
    y = y + _expand_lanes(jnp.exp(cum_c), hd) * _dot_nt(cm, h.astype(BF16))
    y = y + _expand_lanes(pr[2:3, :], hd) * x
    yv = y[pad:, :] * (z * jax.nn.sigmoid(z))
    yv = yv * lax.rsqrt(jnp.mean(yv * yv, axis=1, keepdims=True) + NORM_EPS) * ng
    w_end = jnp.exp(cum_c[lp - 1:lp, :] - cum_c) * dd_c
    xw = x * _expand_lanes(w_end, hd)
    decay = _expand_rows(jnp.exp(cum_r[:, lp - 1:lp]), hd)
    return yv, decay * h + _dot_tn(xw.astype(BF16), bm)


def _ssd_body(*refs, lr, lp, has_state, hpg, hd, gps):
    if has_state:
        (x_ref, bm_ref, cm_ref, z_ref, dtc_ref, dtr_ref, pr_ref, pc_ref, ng_ref, h0_ref,
         y_ref, h1_ref, h_s) = refs
    else:
        (x_ref, bm_ref, cm_ref, z_ref, dtc_ref, dtr_ref, pr_ref, pc_ref, ng_ref,
         y_ref, h1_ref, h_s) = refs
    ci = pl.program_id(2)
    pad = lp - lr
    gw = hpg * hd
    ns = bm_ref.shape[1] // gps

    @pl.when(ci == 0)
    def _():
        if has_state:
            h_s[...] = h0_ref[0]
        else:
            h_s[...] = jnp.zeros_like(h_s)

    for gi in range(gps):
        xs = slice(gi * gw, (gi + 1) * gw)
        bs = slice(gi * ns, (gi + 1) * ns)
        yv, h_new = _ssd_group(
            _pad_front(x_ref[:, xs], pad), _pad_front(bm_ref[:, bs], pad).astype(BF16),
            _pad_front(cm_ref[:, bs], pad).astype(BF16), z_ref[:, xs], dtc_ref[gi, 0], dtr_ref[gi, 0],
            pr_ref[gi], pc_ref[gi], ng_ref[:, xs], h_s[gi], lp=lp, pad=pad, hpg=hpg, hd=hd)
        y_ref[:, xs] = yv.astype(y_ref.dtype)
        h_s[gi] = h_new

    @pl.when(ci == pl.num_programs(2) - 1)
    def _():
        h1_ref[0] = h_s[...]


def _ssd_core(zx, xbc, dt_raw, dt_bias, a_neg, d_skip, norm_g, bsz, seq, groups, nstate, hd, h0, out_dtype):
    inner = norm_g.shape[0]
    heads = inner // hd
    hpg = heads // groups
    gw = hpg * hd
    gps = math.gcd(groups, SSD_GROUPS_PER_STEP)
    ngs = groups // gps
    lr = math.gcd(seq, SSD_CHUNK)
    lp = lr if lr % SEQ_PAD == 0 else SEQ_PAD
    nc = seq // lr
    pad = lp - lr
    d = dt_raw[:, :heads].reshape(bsz, nc, lr, groups, hpg)
    d = jnp.pad(d, ((0, 0), (0, 0), (pad, 0), (0, 0), (0, 0)))
    dtc = jnp.transpose(d, (3, 0, 1, 2, 4)).reshape(groups, bsz, nc * lp, hpg)
    dtr = jnp.transpose(d, (3, 0, 4, 1, 2)).reshape(groups, bsz, hpg, nc * lp)
    par = jnp.stack([dt_bias, a_neg, d_skip]).astype(F32).reshape(3, groups, hpg)
    pr = jnp.transpose(par, (1, 0, 2))
    pc = jnp.transpose(par, (1, 2, 0))
    b_off = inner // (gps * nstate)
    c_off = b_off + ngs
    has_state = h0 is not None
    ops = [xbc, xbc, xbc, zx, dtc, dtr, pr, pc, norm_g.reshape(1, inner)]
    specs = [
        pl.BlockSpec((lr, gps * gw), lambda b, g, c: (b * nc + c, g)),
        pl.BlockSpec((lr, gps * nstate), lambda b, g, c: (b * nc + c, b_off + g)),
        pl.BlockSpec((lr, gps * nstate), lambda b, g, c: (b * nc + c, c_off + g)),
        pl.BlockSpec((lr, gps * gw), lambda b, g, c: (b * nc + c, g)),
        pl.BlockSpec((gps, 1, lp, hpg), lambda b, g, c: (g, b, c, 0)),
        pl.BlockSpec((gps, 1, hpg, lp), lambda b, g, c: (g, b, 0, c)),
        pl.BlockSpec((gps, 3, hpg), lambda b, g, c: (g, 0, 0)),
        pl.BlockSpec((gps, hpg, 3), lambda b, g, c: (g, 0, 0)),
        pl.BlockSpec((1, gps * gw), lambda b, g, c: (0, g)),
    ]
    if has_state:
        ops.append(h0.reshape(bsz, groups, gw, nstate))
        specs.append(pl.BlockSpec((1, gps, gw, nstate), lambda b, g, c: (b, g, 0, 0)))
    body = functools.partial(_ssd_body, lr=lr, lp=lp, has_state=has_state, hpg=hpg, hd=hd, gps=gps)
    y, h1 = pl.pallas_call(
        body, grid=(bsz, ngs, nc), in_specs=specs,
        out_specs=[
            pl.BlockSpec((lr, gps * gw), lambda b, g, c: (b * nc + c, g)),
            pl.BlockSpec((1, gps, gw, nstate), lambda b, g, c: (b, g, 0, 0)),
        ],
        out_shape=[
            jax.ShapeDtypeStruct((bsz * seq, inner), out_dtype),
            jax.ShapeDtypeStruct((bsz, groups, gw, nstate), F32),
        ],
        scratch_shapes=[pltpu.VMEM((gps, gw, nstate), F32)],
        compiler_params=_params(("parallel", "parallel", "arbitrary")), name="ssd_core",
    )(*ops)
    return y, h1.reshape(bsz, heads, hd, nstate)


def _sb_logs(z):
    t = jnp.log(1.0 + jnp.exp2(jnp.abs(z) * (-LOG2E)))
    lb = jnp.minimum(z, 0.0) - t
    return lb, lb - z


def _sbp_body(bias_ref, q_ref, k_ref, v_ref, o_ref, *, blk, hd, hps):
    hg = pl.program_id(1)
    qi = pl.program_id(2)
    scale = hd ** -0.5
    bias = [bias_ref[hg * hps + u] for u in range(hps)]
    qb = [q_ref[0, :, u * hd:(u + 1) * hd].astype(BF16) for u in range(hps)]
    j_i = lax.broadcasted_iota(jnp.int32, (2 * blk, blk), 0)
    s_i = lax.broadcasted_iota(jnp.int32, (2 * blk, blk), 1)
    tri2 = jnp.where((j_i % blk) > s_i, 1.0, 0.0).astype(BF16)
    t_i = lax.broadcasted_iota(jnp.int32, (blk, blk), 0)
    strict = lax.broadcasted_iota(jnp.int32, (blk, blk), 1) < t_i

    def rows(kb):
        return pl.ds(pl.multiple_of(kb * blk, blk), blk)

    def scores(kb, u):
        return _dot_nt(qb[u], k_ref[0, rows(kb), u * hd:(u + 1) * hd].astype(BF16))

    def weights(s, run, u, mask):
        lb, lr = _sb_logs(s * scale + bias[u])
        if mask is not None:
            lr = jnp.where(mask, lr, 0.0)
        hi, lo = _split2(lr)
        local = _dot(jnp.concatenate([hi, lo], axis=1), tri2)
        a = jnp.exp(lb + (run + local))
        if mask is not None:
            a = jnp.where(mask, a, 0.0)
        return a.astype(BF16), run + (local[:, 0:1] + lr[:, 0:1])

    def weighted_values(a, kb, u):
        return _dot(a, v_ref[0, rows(kb), u * hd:(u + 1) * hd].astype(BF16))

    def step(j, carry):
        kb = qi - 1 - j
        out = []
        for u in range(hps):
            run, acc, s, a_prev = carry[u]
            s_next = scores(jnp.maximum(kb - 1, 0), u)
            acc = acc + weighted_values(a_prev, kb + 1, u)
            a, run = weights(s, run, u, None)
            out.append((run, acc, s_next, a))
        return tuple(out)

    carry = []
    for u in range(hps):
        a, run = weights(scores(qi, u), jnp.zeros((blk, 1), F32), u, strict)
        carry.append((run, jnp.zeros((blk, hd), F32), scores(jnp.maximum(qi - 1, 0), u), a))
    carry = lax.fori_loop(0, qi, step, tuple(carry))
    for u in range(hps):
        acc = carry[u][1] + weighted_values(carry[u][3], 0, u)
        o_ref[:, u * hd:(u + 1) * hd] = acc.astype(o_ref.dtype)


def _sb_prompt(qkv, bias, bsz, seq, heads, hd, out_dtype):
    blk = math.gcd(seq, SB_BLOCK)
    nq = seq // blk
    hps = math.gcd(heads, SB_HEADS_PER_STEP)
    nhg = heads // hps
    body = functools.partial(_sbp_body, blk=blk, hd=hd, hps=hps)
    return pl.pallas_call(
        body, grid=(bsz, nhg, nq),
        in_specs=[
            pl.BlockSpec(memory_space=pltpu.SMEM),
            pl.BlockSpec((1, blk, hps * hd), lambda b, h, i: (0, b * nq + i, h)),
            pl.BlockSpec((1, seq, hps * hd), lambda b, h, i: (1, b, h)),
            pl.BlockSpec((1, seq, hps * hd), lambda b, h, i: (2, b, h)),
        ],
        out_specs=pl.BlockSpec((blk, hps * hd), lambda b, h, i: (b * nq + i, h)),
        out_shape=jax.ShapeDtypeStruct((bsz * seq, heads * hd), out_dtype),
        compiler_params=_params(("parallel", "parallel", "arbitrary")), name="sb_prompt",
    )(bias.astype(F32), qkv, qkv, qkv)


def _sbs_body(pt_ref, q_ref, kn_ref, vn_ref, *rest, sq, heads, hd, page, ppg):
    del pt_ref
    kp_refs, vp_refs = rest[:ppg], rest[ppg:2 * ppg]
    bias_ref, o_ref, run_s, acc_s, qbd_s = rest[2 * ppg:]
    p = pl.program_id(1)
    width = heads * hd
    cols = qbd_s.shape[0]
    scale = hd ** -0.5

    def own_head():
        row_i = lax.broadcasted_iota(jnp.int32, (cols, width), 0)
        lane_i = lax.broadcasted_iota(jnp.int32, (cols, width), 1)
        return (row_i // sq) == (lane_i // hd)

    def process(kks, vvs, mask):
        n = len(kks)
        j_i = lax.broadcasted_iota(jnp.int32, (page, 2 * page), 1)
        s_i = lax.broadcasted_iota(jnp.int32, (page, 2 * page), 0)
        tri2 = jnp.where((j_i % page) > s_i, 1.0, 0.0).astype(BF16)
        kcat = kks[0] if n == 1 else jnp.concatenate(kks, axis=0)
        z = _dot_nt(kcat, qbd_s[...]) * scale + bias_ref[...]
        lb, lr = _sb_logs(z)
        if mask is not None:
            lr = jnp.where(mask, lr, 0.0)
        run = run_s[...]
        parts = []
        for i in range(n):
            rs = slice(i * page, (i + 1) * page)
            hi, lo = _split2(lr[rs])
            local = _dot(tri2, jnp.concatenate([hi, lo], axis=0))
            a = jnp.exp(lb[rs] + (run + local))
            if mask is not None:
                a = jnp.where(mask, a, 0.0)
            parts.append(a.astype(BF16))
            run = run + (local[0:1, :] + lr[rs][0:1, :])
        run_s[...] = run
        acat = parts[0] if n == 1 else jnp.concatenate(parts, axis=0)
        vcat = vvs[0] if n == 1 else jnp.concatenate(vvs, axis=0)
        acc_s[...] += _dot_tn(acat, vcat)

    def load_page(ref):
        return jnp.concatenate([ref[0, 0, pl.ds(h, page, stride=heads), :] for h in range(heads)],
                               axis=1).astype(BF16)

    @pl.when(p == 0)
    def _():
        run_s[...] = jnp.zeros_like(run_s)
        acc_s[...] = jnp.zeros_like(acc_s)
        q = q_ref[0]
        tiled = jnp.concatenate([q] * heads + [jnp.zeros((cols - heads * sq, width), F32)], axis=0)
        qbd_s[...] = jnp.where(own_head(), tiled, 0.0).astype(BF16)
        key_i = lax.broadcasted_iota(jnp.int32, (page, cols), 0)
        qry_i = lax.broadcasted_iota(jnp.int32, (page, cols), 1) % sq
        zeros = jnp.zeros((page - sq, width), F32)
        process([jnp.concatenate([kn_ref[0], zeros], axis=0).astype(BF16)],
                [jnp.concatenate([vn_ref[0], zeros], axis=0).astype(BF16)], key_i < qry_i)

    @pl.when(p > 0)
    def _():
        process([load_page(r) for r in kp_refs], [load_page(r) for r in vp_refs], None)

    @pl.when(p == pl.num_programs(1) - 1)
    def _():
        acc = jnp.where(own_head(), acc_s[...], 0.0)
        out = acc[0:sq, :]
        for h in range(1, heads):
            out = out + acc[h * sq:(h + 1) * sq, :]
        o_ref[...] = out.astype(o_ref.dtype)


def _sb_sample(qkv, pool_k, pool_v, layer, page_table, bias, bsz, sq, heads, hd, out_dtype):
    width = heads * hd
    n_pages = page_table.shape[1]
    page = pool_k.shape[2]
    cols = SEQ_PAD
    ppg = math.gcd(n_pages, SB_PAGES_PER_STEP)
    assert heads * sq <= cols and sq <= page
    bias_cols = jnp.pad(jnp.repeat(bias.astype(F32), sq), (0, cols - heads * sq)).reshape(1, cols)
    pool_k = pool_k.reshape(pool_k.shape[0], pool_k.shape[1], page * heads, hd)
    pool_v = pool_v.reshape(pool_v.shape[0], pool_v.shape[1], page * heads, hd)

    def page_spec(i):
        def idx(b, s, pt):
            return (layer, pt[b, n_pages - 1 - (jnp.maximum(s - 1, 0) * ppg + i)], 0, 0)
        return pl.BlockSpec((1, 1, page * heads, hd), idx)

    body = functools.partial(_sbs_body, sq=sq, heads=heads, hd=hd, page=page, ppg=ppg)
    grid_spec = pltpu.PrefetchScalarGridSpec(
        num_scalar_prefetch=1, grid=(bsz, n_pages // ppg + 1),
        in_specs=[
            pl.BlockSpec((1, sq, width), lambda b, s, pt: (0, b, 0)),
            pl.BlockSpec((1, sq, width), lambda b, s, pt: (1, b, 0)),
            pl.BlockSpec((1, sq, width), lambda b, s, pt: (2, b, 0)),
            *[page_spec(i) for i in range(ppg)],
            *[page_spec(i) for i in range(ppg)],
            pl.BlockSpec((1, cols), lambda b, s, pt: (0, 0)),
        ],
        out_specs=pl.BlockSpec((sq, width), lambda b, s, pt: (b, 0)),
        scratch_shapes=[pltpu.VMEM((1, cols), F32), pltpu.VMEM((cols, width), F32),
                        pltpu.VMEM((cols, width), BF16)],
    )
    return pl.pallas_call(
        body, grid_spec=grid_spec,
        out_shape=jax.ShapeDtypeStruct((bsz * sq, width), out_dtype),
        compiler_params=_params(("parallel", "arbitrary")), name="sb_sample",
    )(page_table, qkv, qkv, qkv, *([pool_k] * ppg), *([pool_v] * ppg), bias_cols)


def _xa_body(x_ref, g_ref, wq_ref, mk_ref, mv_ref, wo_ref, o_ref, *, heads, hd):
    scale = hd ** -0.5
    x = x_ref[...]
    xn = (x * lax.rsqrt(jnp.mean(x * x, axis=-1, keepdims=True) + NORM_EPS) * g_ref[...]).astype(BF16)
    q = _dot(xn, wq_ref[...]).astype(BF16)
    outs = []
    for h in range(heads):
        cs = slice(h * hd, (h + 1) * hd)
        s = _dot_nt(q[:, cs], mk_ref[0, :, cs].astype(BF16)) * scale
        e = jnp.exp(s - jnp.max(s, axis=1, keepdims=True))
        pr = e / jnp.sum(e, axis=1, keepdims=True)
        outs.append(_dot(pr.astype(BF16), mv_ref[0, :, cs].astype(BF16)).astype(BF16))
    o_ref[...] = x + _dot(jnp.concatenate(outs, axis=1), wo_ref[...])


def _cross_attn_layer(x, g, wq, mk, mv, mk_idx, mv_idx, wo, bsz, seq, mem_len, heads):
    d = x.shape[1]
    tq = math.gcd(seq, XA_QBLOCK)
    nq = seq // tq
    body = functools.partial(_xa_body, heads=heads, hd=d // heads)
    return pl.pallas_call(
        body, grid=(bsz, nq),
        in_specs=[
            pl.BlockSpec((tq, d), lambda b, i: (b * nq + i, 0)),
            pl.BlockSpec((1, d), lambda b, i: (0, 0)),
            pl.BlockSpec((d, d), lambda b, i: (0, 0)),
            pl.BlockSpec((1, mem_len, d), lambda b, i: (mk_idx, b, 0)),
            pl.BlockSpec((1, mem_len, d), lambda b, i: (mv_idx, b, 0)),
            pl.BlockSpec((d, d), lambda b, i: (0, 0)),
        ],
        out_specs=pl.BlockSpec((tq, d), lambda b, i: (b * nq + i, 0)),
        out_shape=jax.ShapeDtypeStruct((bsz * seq, d), F32),
        compiler_params=_params(("parallel", "parallel")), name="cross_attn",
    )(x, g.reshape(1, d), wq, mk, mv, wo)


def _pad_cols(w, n):
    return jnp.pad(w, ((0, 0), (0, n - w.shape[1])))


def _prep_weights(p):
    w = {}
    ml_inner = p['ml_w_up'].shape[2]
    ssd_heads = p['ssd_dt_bias'].shape[1]
    ssd_main = p['ssd_w_in'].shape[2] - ssd_heads
    ffn_hidden = p['ffn_w_out'].shape[1]
    w['ml_w_up'] = p['ml_w_up'].astype(BF16)
    w['ml_w_qk'] = p['ml_w_qk'].astype(BF16)
    w['ml_w_vo'] = p['ml_w_vog'][:, :, :2 * ml_inner].astype(BF16)
    w['ml_w_g'] = jnp.stack([_pad_cols(m[:, 2 * ml_inner:], LANES) for m in p['ml_w_vog']]).astype(BF16)
    w['ml_b_g'] = jnp.stack([jnp.pad(b, (0, LANES - b.shape[0])) for b in p['ml_b_gate']]).astype(F32)
    w['ml_w_down'] = p['ml_w_down'].astype(BF16)
    w['sb_w_qkv'] = p['sb_w_qkv'].astype(BF16)
    w['sb_w_o'] = p['sb_w_o'].astype(BF16)
    w['ssd_w_main'] = p['ssd_w_in'][:, :, :ssd_main].astype(BF16)
    w['ssd_w_dt'] = jnp.stack([_pad_cols(m[:, ssd_main:], LANES) for m in p['ssd_w_in']]).astype(BF16)
    w['ssd_w_out'] = p['ssd_w_out'].astype(BF16)
    w['ssd_a'] = -jnp.exp(p['ssd_a_log'].astype(F32))
    w['xa_w_q'] = p['xa_w_q'].astype(BF16)
    w['xa_w_kv'] = p['xa_w_kv'].astype(BF16)
    w['xa_w_o'] = p['xa_w_o'].astype(BF16)
    w['ffn_w_gate'] = p['ffn_w_in'][:, :, :ffn_hidden].astype(BF16)
    w['ffn_w_up'] = p['ffn_w_in'][:, :, ffn_hidden:].astype(BF16)
    w['ffn_w_out'] = p['ffn_w_out'].astype(BF16)
    return w


def _trunk(x3, mem_kv, ml_states, sb_pools, ssd_states, page_table, p, w, dims, act_dtype):
    bsz, seq, d = x3.shape
    depth = p['norm_mix'].shape[0]
    ml_heads, sb_heads, xa_heads, ssd_groups, ssd_state, ssd_hd, mem_len = dims
    x = x3.reshape(bsz * seq, d)
    ml_new, sb_new, ssd_new = [], [], []
    n_ml = n_sb = n_ssd = 0
    for layer in range(depth):
        kind = layer % 3
        if kind == 0:
            j = n_ml
            n_ml += 1
            inner = p['ml_w_up'].shape[2]
            width = p['ml_conv_w'].shape[1]
            xm = _matmul(x, w['ml_w_up'][j], g=p['norm_mix'][layer])
            st = None if ml_states is None else ml_states[j]
            buf = None if st is None else st[3]
            xc = _causal_conv_silu(xm, 0, inner, buf, p['ml_conv_w'][j], p['ml_conv_b'][j], bsz, seq, act_dtype)
            qk = _matmul(xc, w['ml_w_qk'][j])
            vo, gates = _matmul(xm, w['ml_w_vo'][j], extra=(w['ml_w_g'][j], w['ml_b_g'][j]))
            hg, c1, n1, m1 = _mlstm_core(qk, vo, gates, p['ml_norm'][j], bsz, seq, ml_heads,
                                         None if st is None else st[:3], act_dtype)
            x = _matmul(hg, w['ml_w_down'][j], res=x)
            ml_new.append((c1, n1, m1, _conv_state(xm, 0, inner, buf, bsz, seq, width)))
        elif kind == 1:
            j = n_sb
            n_sb += 1
            hd = d // sb_heads
            qkv = _matmul(x, w['sb_w_qkv'][j], g=p['norm_mix'][layer], splits=3)
            if sb_pools is None:
                o = _sb_prompt(qkv, p['sb_bias'][j], bsz, seq, sb_heads, hd, act_dtype)
            else:
                o = _sb_sample(qkv, sb_pools[0], sb_pools[1], j, page_table, p['sb_bias'][j],
                               bsz, seq, sb_heads, hd, act_dtype)
            x = _matmul(o, w['sb_w_o'][j], res=x)
            shp = (bsz, seq, sb_heads, hd)
            sb_new.append((qkv[1].reshape(shp), qkv[2].reshape(shp)))
        else:
            j = n_ssd
            n_ssd += 1
            inner = p['ssd_norm'].shape[1]
            conv_ch = p['ssd_conv_w'].shape[2]
            width = p['ssd_conv_w'].shape[1]
            zx, dt_raw = _matmul(x, w['ssd_w_main'][j], g=p['norm_mix'][layer],
                                 extra=(w['ssd_w_dt'][j], jnp.zeros((LANES,), F32)))
            st = None if ssd_states is None else ssd_states[j]
            buf = None if st is None else st[1]
            xbc = _causal_conv_silu(zx, inner, conv_ch, buf, p['ssd_conv_w'][j], p['ssd_conv_b'][j],
                                    bsz, seq, F32)
            y, h1 = _ssd_core(zx, xbc, dt_raw, p['ssd_dt_bias'][j], w['ssd_a'][j],
                              p['ssd_d'][j], p['ssd_norm'][j], bsz, seq, ssd_groups, ssd_state, ssd_hd,
                              None if st is None else st[0], act_dtype)
            x = _matmul(y, w['ssd_w_out'][j], res=x)
            ssd_new.append((h1, _conv_state(zx, inner, conv_ch, buf, bsz, seq, width)))
        mk, mv, mk_idx, mv_idx = mem_kv[layer]
        x = _cross_attn_layer(x, p['norm_xa'][layer], w['xa_w_q'][layer], mk, mv, mk_idx, mv_idx,
                              w['xa_w_o'][layer], bsz, seq, mem_len, xa_heads)
        hid = _matmul(x, w['ffn_w_gate'][layer], g=p['norm_ffn'][layer], w2=w['ffn_w_up'][layer],
                      out_dtype=act_dtype)
        x = _matmul(hid, w['ffn_w_out'][layer], res=x)
    y = _final_norm(x, p['norm_final'])
    return y.reshape(bsz, seq, d), ml_new, sb_new, ssd_new


def _norm_body(x_ref, g_ref, o_ref):
    x = x_ref[...]
    o_ref[...] = x * lax.rsqrt(jnp.mean(x * x, axis=-1, keepdims=True) + NORM_EPS) * g_ref[...]


def _final_norm(x, g):
    t, d = x.shape
    tm = min(ROW_TILE, t)
    return pl.pallas_call(
        _norm_body, grid=(t // tm,),
        in_specs=[pl.BlockSpec((tm, d), lambda i: (i, 0)), pl.BlockSpec((1, d), lambda i: (0, 0))],
        out_specs=pl.BlockSpec((tm, d), lambda i: (i, 0)),
        out_shape=jax.ShapeDtypeStruct((t, d), F32),
        compiler_params=_params(("parallel",)), name="final_norm",
    )(x, g.reshape(1, d))


def _stack(items, idx):
    return jnp.stack([it[idx] for it in items])


def kernel(x_prompt, x_sample, cache_mem_k, cache_mem_v, cache_sb_k, cache_sb_v, state_ml_c, state_ml_n, state_ml_m, state_ml_conv, state_ssd_h, state_ssd_conv, page_table, mem_prompt, norm_mix, norm_xa, norm_ffn, norm_mem, norm_final, ml_w_up, ml_conv_w, ml_conv_b, ml_w_qk, ml_w_vog, ml_b_gate, ml_norm, ml_w_down, sb_w_qkv, sb_bias, sb_w_o, ssd_w_in, ssd_conv_w, ssd_conv_b, ssd_dt_bias, ssd_a_log, ssd_d, ssd_norm, ssd_w_out, xa_w_q, xa_w_kv, xa_w_o, ffn_w_in, ffn_w_out):
    p = dict(norm_mix=norm_mix, norm_xa=norm_xa, norm_ffn=norm_ffn, norm_final=norm_final,
             ml_w_up=ml_w_up, ml_conv_w=ml_conv_w, ml_conv_b=ml_conv_b, ml_w_qk=ml_w_qk,
             ml_w_vog=ml_w_vog, ml_b_gate=ml_b_gate, ml_norm=ml_norm, ml_w_down=ml_w_down,
             sb_w_qkv=sb_w_qkv, sb_bias=sb_bias, sb_w_o=sb_w_o,
             ssd_w_in=ssd_w_in, ssd_conv_w=ssd_conv_w, ssd_conv_b=ssd_conv_b, ssd_dt_bias=ssd_dt_bias,
             ssd_a_log=ssd_a_log, ssd_d=ssd_d, ssd_norm=ssd_norm, ssd_w_out=ssd_w_out,
             xa_w_q=xa_w_q, xa_w_kv=xa_w_kv, xa_w_o=xa_w_o, ffn_w_in=ffn_w_in, ffn_w_out=ffn_w_out)
    w = _prep_weights(p)
    depth = norm_mix.shape[0]
    d = x_prompt.shape[2]
    bp, mem_len = mem_prompt.shape[0], mem_prompt.shape[1]
    bd = x_sample.shape[0]
    ml_heads = state_ml_c.shape[2]
    sb_heads = cache_sb_k.shape[3]
    xa_heads = cache_mem_k.shape[3]
    ssd_hd, ssd_state = state_ssd_h.shape[3], state_ssd_h.shape[4]
    ssd_groups = (ssd_conv_w.shape[2] - ssd_norm.shape[1]) // (2 * ssd_state)
    n_ml, n_ssd = state_ml_c.shape[0], state_ssd_h.shape[0]
    dims = (ml_heads, sb_heads, xa_heads, ssd_groups, ssd_state, ssd_hd, mem_len)

    mem2 = mem_prompt.reshape(bp * mem_len, d)
    mem_kv_p = [_matmul(mem2, w['xa_w_kv'][l], g=norm_mem[l], splits=2) for l in range(depth)]
    y_prompt, ml_p, sb_p, ssd_p = _trunk(x_prompt, [(kv, kv, 0, 1) for kv in mem_kv_p], None, None, None,
                                         None, p, w, dims, BF16)
    shp = (bp, mem_len, xa_heads, d // xa_heads)
    mem_k_p = jnp.stack([kv[0].reshape(shp) for kv in mem_kv_p])
    mem_v_p = jnp.stack([kv[1].reshape(shp) for kv in mem_kv_p])

    mem_kv_s = [(cache_mem_k[l].reshape(1, bd * mem_len, d), cache_mem_v[l].reshape(1, bd * mem_len, d), 0, 0)
                for l in range(depth)]
    ml_cache = [(state_ml_c[j], state_ml_n[j], state_ml_m[j], state_ml_conv[j]) for j in range(n_ml)]
    ssd_cache = [(state_ssd_h[j], state_ssd_conv[j]) for j in range(n_ssd)]
    y_sample, ml_s, sb_s, ssd_s = _trunk(x_sample, mem_kv_s, ml_cache, (cache_sb_k, cache_sb_v), ssd_cache,
                                         page_table, p, w, dims, F32)

    return (y_prompt, y_sample, mem_k_p, mem_v_p,
            _stack(sb_p, 0), _stack(sb_p, 1),
            _stack(ml_p, 0), _stack(ml_p, 1), _stack(ml_p, 2), _stack(ml_p, 3),
            _stack(ssd_p, 0), _stack(ssd_p, 1),
            _stack(sb_s, 0), _stack(sb_s, 1),
            _stack(ml_s, 0), _stack(ml_s, 1), _stack(ml_s, 2), _stack(ml_s, 3),
            _stack(ssd_s, 0), _stack(ssd_s, 1))
```

```python
import functools
import math

import jax
import jax.numpy as jnp
from jax import lax
from jax.experimental import pallas as pl
from jax.experimental.pallas import tpu as pltpu

F32 = jnp.float32
BF16 = jnp.bfloat16
NORM_EPS = 1e-6
NEG = -1e30
LOG2E = 1.4426950408889634
SUBLANES = 8
VMEM_LIMIT_BYTES = 56 * 1024 * 1024
LANES = 128
MM_VMEM_BUDGET_BYTES = 44 * 1024 * 1024
MXU_FLOPS = 1.0e15
HBM_BYTES_PER_S = 3.0e12
STEP_OVERHEAD_S = 0.35e-6
ROW_TILE = 256
ML_HEADS_PER_STEP = 2
SEQ_PAD = 128
ML_CHUNK = 256
SSD_CHUNK = 128
SSD_GROUPS_PER_STEP = 4
SB_BLOCK = 256
SB_HEADS_PER_STEP = 2
SB_PAGES_PER_STEP = 8
XA_QBLOCK = 256
CONV_COLS = 2048


def _params(sem):
    return pltpu.CompilerParams(dimension_semantics=sem, vmem_limit_bytes=VMEM_LIMIT_BYTES)


def _dot(a, b):
    return jnp.dot(a, b, preferred_element_type=F32)


def _dot_nt(a, b):
    return lax.dot_general(a, b, (((1,), (1,)), ((), ())), preferred_element_type=F32)


def _dot_tn(a, b):
    return lax.dot_general(a, b, (((0,), (0,)), ((), ())), preferred_element_type=F32)


def _split3(x):
    hi = x.astype(BF16)
    r = x - hi.astype(F32)
    mid = r.astype(BF16)
    lo = (r - mid.astype(F32)).astype(BF16)
    return hi, mid, lo


def _split2(x):
    hi = x.astype(BF16)
    lo = (x - hi.astype(F32)).astype(BF16)
    return hi, lo


def _cumsum_rows(x):
    n = x.shape[0]
    t = lax.broadcasted_iota(jnp.int32, (n, n), 0)
    s = lax.broadcasted_iota(jnp.int32, (n, n), 1)
    m = jnp.where(s <= t, 1.0, 0.0).astype(BF16)
    hi, mid, lo = _split3(x)
    return _dot(m, hi) + _dot(m, mid) + _dot(m, lo)


def _cumsum_lanes(x):
    n = x.shape[1]
    j = lax.broadcasted_iota(jnp.int32, (n, n), 0)
    s = lax.broadcasted_iota(jnp.int32, (n, n), 1)
    m = jnp.where(j <= s, 1.0, 0.0).astype(BF16)
    hi, mid, lo = _split3(x)
    return _dot(hi, m) + _dot(mid, m) + _dot(lo, m)


def _softplus(x):
    return jnp.maximum(x, 0.0) + jnp.log1p(jnp.exp(-jnp.abs(x)))


def _pad_front(x, pad):
    if pad == 0:
        return x
    return jnp.concatenate([jnp.zeros((pad, x.shape[1]), x.dtype), x], axis=0)


def _expand_lanes(cols, width):
    n, k = cols.shape
    lane = lax.broadcasted_iota(jnp.int32, (n, k * width), 1)
    out = jnp.broadcast_to(cols[:, k - 1:k], (n, k * width))
    for r in range(k - 2, -1, -1):
        out = jnp.where(lane < (r + 1) * width, cols[:, r:r + 1], out)
    return out


def _expand_rows(rows, height):
    k, n = rows.shape
    sub = lax.broadcasted_iota(jnp.int32, (k * height, n), 0)
    out = jnp.broadcast_to(rows[k - 1:k, :], (k * height, n))
    for r in range(k - 2, -1, -1):
        out = jnp.where(sub < (r + 1) * height, rows[r:r + 1, :], out)
    return out


def _mm_body(*refs, has_norm, swiglu, has_res, has_extra):
    it = iter(refs)
    x_ref = next(it)
    g_ref = next(it) if has_norm else None
    w_ref = next(it)
    w2_ref = next(it) if swiglu else None
    r_ref = next(it) if has_res else None
    we_ref, be_ref = (next(it), next(it)) if has_extra else (None, None)
    o_ref = next(it)
    x = x_ref[...]
    if has_norm:
        x32 = x.astype(F32)
        x = x32 * lax.rsqrt(jnp.mean(x32 * x32, axis=-1, keepdims=True) + NORM_EPS) * g_ref[...]
    xb = x.astype(BF16)
    acc = _dot(xb, w_ref[...])
    if swiglu:
        acc = acc * jax.nn.sigmoid(acc) * _dot(xb, w2_ref[...])
    if has_res:
        acc = r_ref[...] + acc
    o_ref[...] = acc.reshape(o_ref.shape).astype(o_ref.dtype)
    if has_extra:
        next(it)[...] = _dot(xb, we_ref[...]) + be_ref[...]


def _mm_tiles(t, k, ncols, x_bytes, out_bytes, n_w, has_res, n_passes):
    best = None
    for tm in sorted({min(t, m) for m in (256, 512, 1024, 2048)}):
        if t % tm:
            continue
        for d in range(1, ncols // LANES + 1):
            tn = ncols // d
            if ncols % d or tn % LANES:
                continue
            vmem = 2 * (tm * k * x_bytes + k * tn * 2 * n_w + tm * tn * (out_bytes + 4 * has_res)
                        + tm * LANES * 4 + k * LANES * 2)
            if vmem > MM_VMEM_BUDGET_BYTES:
                continue
            col_tiles = d * n_passes
            steps = col_tiles * (t // tm)
            hbm = col_tiles * t * k * x_bytes + t * ncols * n_passes * (out_bytes + 4 * has_res)
            cost = (max(2.0 * t * k * ncols * n_passes * n_w / MXU_FLOPS, hbm / HBM_BYTES_PER_S)
                    + steps * STEP_OVERHEAD_S)
            if best is None or cost < best[0]:
                best = (cost, tm, tn)
    assert best is not None
    return best[1], best[2]


def _matmul(x, w, *, g=None, w2=None, res=None, extra=None, splits=1, out_dtype=F32):
    t, k = x.shape
    n = w.shape[1]
    ncols = n // splits
    tm, tn = _mm_tiles(t, k, ncols, x.dtype.itemsize, jnp.dtype(out_dtype).itemsize,
                       2 if w2 is not None else 1, res is not None, splits)
    cps = ncols // tn
    grid = (n // tn, t // tm)
    ops = [x]
    specs = [pl.BlockSpec((tm, k), lambda j, i: (i, 0))]
    if g is not None:
        ops.append(g.reshape(1, k))
        specs.append(pl.BlockSpec((1, k), lambda j, i: (0, 0)))
    ops.append(w)
    specs.append(pl.BlockSpec((k, tn), lambda j, i: (0, j)))
    if w2 is not None:
        ops.append(w2)
        specs.append(pl.BlockSpec((k, tn), lambda j, i: (0, j)))
    if res is not None:
        ops.append(res)
        specs.append(pl.BlockSpec((tm, tn), lambda j, i: (i, j)))
    if extra is not None:
        ops += [extra[0], extra[1].reshape(1, LANES)]
        specs += [pl.BlockSpec((k, LANES), lambda j, i: (0, 0)), pl.BlockSpec((1, LANES), lambda j, i: (0, 0))]
    if splits == 1:
        out_specs = [pl.BlockSpec((tm, tn), lambda j, i: (i, j))]
        out_shape = [jax.ShapeDtypeStruct((t, n), out_dtype)]
    else:
        out_specs = [pl.BlockSpec((1, tm, tn), lambda j, i: (j // cps, i, j % cps))]
        out_shape = [jax.ShapeDtypeStruct((splits, t, ncols), out_dtype)]
    if extra is not None:
        out_specs.append(pl.BlockSpec((tm, LANES), lambda j, i: (i, 0)))
        out_shape.append(jax.ShapeDtypeStruct((t, LANES), F32))
    body = functools.partial(_mm_body, has_norm=g is not None, swiglu=w2 is not None,
                             has_res=res is not None, has_extra=extra is not None)
    out = pl.pallas_call(
        body, grid=grid, in_specs=specs, out_specs=out_specs, out_shape=out_shape,
        compiler_params=_params(("arbitrary", "arbitrary")), name="matmul",
    )(*ops)
    return out if extra is not None else out[0]


def _conv_body(x_ref, prev_ref, buf_ref, w_ref, b_ref, o_ref, *, width):
    ts = x_ref.shape[0]
    x = x_ref[...]
    hist = jnp.where(pl.program_id(1) == 0, buf_ref[0], prev_ref[...])
    xx = jnp.concatenate([hist, x], axis=0)
    w = w_ref[...]
    y = b_ref[...]
    for j in range(width):
        lo = SUBLANES - (width - 1) + j
        y = y + xx[lo:lo + ts, :] * w[j:j + 1, :]
    o_ref[...] = (y * jax.nn.sigmoid(y)).astype(o_ref.dtype)


def _causal_conv_silu(x2, col0, chans, buf, w, b, bsz, seq, out_dtype):
    width = w.shape[0]
    assert width - 1 <= SUBLANES
    ts = math.gcd(seq, ROW_TILE)
    nt = seq // ts
    tc = math.gcd(chans, CONV_COLS, col0)
    assert ts % SUBLANES == 0
    cb0 = col0 // tc
    rb = ts // SUBLANES
    if buf is None:
        buf8 = jnp.zeros((bsz, SUBLANES, chans), x2.dtype)
    else:
        buf8 = jnp.pad(buf.astype(x2.dtype), ((0, 0), (SUBLANES - (width - 1), 0), (0, 0)))
    return pl.pallas_call(
        functools.partial(_conv_body, width=width), grid=(bsz, nt, chans // tc),
        in_specs=[
            pl.BlockSpec((ts, tc), lambda bi, i, c: (bi * nt + i, cb0 + c)),
            pl.BlockSpec((SUBLANES, tc), lambda bi, i, c: (jnp.maximum((bi * nt + i) * rb - 1, 0), cb0 + c)),
            pl.BlockSpec((1, SUBLANES, tc), lambda bi, i, c: (bi, 0, c)),
            pl.BlockSpec((width, tc), lambda bi, i, c: (0, c)),
            pl.BlockSpec((1, tc), lambda bi, i, c: (0, c)),
        ],
        out_specs=pl.BlockSpec((ts, tc), lambda bi, i, c: (bi * nt + i, c)),
        out_shape=jax.ShapeDtypeStruct((bsz * seq, chans), out_dtype),
        compiler_params=_params(("parallel", "parallel", "parallel")), name="conv_silu",
    )(x2, x2, buf8, w, b.reshape(1, chans))


def _conv_state(x2, col0, chans, buf, bsz, seq, width):
    tail = x2.reshape(bsz, seq, -1)[:, max(seq - (width - 1), 0):, col0:col0 + chans]
    if seq >= width - 1:
        return tail
    if buf is None:
        buf = jnp.zeros((bsz, width - 1, chans), x2.dtype)
    return jnp.concatenate([buf.astype(x2.dtype), tail], axis=1)[:, -(width - 1):]


def _mlstm_head(q, k, v, o, gc, gr, ng, c, n, m_prev, *, lp, pad, dh):
    q = q * (dh ** -0.5)
    ii_c, ii_r = gc[:, 0:1], gr[0:1, :]
    ff_c, ff_r = -_softplus(-gc[:, 1:2]), -_softplus(-gr[1:2, :])
    if pad:
        ok_c = lax.broadcasted_iota(jnp.int32, (lp, 1), 0) >= pad
        ok_r = lax.broadcasted_iota(jnp.int32, (1, lp), 1) >= pad
        ii_c, ii_r = jnp.where(ok_c, ii_c, NEG), jnp.where(ok_r, ii_r, NEG)
        ff_c, ff_r = jnp.where(ok_c, ff_c, 0.0), jnp.where(ok_r, ff_r, 0.0)
    b_c = _cumsum_rows(ff_c)
    b_r = _cumsum_lanes(ff_r)
    t_i = lax.broadcasted_iota(jnp.int32, (lp, lp), 0)
    s_i = lax.broadcasted_iota(jnp.int32, (lp, lp), 1)
    log_d = jnp.where(s_i <= t_i, b_c + (ii_r - b_r), NEG)
    log_inter = b_c + m_prev
    m_t = jnp.maximum(log_inter, jnp.max(log_d, axis=1, keepdims=True))
    d_mat = jnp.exp(log_d - m_t)
    w_inter = jnp.exp(log_inter - m_t)
    qb, kb, vb = q.astype(BF16), k.astype(BF16), v.astype(BF16)
    s = _dot_nt(qb, kb) * d_mat
    num = _dot(s.astype(BF16), vb) + w_inter * _dot(qb, c.astype(BF16))
    den = jnp.sum(s, axis=1, keepdims=True) + w_inter * jnp.sum(q * n, axis=1, keepdims=True)
    h = num / jnp.maximum(jnp.abs(den), jnp.exp(-m_t))
    h = h * lax.rsqrt(jnp.mean(h * h, axis=1, keepdims=True) + NORM_EPS) * ng
    hg = jax.nn.sigmoid(o) * h[pad:, :]

    b_last = b_c[lp - 1:lp, :]
    m_last = m_t[lp - 1:lp, :]
    w_end = jnp.exp(b_last - b_c + ii_c - m_last)
    f_end = jnp.exp(b_last + m_prev - m_last)
    kw = k * w_end
    c_new = f_end * c + _dot_tn(kw.astype(BF16), vb)
    n_new = f_end * n + jnp.sum(kw, axis=0, keepdims=True)
    return hg, c_new, n_new, m_last


def _mlstm_body(*refs, lr, lp, has_state, has_prev, dh, hps):
    it = iter(refs)
    q_ref, k_ref, v_ref, o_ref, gc_ref, gr_ref, ng_ref = (next(it) for _ in range(7))
    c0_ref, n0_ref, m0_ref = (next(it), next(it), next(it)) if has_state else (None, None, None)
    if has_prev:
        next(it)
    hg_ref, c1_ref, n1_ref, m1_ref, c_s, n_s, m_s = it
    ci = pl.program_id(2)
    pad = lp - lr

    @pl.when(ci == 0)
    def _():
        if has_state:
            c_s[...] = c0_ref[0, 0]
            n_s[...] = n0_ref[0]
            m_s[...] = m0_ref[0]
        else:
            c_s[...] = jnp.zeros_like(c_s)
            n_s[...] = jnp.zeros_like(n_s)
            m_s[...] = jnp.zeros_like(m_s)

    args = []
    for u in range(hps):
        cs = slice(u * dh, (u + 1) * dh)
        args.append((_pad_front(q_ref[:, cs].astype(F32), pad), _pad_front(k_ref[:, cs].astype(F32), pad),
                     _pad_front(v_ref[:, cs].astype(F32), pad), o_ref[:, cs], gc_ref[u, 0], gr_ref[u, 0],
                     ng_ref[:, cs], c_s[u], n_s[u], m_s[u]))
    outs = [_mlstm_head(*a, lp=lp, pad=pad, dh=dh) for a in args]
    for u, (hg, c_new, n_new, m_new) in enumerate(outs):
        hg_ref[:, u * dh:(u + 1) * dh] = hg.astype(hg_ref.dtype)
        c_s[u] = c_new
        n_s[u] = n_new
        m_s[u] = m_new

    @pl.when(ci == pl.num_programs(2) - 1)
    def _():
        c1_ref[0, 0] = c_s[...]
        n1_ref[0] = n_s[...]
        m1_ref[0] = m_s[...]


def _mlstm_core(qk, vo, gates, norm_g, bsz, seq, heads, layer, n_layers, state, c_prev, out_dtype):
    inner = qk.shape[1] // 2
    dh = inner // heads
    hps = math.gcd(heads, ML_HEADS_PER_STEP)
    nhg = heads // hps
    lr = math.gcd(seq, ML_CHUNK)
    lp = lr if lr % SEQ_PAD == 0 else SEQ_PAD
    nc = seq // lr
    pad = lp - lr
    gi = gates[:, :heads].reshape(bsz, seq, heads)
    gf = gates[:, heads:2 * heads].reshape(bsz, seq, heads)
    g = jnp.stack([gi, gf], axis=-1)
    g = g.reshape(bsz, nc, lr, heads, 2)
    g = jnp.pad(g, ((0, 0), (0, 0), (pad, 0), (0, 0), (0, 0)))
    gc = jnp.transpose(g, (3, 0, 1, 2, 4)).reshape(heads, bsz, nc * lp, 2)
    gr = jnp.transpose(g, (3, 0, 4, 1, 2)).reshape(heads, bsz, 2, nc * lp)
    has_state = state is not None
    ops = [qk, qk, vo, vo, gc, gr, norm_g.reshape(1, inner)]
    specs = [
        pl.BlockSpec((lr, hps * dh), lambda b, h, c: (b * nc + c, h)),
        pl.BlockSpec((lr, hps * dh), lambda b, h, c: (b * nc + c, nhg + h)),
        pl.BlockSpec((lr, hps * dh), lambda b, h, c: (b * nc + c, h)),
        pl.BlockSpec((lr, hps * dh), lambda b, h, c: (b * nc + c, nhg + h)),
        pl.BlockSpec((hps, 1, lp, 2), lambda b, h, c: (h, b, c, 0)),
        pl.BlockSpec((hps, 1, 2, lp), lambda b, h, c: (h, b, 0, c)),
        pl.BlockSpec((1, hps * dh), lambda b, h, c: (0, h)),
    ]
    if has_state:
        c0, n0, m0 = state
        ops += [c0, n0.reshape(bsz, heads, 1, dh), m0.reshape(bsz, heads, 1, 1)]
        specs += [
            pl.BlockSpec((1, 1, hps, dh, dh), lambda b, h, c: (layer, b, h, 0, 0)),
            pl.BlockSpec((1, hps, 1, dh), lambda b, h, c: (b, h, 0, 0)),
            pl.BlockSpec((1, hps, 1, 1), lambda b, h, c: (b, h, 0, 0)),
        ]
    aliases = {}
    if c_prev is not None:
        aliases[len(ops)] = 1
        ops.append(c_prev)
        specs.append(pl.BlockSpec(memory_space=pl.ANY))
    body = functools.partial(_mlstm_body, lr=lr, lp=lp, has_state=has_state, has_prev=c_prev is not None,
                             dh=dh, hps=hps)
    hg, c1, n1, m1 = pl.pallas_call(
        body, grid=(bsz, nhg, nc), in_specs=specs, input_output_aliases=aliases,
        out_specs=[
            pl.BlockSpec((lr, hps * dh), lambda b, h, c: (b * nc + c, h)),
            pl.BlockSpec((1, 1, hps, dh, dh), lambda b, h, c: (layer, b, h, 0, 0)),
            pl.BlockSpec((1, hps, 1, dh), lambda b, h, c: (b, h, 0, 0)),
            pl.BlockSpec((1, hps, 1, 1), lambda b, h, c: (b, h, 0, 0)),
        ],
        out_shape=[
            jax.ShapeDtypeStruct((bsz * seq, inner), out_dtype),
            jax.ShapeDtypeStruct((n_layers, bsz, heads, dh, dh), F32),
            jax.ShapeDtypeStruct((bsz, heads, 1, dh), F32),
            jax.ShapeDtypeStruct((bsz, heads, 1, 1), F32),
        ],
        scratch_shapes=[pltpu.VMEM((hps, dh, dh), F32), pltpu.VMEM((hps, 1, dh), F32),
                        pltpu.VMEM((hps, 1, 1), F32)],
        compiler_params=_params(("parallel", "parallel", "arbitrary")), name="mlstm_core",
    )(*ops)
    return hg, c1, n1.reshape(bsz, heads, dh), m1.reshape(bsz, heads)


def _ssd_group(x, bm, cm, z, dtc, dtr, pr, pc, ng, h, *, lp, pad, hpg, hd):
    dd_c = _softplus(dtc + pr[0:1, :])
    dd_r = _softplus(dtr + pc[:, 0:1])
    if pad:
        dd_c = jnp.where(lax.broadcasted_iota(jnp.int32, (lp, hpg), 0) >= pad, dd_c, 0.0)
        dd_r = jnp.where(lax.broadcasted_iota(jnp.int32, (hpg, lp), 1) >= pad, dd_r, 0.0)
    cum_c = _cumsum_rows(dd_c * pr[1:2, :])
    cum_r = _cumsum_lanes(dd_r * pc[:, 1:2])
    cb = _dot_nt(cm, bm)
    t_i = lax.broadcasted_iota(jnp.int32, (lp, lp), 0)
    s_i = lax.broadcasted_iota(jnp.int32, (lp, lp), 1)
    tri = s_i <= t_i
    head_of_lane = lax.broadcasted_iota(jnp.int32, (1, hpg * hd), 1) // hd
    y = jnp.zeros((lp, hpg * hd), F32)
    for r in range(hpg):
        seg = jnp.where(tri, cum_c[:, r:r + 1] - cum_r[r:r + 1, :], NEG)
        w = jnp.exp(seg) * cb * dd_r[r:r + 1, :]
        xr = jnp.where(head_of_lane == r, x, 0.0).astype(BF16)
        y = y + _dot(w.astype(BF16), xr)
    y = y + _expand_lanes(jnp.exp(cum_c), hd) * _dot_nt(cm, h.astype(BF16))
    y = y + _expand_lanes(pr[2:3, :], hd) * x
    yv = y[pad:, :] * (z * jax.nn.sigmoid(z))
    yv = yv * lax.rsqrt(jnp.mean(yv * yv, axis=1, keepdims=True) + NORM_EPS) * ng
    w_end = jnp.exp(cum_c[lp - 1:lp, :] - cum_c) * dd_c
    xw = x * _expand_lanes(w_end, hd)
    decay = _expand_rows(jnp.exp(cum_r[:, lp - 1:lp]), hd)
    return yv, decay * h + _dot_tn(xw.astype(BF16), bm)


def _ssd_body(*refs, lr, lp, has_state, hpg, hd, gps):
    if has_state:
        (x_ref, bm_ref, cm_ref, z_ref, dtc_ref, dtr_ref, pr_ref, pc_ref, ng_ref, h0_ref,
         y_ref, h1_ref, h_s) = refs
    else:
        (x_ref, bm_ref, cm_ref, z_ref, dtc_ref, dtr_ref, pr_ref, pc_ref, ng_ref,
         y_ref, h1_ref, h_s) = refs
    ci = pl.program_id(2)
    pad = lp - lr
    gw = hpg * hd
    ns = bm_ref.shape[1] // gps

    @pl.when(ci == 0)
    def _():
        if has_state:
            h_s[...] = h0_ref[0]
        else:
            h_s[...] = jnp.zeros_like(h_s)

    args = []
    for gi in range(gps):
        xs = slice(gi * gw, (gi + 1) * gw)
        bs = slice(gi * ns, (gi + 1) * ns)
        args.append((_pad_front(x_ref[:, xs], pad), _pad_front(bm_ref[:, bs], pad).astype(BF16),
                     _pad_front(cm_ref[:, bs], pad).astype(BF16), z_ref[:, xs], dtc_ref[gi, 0], dtr_ref[gi, 0],
                     pr_ref[gi], pc_ref[gi], ng_ref[:, xs], h_s[gi]))
    outs = [_ssd_group(*a, lp=lp, pad=pad, hpg=hpg, hd=hd) for a in args]
    for gi, (yv, h_new) in enumerate(outs):
        y_ref[:, gi * gw:(gi + 1) * gw] = yv.astype(y_ref.dtype)
        h_s[gi] = h_new

    @pl.when(ci == pl.num_programs(2) - 1)
    def _():
        h1_ref[0] = h_s[...]


def _ssd_core(zx, xbc, dt_raw, dt_bias, a_neg, d_skip, norm_g, bsz, seq, groups, nstate, hd, h0, out_dtype):
    inner = norm_g.shape[0]
    heads = inner // hd
    hpg = heads // groups
    gw = hpg * hd
    gps = math.gcd(groups, SSD_GROUPS_PER_STEP)
    ngs = groups // gps
    lr = math.gcd(seq, SSD_CHUNK)
    lp = lr if lr % SEQ_PAD == 0 else SEQ_PAD
    nc = seq // lr
    pad = lp - lr
    d = dt_raw[:, :heads].reshape(bsz, nc, lr, groups, hpg)
    d = jnp.pad(d, ((0, 0), (0, 0), (pad, 0), (0, 0), (0, 0)))
    dtc = jnp.transpose(d, (3, 0, 1, 2, 4)).reshape(groups, bsz, nc * lp, hpg)
    dtr = jnp.transpose(d, (3, 0, 4, 1, 2)).reshape(groups, bsz, hpg, nc * lp)
    par = jnp.stack([dt_bias, a_neg, d_skip]).astype(F32).reshape(3, groups, hpg)
    pr = jnp.transpose(par, (1, 0, 2))
    pc = jnp.transpose(par, (1, 2, 0))
    b_off = inner // (gps * nstate)
    c_off = b_off + ngs
    has_state = h0 is not None
    ops = [xbc, xbc, xbc, zx, dtc, dtr, pr, pc, norm_g.reshape(1, inner)]
    specs = [
        pl.BlockSpec((lr, gps * gw), lambda b, g, c: (b * nc + c, g)),
        pl.BlockSpec((lr, gps * nstate), lambda b, g, c: (b * nc + c, b_off + g)),
        pl.BlockSpec((lr, gps * nstate), lambda b, g, c: (b * nc + c, c_off + g)),
        pl.BlockSpec((lr, gps * gw), lambda b, g, c: (b * nc + c, g)),
        pl.BlockSpec((gps, 1, lp, hpg), lambda b, g, c: (g, b, c, 0)),
        pl.BlockSpec((gps, 1, hpg, lp), lambda b, g, c: (g, b, 0, c)),
        pl.BlockSpec((gps, 3, hpg), lambda b, g, c: (g, 0, 0)),
        pl.BlockSpec((gps, hpg, 3), lambda b, g, c: (g, 0, 0)),
        pl.BlockSpec((1, gps * gw), lambda b, g, c: (0, g)),
    ]
    if has_state:
        ops.append(h0.reshape(bsz, groups, gw, nstate))
        specs.append(pl.BlockSpec((1, gps, gw, nstate), lambda b, g, c: (b, g, 0, 0)))
    body = functools.partial(_ssd_body, lr=lr, lp=lp, has_state=has_state, hpg=hpg, hd=hd, gps=gps)
    y, h1 = pl.pallas_call(
        body, grid=(bsz, ngs, nc), in_specs=specs,
        out_specs=[
            pl.BlockSpec((lr, gps * gw), lambda b, g, c: (b * nc + c, g)),
            pl.BlockSpec((1, gps, gw, nstate), lambda b, g, c: (b, g, 0, 0)),
        ],
        out_shape=[
            jax.ShapeDtypeStruct((bsz * seq, inner), out_dtype),
            jax.ShapeDtypeStruct((bsz, groups, gw, nstate), F32),
        ],
        scratch_shapes=[pltpu.VMEM((gps, gw, nstate), F32)],
        compiler_params=_params(("parallel", "parallel", "arbitrary")), name="ssd_core",
    )(*ops)
    return y, h1.reshape(bsz, heads, hd, nstate)


def _sb_logs2(w):
    t = jnp.log2(1.0 + jnp.exp2(-jnp.abs(w)))
    lb = jnp.minimum(w, 0.0) - t
    return lb, lb - w


def _sbp_body(bias_ref, q_ref, k_ref, v_ref, o_ref, *, blk, hd, hps):
    hg = pl.program_id(1)
    qi = pl.program_id(2)
    scale = hd ** -0.5
    bias = [bias_ref[hg * hps + u] * LOG2E for u in range(hps)]
    qb = [q_ref[0, :, u * hd:(u + 1) * hd].astype(BF16) for u in range(hps)]
    j_i = lax.broadcasted_iota(jnp.int32, (2 * blk, blk), 0)
    s_i = lax.broadcasted_iota(jnp.int32, (2 * blk, blk), 1)
    tri2 = jnp.where((j_i % blk) > s_i, 1.0, 0.0).astype(BF16)
    t_i = lax.broadcasted_iota(jnp.int32, (blk, blk), 0)
    strict = lax.broadcasted_iota(jnp.int32, (blk, blk), 1) < t_i

    def rows(kb):
        return pl.ds(pl.multiple_of(kb * blk, blk), blk)

    def scores(kb, u):
        return _dot_nt(qb[u], k_ref[0, rows(kb), u * hd:(u + 1) * hd].astype(BF16))

    def weights(s, run, u, mask):
        lb, lr = _sb_logs2(s * (scale * LOG2E) + bias[u])
        if mask is not None:
            lr = jnp.where(mask, lr, 0.0)
        hi, lo = _split2(lr)
        local = _dot(jnp.concatenate([hi, lo], axis=1), tri2)
        a = jnp.exp2(lb + (run + local))
        if mask is not None:
            a = jnp.where(mask, a, 0.0)
        return a.astype(BF16), run + (local[:, 0:1] + lr[:, 0:1])

    def weighted_values(a, kb, u):
        return _dot(a, v_ref[0, rows(kb), u * hd:(u + 1) * hd].astype(BF16))

    def step(j, carry):
        kb = qi - 1 - j
        out = []
        for u in range(hps):
            run, acc, s, a_prev = carry[u]
            s_next = scores(jnp.maximum(kb - 1, 0), u)
            acc = acc + weighted_values(a_prev, kb + 1, u)
            a, run = weights(s, run, u, None)
            out.append((run, acc, s_next, a))
        return tuple(out)

    carry = []
    for u in range(hps):
        a, run = weights(scores(qi, u), jnp.zeros((blk, 1), F32), u, strict)
        carry.append((run, jnp.zeros((blk, hd), F32), scores(jnp.maximum(qi - 1, 0), u), a))
    carry = lax.fori_loop(0, qi, step, tuple(carry))
    for u in range(hps):
        acc = carry[u][1] + weighted_values(carry[u][3], 0, u)
        o_ref[:, u * hd:(u + 1) * hd] = acc.astype(o_ref.dtype)


def _sb_prompt(qkv, bias, bsz, seq, heads, hd, out_dtype):
    blk = math.gcd(seq, SB_BLOCK)
    nq = seq // blk
    hps = math.gcd(heads, SB_HEADS_PER_STEP)
    nhg = heads // hps
    body = functools.partial(_sbp_body, blk=blk, hd=hd, hps=hps)
    return pl.pallas_call(
        body, grid=(bsz, nhg, nq),
        in_specs=[
            pl.BlockSpec(memory_space=pltpu.SMEM),
            pl.BlockSpec((1, blk, hps * hd), lambda b, h, i: (0, b * nq + i, h)),
            pl.BlockSpec((1, seq, hps * hd), lambda b, h, i: (1, b, h)),
            pl.BlockSpec((1, seq, hps * hd), lambda b, h, i: (2, b, h)),
        ],
        out_specs=pl.BlockSpec((blk, hps * hd), lambda b, h, i: (b * nq + i, h)),
        out_shape=jax.ShapeDtypeStruct((bsz * seq, heads * hd), out_dtype),
        compiler_params=_params(("parallel", "parallel", "arbitrary")), name="sb_prompt",
    )(bias.astype(F32), qkv, qkv, qkv)


def _sbs_body(pt_ref, q_ref, kn_ref, vn_ref, *rest, sq, heads, hd, page, ppg):
    del pt_ref
    kp_refs, vp_refs = rest[:ppg], rest[ppg:2 * ppg]
    bias_ref, o_ref, run_s, acc_s, qbd_s = rest[2 * ppg:]
    p = pl.program_id(1)
    width = heads * hd
    cols = qbd_s.shape[0]
    scale = hd ** -0.5

    def own_head():
        row_i = lax.broadcasted_iota(jnp.int32, (cols, width), 0)
        lane_i = lax.broadcasted_iota(jnp.int32, (cols, width), 1)
        return (row_i // sq) == (lane_i // hd)

    def process(kks, vvs, mask):
        n = len(kks)
        j_i = lax.broadcasted_iota(jnp.int32, (page, 2 * page), 1)
        s_i = lax.broadcasted_iota(jnp.int32, (page, 2 * page), 0)
        tri2 = jnp.where((j_i % page) > s_i, 1.0, 0.0).astype(BF16)
        kcat = kks[0] if n == 1 else jnp.concatenate(kks, axis=0)
        z = _dot_nt(kcat, qbd_s[...]) * (scale * LOG2E) + bias_ref[...] * LOG2E
        lb, lr = _sb_logs2(z)
        if mask is not None:
            lr = jnp.where(mask, lr, 0.0)
        run = run_s[...]
        parts = []
        for i in range(n):
            rs = slice(i * page, (i + 1) * page)
            hi, lo = _split2(lr[rs])
            local = _dot(tri2, jnp.concatenate([hi, lo], axis=0))
            a = jnp.exp2(lb[rs] + (run + local))
            if mask is not None:
                a = jnp.where(mask, a, 0.0)
            parts.append(a.astype(BF16))
            run = run + (local[0:1, :] + lr[rs][0:1, :])
        run_s[...] = run
        acat = parts[0] if n == 1 else jnp.concatenate(parts, axis=0)
        vcat = vvs[0] if n == 1 else jnp.concatenate(vvs, axis=0)
        acc_s[...] += _dot_tn(acat, vcat)

    def load_page(ref):
        return jnp.concatenate([ref[0, 0, pl.ds(h, page, stride=heads), :] for h in range(heads)],
                               axis=1).astype(BF16)

    @pl.when(p == 0)
    def _():
        run_s[...] = jnp.zeros_like(run_s)
        acc_s[...] = jnp.zeros_like(acc_s)
        q = q_ref[0]
        tiled = jnp.concatenate([q] * heads + [jnp.zeros((cols - heads * sq, width), F32)], axis=0)
        qbd_s[...] = jnp.where(own_head(), tiled, 0.0).astype(BF16)
        key_i = lax.broadcasted_iota(jnp.int32, (page, cols), 0)
        qry_i = lax.broadcasted_iota(jnp.int32, (page, cols), 1) % sq
        zeros = jnp.zeros((page - sq, width), F32)
        process([jnp.concatenate([kn_ref[0], zeros], axis=0).astype(BF16)],
                [jnp.concatenate([vn_ref[0], zeros], axis=0).astype(BF16)], key_i < qry_i)

    @pl.when(p > 0)
    def _():
        process([load_page(r) for r in kp_refs], [load_page(r) for r in vp_refs], None)

    @pl.when(p == pl.num_programs(1) - 1)
    def _():
        acc = jnp.where(own_head(), acc_s[...], 0.0)
        out = acc[0:sq, :]
        for h in range(1, heads):
            out = out + acc[h * sq:(h + 1) * sq, :]
        o_ref[...] = out.astype(o_ref.dtype)


def _sb_sample(qkv, pool_k, pool_v, layer, page_table, bias, bsz, sq, heads, hd, out_dtype):
    width = heads * hd
    n_pages = page_table.shape[1]
    page = pool_k.shape[2]
    cols = SEQ_PAD
    ppg = math.gcd(n_pages, SB_PAGES_PER_STEP)
    assert heads * sq <= cols and sq <= page
    bias_cols = jnp.pad(jnp.repeat(bias.astype(F32), sq), (0, cols - heads * sq)).reshape(1, cols)
    pool_k = pool_k.reshape(pool_k.shape[0], pool_k.shape[1], page * heads, hd)
    pool_v = pool_v.reshape(pool_v.shape[0], pool_v.shape[1], page * heads, hd)

    def page_spec(i):
        def idx(b, s, pt):
            return (layer, pt[b, n_pages - 1 - (jnp.maximum(s - 1, 0) * ppg + i)], 0, 0)
        return pl.BlockSpec((1, 1, page * heads, hd), idx)

    body = functools.partial(_sbs_body, sq=sq, heads=heads, hd=hd, page=page, ppg=ppg)
    grid_spec = pltpu.PrefetchScalarGridSpec(
        num_scalar_prefetch=1, grid=(bsz, n_pages // ppg + 1),
        in_specs=[
            pl.BlockSpec((1, sq, width), lambda b, s, pt: (0, b, 0)),
            pl.BlockSpec((1, sq, width), lambda b, s, pt: (1, b, 0)),
            pl.BlockSpec((1, sq, width), lambda b, s, pt: (2, b, 0)),
            *[page_spec(i) for i in range(ppg)],
            *[page_spec(i) for i in range(ppg)],
            pl.BlockSpec((1, cols), lambda b, s, pt: (0, 0)),
        ],
        out_specs=pl.BlockSpec((sq, width), lambda b, s, pt: (b, 0)),
        scratch_shapes=[pltpu.VMEM((1, cols), F32), pltpu.VMEM((cols, width), F32),
                        pltpu.VMEM((cols, width), BF16)],
    )
    return pl.pallas_call(
        body, grid_spec=grid_spec,
        out_shape=jax.ShapeDtypeStruct((bsz * sq, width), out_dtype),
        compiler_params=_params(("parallel", "arbitrary")), name="sb_sample",
    )(page_table, qkv, qkv, qkv, *([pool_k] * ppg), *([pool_v] * ppg), bias_cols)


def _xa_body(x_ref, g_ref, wq_ref, mk_ref, mv_ref, wo_ref, o_ref, *, heads, hd, by_head):
    scale = hd ** -0.5

    def mem_head(ref, h):
        return (ref[0, 0, :, h, :] if by_head else ref[0, :, h * hd:(h + 1) * hd]).astype(BF16)

    x = x_ref[...]
    xn = (x * lax.rsqrt(jnp.mean(x * x, axis=-1, keepdims=True) + NORM_EPS) * g_ref[...]).astype(BF16)
    q = _dot(xn, wq_ref[...]).astype(BF16)
    outs = []
    for h in range(heads):
        cs = slice(h * hd, (h + 1) * hd)
        s = _dot_nt(q[:, cs], mem_head(mk_ref, h)) * scale
        e = jnp.exp(s - jnp.max(s, axis=1, keepdims=True))
        pr = e / jnp.sum(e, axis=1, keepdims=True)
        outs.append(_dot(pr.astype(BF16), mem_head(mv_ref, h)).astype(BF16))
    o_ref[...] = x + _dot(jnp.concatenate(outs, axis=1), wo_ref[...])


def _cross_attn_layer(x, g, wq, mk, mv, mk_idx, mv_idx, wo, bsz, seq, mem_len, heads):
    d = x.shape[1]
    tq = math.gcd(seq, XA_QBLOCK)
    nq = seq // tq
    by_head = mk.ndim == 5
    body = functools.partial(_xa_body, heads=heads, hd=d // heads, by_head=by_head)

    def mem_spec(idx):
        if by_head:
            return pl.BlockSpec((1, 1, mem_len, heads, d // heads), lambda b, i: (idx, b, 0, 0, 0))
        return pl.BlockSpec((1, mem_len, d), lambda b, i: (idx, b, 0))

    return pl.pallas_call(
        body, grid=(bsz, nq),
        in_specs=[
            pl.BlockSpec((tq, d), lambda b, i: (b * nq + i, 0)),
            pl.BlockSpec((1, d), lambda b, i: (0, 0)),
            pl.BlockSpec((d, d), lambda b, i: (0, 0)),
            mem_spec(mk_idx),
            mem_spec(mv_idx),
            pl.BlockSpec((d, d), lambda b, i: (0, 0)),
        ],
        out_specs=pl.BlockSpec((tq, d), lambda b, i: (b * nq + i, 0)),
        out_shape=jax.ShapeDtypeStruct((bsz * seq, d), F32),
        compiler_params=_params(("parallel", "parallel")), name="cross_attn",
    )(x, g.reshape(1, d), wq, mk, mv, wo)


def _prep_weights(p):
    ml_inner = p['ml_w_up'].shape[2]
    ssd_main = p['ssd_w_in'].shape[2] - p['ssd_dt_bias'].shape[1]
    ffn_hidden = p['ffn_w_out'].shape[1]

    def per_layer(a, lo=None, hi=None, pad_to=None, dtype=BF16):
        out = []
        for j in range(a.shape[0]):
            m = a[j, :, lo:hi] if a.ndim == 3 else a[j, lo:hi]
            if pad_to is not None:
                m = jnp.pad(m, [(0, 0)] * (m.ndim - 1) + [(0, pad_to - m.shape[-1])])
            out.append(m.astype(dtype))
        return out

    return {
        'ml_w_up': per_layer(p['ml_w_up']),
        'ml_w_qk': per_layer(p['ml_w_qk']),
        'ml_w_vo': per_layer(p['ml_w_vog'], 0, 2 * ml_inner),
        'ml_w_g': per_layer(p['ml_w_vog'], 2 * ml_inner, None, LANES),
        'ml_b_g': per_layer(p['ml_b_gate'], None, None, LANES, F32),
        'ml_w_down': per_layer(p['ml_w_down']),
        'sb_w_qkv': per_layer(p['sb_w_qkv']),
        'sb_w_o': per_layer(p['sb_w_o']),
        'ssd_w_main': per_layer(p['ssd_w_in'], 0, ssd_main),
        'ssd_w_dt': per_layer(p['ssd_w_in'], ssd_main, None, LANES),
        'ssd_w_out': per_layer(p['ssd_w_out']),
        'ssd_a': -jnp.exp(p['ssd_a_log'].astype(F32)),
        'xa_w_q': per_layer(p['xa_w_q']),
        'xa_w_kv': per_layer(p['xa_w_kv']),
        'xa_w_o': per_layer(p['xa_w_o']),
        'ffn_w_gate': per_layer(p['ffn_w_in'], 0, ffn_hidden),
        'ffn_w_up': per_layer(p['ffn_w_in'], ffn_hidden, None),
        'ffn_w_out': per_layer(p['ffn_w_out']),
    }


def _trunk(x3, mem_kv, ml_states, sb_pools, ssd_states, page_table, p, w, dims, act_dtype):
    bsz, seq, d = x3.shape
    depth = p['norm_mix'].shape[0]
    ml_heads, sb_heads, xa_heads, ssd_groups, ssd_state, ssd_hd, mem_len = dims
    x = x3.reshape(bsz * seq, d)
    ml_new, sb_new, ssd_new = [], [], []
    n_ml = n_sb = n_ssd = 0
    n_ml_layers = p['ml_w_up'].shape[0]
    ml_c = None
    for layer in range(depth):
        kind = layer % 3
        if kind == 0:
            j = n_ml
            n_ml += 1
            inner = p['ml_w_up'].shape[2]
            width = p['ml_conv_w'].shape[1]
            xm = _matmul(x, w['ml_w_up'][j], g=p['norm_mix'][layer])
            st = None if ml_states is None else ml_states[j]
            buf = None if st is None else st[3]
            xc = _causal_conv_silu(xm, 0, inner, buf, p['ml_conv_w'][j], p['ml_conv_b'][j], bsz, seq, act_dtype)
            qk = _matmul(xc, w['ml_w_qk'][j])
            vo, gates = _matmul(xm, w['ml_w_vo'][j], extra=(w['ml_w_g'][j], w['ml_b_g'][j]))
            hg, ml_c, n1, m1 = _mlstm_core(qk, vo, gates, p['ml_norm'][j], bsz, seq, ml_heads, j, n_ml_layers,
                                           None if st is None else st[:3], ml_c, act_dtype)
            x = _matmul(hg, w['ml_w_down'][j], res=x)
            ml_new.append((n1, m1, _conv_state(xm, 0, inner, buf, bsz, seq, width)))
        elif kind == 1:
            j = n_sb
            n_sb += 1
            hd = d // sb_heads
            qkv = _matmul(x, w['sb_w_qkv'][j], g=p['norm_mix'][layer], splits=3)
            if sb_pools is None:
                o = _sb_prompt(qkv, p['sb_bias'][j], bsz, seq, sb_heads, hd, act_dtype)
            else:
                o = _sb_sample(qkv, sb_pools[0], sb_pools[1], j, page_table, p['sb_bias'][j],
                               bsz, seq, sb_heads, hd, act_dtype)
            x = _matmul(o, w['sb_w_o'][j], res=x)
            shp = (bsz, seq, sb_heads, hd)
            sb_new.append((qkv[1].reshape(shp), qkv[2].reshape(shp)))
        else:
            j = n_ssd
            n_ssd += 1
            inner = p['ssd_norm'].shape[1]
            conv_ch = p['ssd_conv_w'].shape[2]
            width = p['ssd_conv_w'].shape[1]
            zx, dt_raw = _matmul(x, w['ssd_w_main'][j], g=p['norm_mix'][layer],
                                 extra=(w['ssd_w_dt'][j], jnp.zeros((LANES,), F32)))
            st = None if ssd_states is None else ssd_states[j]
            buf = None if st is None else st[1]
            xbc = _causal_conv_silu(zx, inner, conv_ch, buf, p['ssd_conv_w'][j], p['ssd_conv_b'][j],
                                    bsz, seq, F32)
            y, h1 = _ssd_core(zx, xbc, dt_raw, p['ssd_dt_bias'][j], w['ssd_a'][j],
                              p['ssd_d'][j], p['ssd_norm'][j], bsz, seq, ssd_groups, ssd_state, ssd_hd,
                              None if st is None else st[0], act_dtype)
            x = _matmul(y, w['ssd_w_out'][j], res=x)
            ssd_new.append((h1, _conv_state(zx, inner, conv_ch, buf, bsz, seq, width)))
        mk, mv, mk_idx, mv_idx = mem_kv[layer]
        x = _cross_attn_layer(x, p['norm_xa'][layer], w['xa_w_q'][layer], mk, mv, mk_idx, mv_idx,
                              w['xa_w_o'][layer], bsz, seq, mem_len, xa_heads)
        hid = _matmul(x, w['ffn_w_gate'][layer], g=p['norm_ffn'][layer], w2=w['ffn_w_up'][layer],
                      out_dtype=act_dtype)
        x = _matmul(hid, w['ffn_w_out'][layer], res=x)
    y = _final_norm(x, p['norm_final'])
    return y.reshape(bsz, seq, d), ml_c, ml_new, sb_new, ssd_new


def _norm_body(x_ref, g_ref, o_ref):
    x = x_ref[...]
    o_ref[...] = x * lax.rsqrt(jnp.mean(x * x, axis=-1, keepdims=True) + NORM_EPS) * g_ref[...]


def _final_norm(x, g):
    t, d = x.shape
    tm = min(ROW_TILE, t)
    return pl.pallas_call(
        _norm_body, grid=(t // tm,),
        in_specs=[pl.BlockSpec((tm, d), lambda i: (i, 0)), pl.BlockSpec((1, d), lambda i: (0, 0))],
        out_specs=pl.BlockSpec((tm, d), lambda i: (i, 0)),
        out_shape=jax.ShapeDtypeStruct((t, d), F32),
        compiler_params=_params(("parallel",)), name="final_norm",
    )(x, g.reshape(1, d))


def _stack(items, idx):
    return jnp.stack([it[idx] for it in items])


def kernel(x_prompt, x_sample, cache_mem_k, cache_mem_v, cache_sb_k, cache_sb_v, state_ml_c, state_ml_n, state_ml_m, state_ml_conv, state_ssd_h, state_ssd_conv, page_table, mem_prompt, norm_mix, norm_xa, norm_ffn, norm_mem, norm_final, ml_w_up, ml_conv_w, ml_conv_b, ml_w_qk, ml_w_vog, ml_b_gate, ml_norm, ml_w_down, sb_w_qkv, sb_bias, sb_w_o, ssd_w_in, ssd_conv_w, ssd_conv_b, ssd_dt_bias, ssd_a_log, ssd_d, ssd_norm, ssd_w_out, xa_w_q, xa_w_kv, xa_w_o, ffn_w_in, ffn_w_out):
    p = dict(norm_mix=norm_mix, norm_xa=norm_xa, norm_ffn=norm_ffn, norm_final=norm_final,
             ml_w_up=ml_w_up, ml_conv_w=ml_conv_w, ml_conv_b=ml_conv_b, ml_w_qk=ml_w_qk,
             ml_w_vog=ml_w_vog, ml_b_gate=ml_b_gate, ml_norm=ml_norm, ml_w_down=ml_w_down,
             sb_w_qkv=sb_w_qkv, sb_bias=sb_bias, sb_w_o=sb_w_o,
             ssd_w_in=ssd_w_in, ssd_conv_w=ssd_conv_w, ssd_conv_b=ssd_conv_b, ssd_dt_bias=ssd_dt_bias,
             ssd_a_log=ssd_a_log, ssd_d=ssd_d, ssd_norm=ssd_norm, ssd_w_out=ssd_w_out,
             xa_w_q=xa_w_q, xa_w_kv=xa_w_kv, xa_w_o=xa_w_o, ffn_w_in=ffn_w_in, ffn_w_out=ffn_w_out)
    w = _prep_weights(p)
    depth = norm_mix.shape[0]
    d = x_prompt.shape[2]
    bp, mem_len = mem_prompt.shape[0], mem_prompt.shape[1]
    bd = x_sample.shape[0]
    ml_heads = state_ml_c.shape[2]
    sb_heads = cache_sb_k.shape[3]
    xa_heads = cache_mem_k.shape[3]
    ssd_hd, ssd_state = state_ssd_h.shape[3], state_ssd_h.shape[4]
    ssd_groups = (ssd_conv_w.shape[2] - ssd_norm.shape[1]) // (2 * ssd_state)
    n_ml, n_ssd = state_ml_c.shape[0], state_ssd_h.shape[0]
    dims = (ml_heads, sb_heads, xa_heads, ssd_groups, ssd_state, ssd_hd, mem_len)

    mem2 = mem_prompt.reshape(bp * mem_len, d)
    mem_kv_p = [_matmul(mem2, w['xa_w_kv'][l], g=norm_mem[l], splits=2) for l in range(depth)]
    y_prompt, ml_c_p, ml_p, sb_p, ssd_p = _trunk(x_prompt, [(kv, kv, 0, 1) for kv in mem_kv_p], None, None, None,
                                         None, p, w, dims, BF16)
    shp = (bp, mem_len, xa_heads, d // xa_heads)
    mem_k_p = jnp.stack([kv[0].reshape(shp) for kv in mem_kv_p])
    mem_v_p = jnp.stack([kv[1].reshape(shp) for kv in mem_kv_p])

    mem_kv_s = [(cache_mem_k, cache_mem_v, l, l) for l in range(depth)]
    ml_cache = [(state_ml_c, state_ml_n[j], state_ml_m[j], state_ml_conv[j]) for j in range(n_ml)]
    ssd_cache = [(state_ssd_h[j], state_ssd_conv[j]) for j in range(n_ssd)]
    y_sample, ml_c_s, ml_s, sb_s, ssd_s = _trunk(x_sample, mem_kv_s, ml_cache, (cache_sb_k, cache_sb_v), ssd_cache,
                                         page_table, p, w, dims, F32)

    return (y_prompt, y_sample, mem_k_p, mem_v_p,
            _stack(sb_p, 0), _stack(sb_p, 1),
            ml_c_p, _stack(ml_p, 0), _stack(ml_p, 1), _stack(ml_p, 2),
            _stack(ssd_p, 0), _stack(ssd_p, 1),
            _stack(sb_s, 0), _stack(sb_s, 1),
            ml_c_s, _stack(ml_s, 0), _stack(ml_s, 1), _stack(ml_s, 2),
            _stack(ssd_s, 0), _stack(ssd_s, 1))
```

```python
import functools
import math

import jax
import jax.numpy as jnp
from jax import lax
from jax.experimental import pallas as pl
from jax.experimental.pallas import tpu as pltpu

F32 = jnp.float32
BF16 = jnp.bfloat16
NORM_EPS = 1e-6
NEG = -1e30
LOG2E = 1.4426950408889634
SUBLANES = 8
VMEM_LIMIT_BYTES = 56 * 1024 * 1024
LANES = 128
MM_VMEM_BUDGET_BYTES = 44 * 1024 * 1024
MXU_FLOPS = 1.0e15
HBM_BYTES_PER_S = 3.0e12
STEP_OVERHEAD_S = 0.35e-6
ROW_TILE = 256
ML_HEADS_PER_STEP = 4
SEQ_PAD = 16
SB_COLS = 128
ML_CHUNK = 256
SSD_CHUNK = 128
SSD_GROUPS_PER_STEP = 4
SB_BLOCK = 256
SB_HEADS_PER_STEP = 2
SB_PAGES_PER_STEP = 8
XA_QBLOCK = 512
CONV_COLS = 2048


def _params(sem):
    return pltpu.CompilerParams(dimension_semantics=sem, vmem_limit_bytes=VMEM_LIMIT_BYTES)


def _dot(a, b):
    return jnp.dot(a, b, preferred_element_type=F32)


def _dot_nt(a, b):
    return lax.dot_general(a, b, (((1,), (1,)), ((), ())), preferred_element_type=F32)


def _dot_tn(a, b):
    return lax.dot_general(a, b, (((0,), (0,)), ((), ())), preferred_element_type=F32)


def _split3(x):
    hi = x.astype(BF16)
    r = x - hi.astype(F32)
    mid = r.astype(BF16)
    lo = (r - mid.astype(F32)).astype(BF16)
    return hi, mid, lo


def _split2(x):
    hi = x.astype(BF16)
    lo = (x - hi.astype(F32)).astype(BF16)
    return hi, lo


def _cumsum_rows(x):
    n = x.shape[0]
    t = lax.broadcasted_iota(jnp.int32, (n, n), 0)
    s = lax.broadcasted_iota(jnp.int32, (n, n), 1)
    m = jnp.where(s <= t, 1.0, 0.0).astype(BF16)
    hi, mid, lo = _split3(x)
    return _dot(m, hi) + _dot(m, mid) + _dot(m, lo)


def _cumsum_lanes(x):
    n = x.shape[1]
    j = lax.broadcasted_iota(jnp.int32, (n, n), 0)
    s = lax.broadcasted_iota(jnp.int32, (n, n), 1)
    m = jnp.where(j <= s, 1.0, 0.0).astype(BF16)
    hi, mid, lo = _split3(x)
    return _dot(hi, m) + _dot(mid, m) + _dot(lo, m)


def _softplus(x):
    return jnp.maximum(x, 0.0) + jnp.log1p(jnp.exp(-jnp.abs(x)))


def _pad_front(x, pad):
    if pad == 0:
        return x
    return jnp.concatenate([jnp.zeros((pad, x.shape[1]), x.dtype), x], axis=0)


def _expand_lanes(cols, width):
    n, k = cols.shape
    lane = lax.broadcasted_iota(jnp.int32, (n, k * width), 1)
    out = jnp.broadcast_to(cols[:, k - 1:k], (n, k * width))
    for r in range(k - 2, -1, -1):
        out = jnp.where(lane < (r + 1) * width, cols[:, r:r + 1], out)
    return out


def _expand_rows(rows, height):
    k, n = rows.shape
    sub = lax.broadcasted_iota(jnp.int32, (k * height, n), 0)
    out = jnp.broadcast_to(rows[k - 1:k, :], (k * height, n))
    for r in range(k - 2, -1, -1):
        out = jnp.where(sub < (r + 1) * height, rows[r:r + 1, :], out)
    return out


def _mm_body(*refs, has_norm, swiglu, has_res, has_extra):
    it = iter(refs)
    x_ref = next(it)
    g_ref = next(it) if has_norm else None
    w_ref = next(it)
    w2_ref = next(it) if swiglu else None
    r_ref = next(it) if has_res else None
    we_ref, be_ref = (next(it), next(it)) if has_extra else (None, None)
    o_ref = next(it)
    x = x_ref[...]
    if has_norm:
        x32 = x.astype(F32)
        x = x32 * lax.rsqrt(jnp.mean(x32 * x32, axis=-1, keepdims=True) + NORM_EPS) * g_ref[...]
    xb = x.astype(BF16)
    acc = _dot(xb, w_ref[...])
    if swiglu:
        acc = acc * jax.nn.sigmoid(acc) * _dot(xb, w2_ref[...])
    if has_res:
        acc = r_ref[...] + acc
    o_ref[...] = acc.reshape(o_ref.shape).astype(o_ref.dtype)
    if has_extra:
        next(it)[...] = _dot(xb, we_ref[...]) + be_ref[...]


def _mm_tiles(t, k, ncols, x_bytes, out_bytes, n_w, has_res, n_passes):
    best = None
    for tm in sorted({min(t, m) for m in (256, 512, 1024, 2048)}):
        if t % tm:
            continue
        for d in range(1, ncols // LANES + 1):
            tn = ncols // d
            if ncols % d or tn % LANES:
                continue
            vmem = 2 * (tm * k * x_bytes + k * tn * 2 * n_w + tm * tn * (out_bytes + 4 * has_res)
                        + tm * LANES * 4 + k * LANES * 2)
            if vmem > MM_VMEM_BUDGET_BYTES:
                continue
            col_tiles = d * n_passes
            steps = col_tiles * (t // tm)
            hbm = col_tiles * t * k * x_bytes + t * ncols * n_passes * (out_bytes + 4 * has_res)
            cost = (max(2.0 * t * k * ncols * n_passes * n_w / MXU_FLOPS, hbm / HBM_BYTES_PER_S)
                    + steps * STEP_OVERHEAD_S)
            if best is None or cost < best[0]:
                best = (cost, tm, tn)
    assert best is not None
    return best[1], best[2]


def _matmul(x, w, *, g=None, w2=None, res=None, extra=None, splits=1, out_dtype=F32):
    t, k = x.shape
    n = w.shape[1]
    ncols = n // splits
    tm, tn = _mm_tiles(t, k, ncols, x.dtype.itemsize, jnp.dtype(out_dtype).itemsize,
                       2 if w2 is not None else 1, res is not None, splits)
    cps = ncols // tn
    grid = (n // tn, t // tm)
    ops = [x]
    specs = [pl.BlockSpec((tm, k), lambda j, i: (i, 0))]
    if g is not None:
        ops.append(g.reshape(1, k))
        specs.append(pl.BlockSpec((1, k), lambda j, i: (0, 0)))
    ops.append(w)
    specs.append(pl.BlockSpec((k, tn), lambda j, i: (0, j)))
    if w2 is not None:
        ops.append(w2)
        specs.append(pl.BlockSpec((k, tn), lambda j, i: (0, j)))
    if res is not None:
        ops.append(res)
        specs.append(pl.BlockSpec((tm, tn), lambda j, i: (i, j)))
    if extra is not None:
        ops += [extra[0], extra[1].reshape(1, LANES)]
        specs += [pl.BlockSpec((k, LANES), lambda j, i: (0, 0)), pl.BlockSpec((1, LANES), lambda j, i: (0, 0))]
    if splits == 1:
        out_specs = [pl.BlockSpec((tm, tn), lambda j, i: (i, j))]
        out_shape = [jax.ShapeDtypeStruct((t, n), out_dtype)]
    else:
        out_specs = [pl.BlockSpec((1, tm, tn), lambda j, i: (j // cps, i, j % cps))]
        out_shape = [jax.ShapeDtypeStruct((splits, t, ncols), out_dtype)]
    if extra is not None:
        out_specs.append(pl.BlockSpec((tm, LANES), lambda j, i: (i, 0)))
        out_shape.append(jax.ShapeDtypeStruct((t, LANES), F32))
    body = functools.partial(_mm_body, has_norm=g is not None, swiglu=w2 is not None,
                             has_res=res is not None, has_extra=extra is not None)
    out = pl.pallas_call(
        body, grid=grid, in_specs=specs, out_specs=out_specs, out_shape=out_shape,
        compiler_params=_params(("arbitrary", "arbitrary")), name="matmul",
    )(*ops)
    return out if extra is not None else out[0]


def _conv_body(x_ref, prev_ref, buf_ref, w_ref, b_ref, o_ref, *, width):
    ts = x_ref.shape[0]
    x = x_ref[...]
    hist = jnp.where(pl.program_id(1) == 0, buf_ref[0], prev_ref[...])
    xx = jnp.concatenate([hist, x], axis=0)
    w = w_ref[...]
    y = b_ref[...]
    for j in range(width):
        k = width - 1 - j
        win = x if k == 0 else pltpu.roll(xx, k, 0)[SUBLANES:, :]
        y = y + win * w[j:j + 1, :]
    o_ref[...] = (y * jax.nn.sigmoid(y)).astype(o_ref.dtype)


def _causal_conv_silu(x2, col0, chans, buf, w, b, bsz, seq, out_dtype):
    width = w.shape[0]
    assert width - 1 <= SUBLANES
    ts = math.gcd(seq, ROW_TILE)
    nt = seq // ts
    tc = math.gcd(chans, CONV_COLS, col0)
    assert ts % SUBLANES == 0
    cb0 = col0 // tc
    rb = ts // SUBLANES
    if buf is None:
        buf8 = jnp.zeros((bsz, SUBLANES, chans), x2.dtype)
    else:
        buf8 = jnp.pad(buf.astype(x2.dtype), ((0, 0), (SUBLANES - (width - 1), 0), (0, 0)))
    return pl.pallas_call(
        functools.partial(_conv_body, width=width), grid=(bsz, nt, chans // tc),
        in_specs=[
            pl.BlockSpec((ts, tc), lambda bi, i, c: (bi * nt + i, cb0 + c)),
            pl.BlockSpec((SUBLANES, tc), lambda bi, i, c: (jnp.maximum((bi * nt + i) * rb - 1, 0), cb0 + c)),
            pl.BlockSpec((1, SUBLANES, tc), lambda bi, i, c: (bi, 0, c)),
            pl.BlockSpec((width, tc), lambda bi, i, c: (0, c)),
            pl.BlockSpec((1, tc), lambda bi, i, c: (0, c)),
        ],
        out_specs=pl.BlockSpec((ts, tc), lambda bi, i, c: (bi * nt + i, c)),
        out_shape=jax.ShapeDtypeStruct((bsz * seq, chans), out_dtype),
        compiler_params=_params(("parallel", "parallel", "parallel")), name="conv_silu",
    )(x2, x2, buf8, w, b.reshape(1, chans))


def _conv_state(x2, col0, chans, buf, bsz, seq, width):
    tail = x2.reshape(bsz, seq, -1)[:, max(seq - (width - 1), 0):, col0:col0 + chans]
    if seq >= width - 1:
        return tail
    if buf is None:
        buf = jnp.zeros((bsz, width - 1, chans), x2.dtype)
    return jnp.concatenate([buf.astype(x2.dtype), tail], axis=1)[:, -(width - 1):]


def _mlstm_head(q, k, v, o, gc, gr, ng, c, n, m_prev, *, lp, pad, dh):
    q = q * (dh ** -0.5)
    ii_c, ii_r = gc[:, 0:1], gr[0:1, :]
    ff_c, ff_r = -_softplus(-gc[:, 1:2]), -_softplus(-gr[1:2, :])
    if pad:
        ok_c = lax.broadcasted_iota(jnp.int32, (lp, 1), 0) >= pad
        ok_r = lax.broadcasted_iota(jnp.int32, (1, lp), 1) >= pad
        ii_c, ii_r = jnp.where(ok_c, ii_c, NEG), jnp.where(ok_r, ii_r, NEG)
        ff_c, ff_r = jnp.where(ok_c, ff_c, 0.0), jnp.where(ok_r, ff_r, 0.0)
    b_c = _cumsum_rows(ff_c)
    b_r = _cumsum_lanes(ff_r)
    t_i = lax.broadcasted_iota(jnp.int32, (lp, lp), 0)
    s_i = lax.broadcasted_iota(jnp.int32, (lp, lp), 1)
    log_d = jnp.where(s_i <= t_i, b_c + (ii_r - b_r), NEG)
    log_inter = b_c + m_prev
    m_t = jnp.maximum(log_inter, jnp.max(log_d, axis=1, keepdims=True))
    d_mat = jnp.exp(log_d - m_t)
    w_inter = jnp.exp(log_inter - m_t)
    qb, kb, vb = q.astype(BF16), k.astype(BF16), v.astype(BF16)
    s = _dot_nt(qb, kb) * d_mat
    num = _dot(s.astype(BF16), vb) + w_inter * _dot(qb, c.astype(BF16))
    den = jnp.sum(s, axis=1, keepdims=True) + w_inter * jnp.sum(q * n, axis=1, keepdims=True)
    h = num / jnp.maximum(jnp.abs(den), jnp.exp(-m_t))
    h = h * lax.rsqrt(jnp.mean(h * h, axis=1, keepdims=True) + NORM_EPS) * ng
    hg = jax.nn.sigmoid(o) * h[pad:, :]

    b_last = b_c[lp - 1:lp, :]
    m_last = m_t[lp - 1:lp, :]
    w_end = jnp.exp(b_last - b_c + ii_c - m_last)
    f_end = jnp.exp(b_last + m_prev - m_last)
    kw = k * w_end
    c_new = f_end * c + _dot_tn(kw.astype(BF16), vb)
    n_new = f_end * n + jnp.sum(kw, axis=0, keepdims=True)
    return hg, c_new, n_new, m_last


def _mlstm_body(*refs, lr, lp, has_state, has_prev, dh, hps):
    it = iter(refs)
    q_ref, k_ref, v_ref, o_ref, gc_ref, gr_ref, ng_ref = (next(it) for _ in range(7))
    c0_ref, n0_ref, m0_ref = (next(it), next(it), next(it)) if has_state else (None, None, None)
    if has_prev:
        next(it)
    hg_ref, c1_ref, n1_ref, m1_ref, c_s, n_s, m_s = it
    ci = pl.program_id(2)
    pad = lp - lr

    @pl.when(ci == 0)
    def _():
        if has_state:
            c_s[...] = c0_ref[0, 0]
            n_s[...] = n0_ref[0]
            m_s[...] = m0_ref[0]
        else:
            c_s[...] = jnp.zeros_like(c_s)
            n_s[...] = jnp.zeros_like(n_s)
            m_s[...] = jnp.zeros_like(m_s)

    args = []
    for u in range(hps):
        cs = slice(u * dh, (u + 1) * dh)
        args.append((_pad_front(q_ref[:, cs].astype(F32), pad), _pad_front(k_ref[:, cs].astype(F32), pad),
                     _pad_front(v_ref[:, cs].astype(F32), pad), o_ref[:, cs], gc_ref[u, 0], gr_ref[u, 0],
                     ng_ref[:, cs], c_s[u], n_s[u], m_s[u]))
    outs = [_mlstm_head(*a, lp=lp, pad=pad, dh=dh) for a in args]
    for u, (hg, c_new, n_new, m_new) in enumerate(outs):
        hg_ref[:, u * dh:(u + 1) * dh] = hg.astype(hg_ref.dtype)
        c_s[u] = c_new
        n_s[u] = n_new
        m_s[u] = m_new

    @pl.when(ci == pl.num_programs(2) - 1)
    def _():
        c1_ref[0, 0] = c_s[...]
        n1_ref[0] = n_s[...]
        m1_ref[0] = m_s[...]


def _mlstm_core(qk, vo, gates, norm_g, bsz, seq, heads, layer, n_layers, state, c_prev, out_dtype):
    inner = qk.shape[1] // 2
    dh = inner // heads
    hps = math.gcd(heads, ML_HEADS_PER_STEP)
    nhg = heads // hps
    lr = math.gcd(seq, ML_CHUNK)
    lp = lr if lr % SEQ_PAD == 0 else SEQ_PAD
    nc = seq // lr
    pad = lp - lr
    gi = gates[:, :heads].reshape(bsz, seq, heads)
    gf = gates[:, heads:2 * heads].reshape(bsz, seq, heads)
    g = jnp.stack([gi, gf], axis=-1)
    g = g.reshape(bsz, nc, lr, heads, 2)
    g = jnp.pad(g, ((0, 0), (0, 0), (pad, 0), (0, 0), (0, 0)))
    gc = jnp.transpose(g, (3, 0, 1, 2, 4)).reshape(heads, bsz, nc * lp, 2)
    gr = jnp.transpose(g, (3, 0, 4, 1, 2)).reshape(heads, bsz, 2, nc * lp)
    has_state = state is not None
    ops = [qk, qk, vo, vo, gc, gr, norm_g.reshape(1, inner)]
    specs = [
        pl.BlockSpec((lr, hps * dh), lambda b, h, c: (b * nc + c, h)),
        pl.BlockSpec((lr, hps * dh), lambda b, h, c: (b * nc + c, nhg + h)),
        pl.BlockSpec((lr, hps * dh), lambda b, h, c: (b * nc + c, h)),
        pl.BlockSpec((lr, hps * dh), lambda b, h, c: (b * nc + c, nhg + h)),
        pl.BlockSpec((hps, 1, lp, 2), lambda b, h, c: (h, b, c, 0)),
        pl.BlockSpec((hps, 1, 2, lp), lambda b, h, c: (h, b, 0, c)),
        pl.BlockSpec((1, hps * dh), lambda b, h, c: (0, h)),
    ]
    if has_state:
        c0, n0, m0 = state
        ops += [c0, n0.reshape(bsz, heads, 1, dh), m0.reshape(bsz, heads, 1, 1)]
        specs += [
            pl.BlockSpec((1, 1, hps, dh, dh), lambda b, h, c: (layer, b, h, 0, 0)),
            pl.BlockSpec((1, hps, 1, dh), lambda b, h, c: (b, h, 0, 0)),
            pl.BlockSpec((1, hps, 1, 1), lambda b, h, c: (b, h, 0, 0)),
        ]
    aliases = {}
    if c_prev is not None:
        aliases[len(ops)] = 1
        ops.append(c_prev)
        specs.append(pl.BlockSpec(memory_space=pl.ANY))
    body = functools.partial(_mlstm_body, lr=lr, lp=lp, has_state=has_state, has_prev=c_prev is not None,
                             dh=dh, hps=hps)
    hg, c1, n1, m1 = pl.pallas_call(
        body, grid=(bsz, nhg, nc), in_specs=specs, input_output_aliases=aliases,
        out_specs=[
            pl.BlockSpec((lr, hps * dh), lambda b, h, c: (b * nc + c, h)),
            pl.BlockSpec((1, 1, hps, dh, dh), lambda b, h, c: (layer, b, h, 0, 0)),
            pl.BlockSpec((1, hps, 1, dh), lambda b, h, c: (b, h, 0, 0)),
            pl.BlockSpec((1, hps, 1, 1), lambda b, h, c: (b, h, 0, 0)),
        ],
        out_shape=[
            jax.ShapeDtypeStruct((bsz * seq, inner), out_dtype),
            jax.ShapeDtypeStruct((n_layers, bsz, heads, dh, dh), F32),
            jax.ShapeDtypeStruct((bsz, heads, 1, dh), F32),
            jax.ShapeDtypeStruct((bsz, heads, 1, 1), F32),
        ],
        scratch_shapes=[pltpu.VMEM((hps, dh, dh), F32), pltpu.VMEM((hps, 1, dh), F32),
                        pltpu.VMEM((hps, 1, 1), F32)],
        compiler_params=_params(("parallel", "parallel", "arbitrary")), name="mlstm_core",
    )(*ops)
    return hg, c1, n1.reshape(bsz, heads, dh), m1.reshape(bsz, heads)


def _ssd_group(x, bm, cm, z, dtc, dtr, pr, pc, ng, h, *, lp, pad, hpg, hd):
    dd_c = _softplus(dtc + pr[0:1, :])
    dd_r = _softplus(dtr + pc[:, 0:1])
    if pad:
        dd_c = jnp.where(lax.broadcasted_iota(jnp.int32, (lp, hpg), 0) >= pad, dd_c, 0.0)
        dd_r = jnp.where(lax.broadcasted_iota(jnp.int32, (hpg, lp), 1) >= pad, dd_r, 0.0)
    cum_c = _cumsum_rows(dd_c * pr[1:2, :])
    cum_r = _cumsum_lanes(dd_r * pc[:, 1:2])
    cb = _dot_nt(cm, bm)
    t_i = lax.broadcasted_iota(jnp.int32, (lp, lp), 0)
    s_i = lax.broadcasted_iota(jnp.int32, (lp, lp), 1)
    tri = s_i <= t_i
    head_of_lane = lax.broadcasted_iota(jnp.int32, (1, hpg * hd), 1) // hd
    y = jnp.zeros((lp, hpg * hd), F32)
    for r in range(hpg):
        seg = jnp.where(tri, cum_c[:, r:r + 1] - cum_r[r:r + 1, :], NEG)
        w = jnp.exp(seg) * cb * dd_r[r:r + 1, :]
        xr = jnp.where(head_of_lane == r, x, 0.0).astype(BF16)
        y = y + _dot(w.astype(BF16), xr)
    y = y + _expand_lanes(jnp.exp(cum_c), hd) * _dot_nt(cm, h.astype(BF16))
    y = y + _expand_lanes(pr[2:3, :], hd) * x
    yv = y[pad:, :] * (z * jax.nn.sigmoid(z))
    yv = yv * lax.rsqrt(jnp.mean(yv * yv, axis=1, keepdims=True) + NORM_EPS) * ng
    w_end = jnp.exp(cum_c[lp - 1:lp, :] - cum_c) * dd_c
    xw = x * _expand_lanes(w_end, hd)
    decay = _expand_rows(jnp.exp(cum_r[:, lp - 1:lp]), hd)
    return yv, decay * h + _dot_tn(xw.astype(BF16), bm)


def _ssd_body(*refs, lr, lp, has_state, hpg, hd, gps):
    if has_state:
        (x_ref, bm_ref, cm_ref, z_ref, dtc_ref, dtr_ref, pr_ref, pc_ref, ng_ref, h0_ref,
         y_ref, h1_ref, h_s) = refs
    else:
        (x_ref, bm_ref, cm_ref, z_ref, dtc_ref, dtr_ref, pr_ref, pc_ref, ng_ref,
         y_ref, h1_ref, h_s) = refs
    ci = pl.program_id(2)
    pad = lp - lr
    gw = hpg * hd
    ns = bm_ref.shape[1] // gps

    @pl.when(ci == 0)
    def _():
        if has_state:
            h_s[...] = h0_ref[0]
        else:
            h_s[...] = jnp.zeros_like(h_s)

    args = []
    for gi in range(gps):
        xs = slice(gi * gw, (gi + 1) * gw)
        bs = slice(gi * ns, (gi + 1) * ns)
        args.append((_pad_front(x_ref[:, xs], pad), _pad_front(bm_ref[:, bs], pad).astype(BF16),
                     _pad_front(cm_ref[:, bs], pad).astype(BF16), z_ref[:, xs], dtc_ref[gi, 0], dtr_ref[gi, 0],
                     pr_ref[gi], pc_ref[gi], ng_ref[:, xs], h_s[gi]))
    outs = [_ssd_group(*a, lp=lp, pad=pad, hpg=hpg, hd=hd) for a in args]
    for gi, (yv, h_new) in enumerate(outs):
        y_ref[:, gi * gw:(gi + 1) * gw] = yv.astype(y_ref.dtype)
        h_s[gi] = h_new

    @pl.when(ci == pl.num_programs(2) - 1)
    def _():
        h1_ref[0] = h_s[...]


def _ssd_core(zx, xbc, dt_raw, dt_bias, a_neg, d_skip, norm_g, bsz, seq, groups, nstate, hd, h0, out_dtype):
    inner = norm_g.shape[0]
    heads = inner // hd
    hpg = heads // groups
    gw = hpg * hd
    gps = math.gcd(groups, SSD_GROUPS_PER_STEP)
    ngs = groups // gps
    lr = math.gcd(seq, SSD_CHUNK)
    lp = lr if lr % SEQ_PAD == 0 else SEQ_PAD
    nc = seq // lr
    pad = lp - lr
    d = dt_raw[:, :heads].reshape(bsz, nc, lr, groups, hpg)
    d = jnp.pad(d, ((0, 0), (0, 0), (pad, 0), (0, 0), (0, 0)))
    dtc = jnp.transpose(d, (3, 0, 1, 2, 4)).reshape(groups, bsz, nc * lp, hpg)
    dtr = jnp.transpose(d, (3, 0, 4, 1, 2)).reshape(groups, bsz, hpg, nc * lp)
    par = jnp.stack([dt_bias, a_neg, d_skip]).astype(F32).reshape(3, groups, hpg)
    pr = jnp.transpose(par, (1, 0, 2))
    pc = jnp.transpose(par, (1, 2, 0))
    b_off = inner // (gps * nstate)
    c_off = b_off + ngs
    has_state = h0 is not None
    ops = [xbc, xbc, xbc, zx, dtc, dtr, pr, pc, norm_g.reshape(1, inner)]
    specs = [
        pl.BlockSpec((lr, gps * gw), lambda b, g, c: (b * nc + c, g)),
        pl.BlockSpec((lr, gps * nstate), lambda b, g, c: (b * nc + c, b_off + g)),
        pl.BlockSpec((lr, gps * nstate), lambda b, g, c: (b * nc + c, c_off + g)),
        pl.BlockSpec((lr, gps * gw), lambda b, g, c: (b * nc + c, g)),
        pl.BlockSpec((gps, 1, lp, hpg), lambda b, g, c: (g, b, c, 0)),
        pl.BlockSpec((gps, 1, hpg, lp), lambda b, g, c: (g, b, 0, c)),
        pl.BlockSpec((gps, 3, hpg), lambda b, g, c: (g, 0, 0)),
        pl.BlockSpec((gps, hpg, 3), lambda b, g, c: (g, 0, 0)),
        pl.BlockSpec((1, gps * gw), lambda b, g, c: (0, g)),
    ]
    if has_state:
        ops.append(h0.reshape(bsz, groups, gw, nstate))
        specs.append(pl.BlockSpec((1, gps, gw, nstate), lambda b, g, c: (b, g, 0, 0)))
    body = functools.partial(_ssd_body, lr=lr, lp=lp, has_state=has_state, hpg=hpg, hd=hd, gps=gps)
    y, h1 = pl.pallas_call(
        body, grid=(bsz, ngs, nc), in_specs=specs,
        out_specs=[
            pl.BlockSpec((lr, gps * gw), lambda b, g, c: (b * nc + c, g)),
            pl.BlockSpec((1, gps, gw, nstate), lambda b, g, c: (b, g, 0, 0)),
        ],
        out_shape=[
            jax.ShapeDtypeStruct((bsz * seq, inner), out_dtype),
            jax.ShapeDtypeStruct((bsz, groups, gw, nstate), F32),
        ],
        scratch_shapes=[pltpu.VMEM((gps, gw, nstate), F32)],
        compiler_params=_params(("parallel", "parallel", "arbitrary")), name="ssd_core",
    )(*ops)
    return y, h1.reshape(bsz, heads, hd, nstate)


def _sb_logs2(w):
    t = jnp.log2(1.0 + jnp.exp2(-jnp.abs(w)))
    lb = jnp.minimum(w, 0.0) - t
    return lb, lb - w


def _sbp_body(bias_ref, q_ref, k_ref, v_ref, o_ref, *, blk, hd, hps):
    hg = pl.program_id(1)
    qi = pl.program_id(2)
    scale = hd ** -0.5
    bias = [bias_ref[hg * hps + u] * LOG2E for u in range(hps)]
    qb = [q_ref[0, :, u * hd:(u + 1) * hd].astype(BF16) for u in range(hps)]
    j_i = lax.broadcasted_iota(jnp.int32, (2 * blk, blk), 0)
    s_i = lax.broadcasted_iota(jnp.int32, (2 * blk, blk), 1)
    tri2 = jnp.where((j_i % blk) > s_i, 1.0, 0.0).astype(BF16)
    t_i = lax.broadcasted_iota(jnp.int32, (blk, blk), 0)
    strict = lax.broadcasted_iota(jnp.int32, (blk, blk), 1) < t_i

    def rows(kb):
        return pl.ds(pl.multiple_of(kb * blk, blk), blk)

    def scores(kb, u):
        return _dot_nt(qb[u], k_ref[0, rows(kb), u * hd:(u + 1) * hd].astype(BF16))

    def weights(s, run, u, mask):
        lb, lr = _sb_logs2(s * (scale * LOG2E) + bias[u])
        if mask is not None:
            lr = jnp.where(mask, lr, 0.0)
        hi, lo = _split2(lr)
        local = _dot(jnp.concatenate([hi, lo], axis=1), tri2)
        a = jnp.exp2(lb + (run + local))
        if mask is not None:
            a = jnp.where(mask, a, 0.0)
        return a.astype(BF16), run + (local[:, 0:1] + lr[:, 0:1])

    def weighted_values(a, kb, u):
        return _dot(a, v_ref[0, rows(kb), u * hd:(u + 1) * hd].astype(BF16))

    def step(j, carry):
        kb = qi - 1 - j
        out = []
        for u in range(hps):
            run, acc, s, a_prev = carry[u]
            s_next = scores(jnp.maximum(kb - 1, 0), u)
            acc = acc + weighted_values(a_prev, kb + 1, u)
            a, run = weights(s, run, u, None)
            out.append((run, acc, s_next, a))
        return tuple(out)

    carry = []
    for u in range(hps):
        a, run = weights(scores(qi, u), jnp.zeros((blk, 1), F32), u, strict)
        carry.append((run, jnp.zeros((blk, hd), F32), scores(jnp.maximum(qi - 1, 0), u), a))
    carry = lax.fori_loop(0, qi, step, tuple(carry))
    for u in range(hps):
        acc = carry[u][1] + weighted_values(carry[u][3], 0, u)
        o_ref[:, u * hd:(u + 1) * hd] = acc.astype(o_ref.dtype)


def _sb_prompt(qkv, bias, bsz, seq, heads, hd, out_dtype):
    blk = math.gcd(seq, SB_BLOCK)
    nq = seq // blk
    hps = math.gcd(heads, SB_HEADS_PER_STEP)
    nhg = heads // hps
    body = functools.partial(_sbp_body, blk=blk, hd=hd, hps=hps)
    return pl.pallas_call(
        body, grid=(bsz, nhg, nq),
        in_specs=[
            pl.BlockSpec(memory_space=pltpu.SMEM),
            pl.BlockSpec((1, blk, hps * hd), lambda b, h, i: (0, b * nq + i, h)),
            pl.BlockSpec((1, seq, hps * hd), lambda b, h, i: (1, b, h)),
            pl.BlockSpec((1, seq, hps * hd), lambda b, h, i: (2, b, h)),
        ],
        out_specs=pl.BlockSpec((blk, hps * hd), lambda b, h, i: (b * nq + i, h)),
        out_shape=jax.ShapeDtypeStruct((bsz * seq, heads * hd), out_dtype),
        compiler_params=_params(("parallel", "parallel", "arbitrary")), name="sb_prompt",
    )(bias.astype(F32), qkv, qkv, qkv)


def _sbs_body(pt_ref, q_ref, kn_ref, vn_ref, *rest, sq, heads, hd, page, ppg):
    del pt_ref
    kp_refs, vp_refs = rest[:ppg], rest[ppg:2 * ppg]
    bias_ref, o_ref, run_s, acc_s, qbd_s = rest[2 * ppg:]
    p = pl.program_id(1)
    width = heads * hd
    cols = qbd_s.shape[0]
    scale = hd ** -0.5

    def own_head():
        row_i = lax.broadcasted_iota(jnp.int32, (cols, width), 0)
        lane_i = lax.broadcasted_iota(jnp.int32, (cols, width), 1)
        return (row_i // sq) == (lane_i // hd)

    def process(kks, vvs, mask):
        n = len(kks)
        j_i = lax.broadcasted_iota(jnp.int32, (page, 2 * page), 1)
        s_i = lax.broadcasted_iota(jnp.int32, (page, 2 * page), 0)
        tri2 = jnp.where((j_i % page) > s_i, 1.0, 0.0).astype(BF16)
        kcat = kks[0] if n == 1 else jnp.concatenate(kks, axis=0)
        z = _dot_nt(kcat, qbd_s[...]) * (scale * LOG2E) + bias_ref[...] * LOG2E
        lb, lr = _sb_logs2(z)
        if mask is not None:
            lr = jnp.where(mask, lr, 0.0)
        run = run_s[...]
        parts = []
        for i in range(n):
            rs = slice(i * page, (i + 1) * page)
            hi, lo = _split2(lr[rs])
            local = _dot(tri2, jnp.concatenate([hi, lo], axis=0))
            a = jnp.exp2(lb[rs] + (run + local))
            if mask is not None:
                a = jnp.where(mask, a, 0.0)
            parts.append(a.astype(BF16))
            run = run + (local[0:1, :] + lr[rs][0:1, :])
        run_s[...] = run
        acat = parts[0] if n == 1 else jnp.concatenate(parts, axis=0)
        vcat = vvs[0] if n == 1 else jnp.concatenate(vvs, axis=0)
        acc_s[...] += _dot_tn(acat, vcat)

    def load_page(ref):
        return jnp.concatenate([ref[0, 0, pl.ds(h, page, stride=heads), :] for h in range(heads)],
                               axis=1).astype(BF16)

    @pl.when(p == 0)
    def _():
        run_s[...] = jnp.zeros_like(run_s)
        acc_s[...] = jnp.zeros_like(acc_s)
        q = q_ref[0]
        tiled = jnp.concatenate([q] * heads + [jnp.zeros((cols - heads * sq, width), F32)], axis=0)
        qbd_s[...] = jnp.where(own_head(), tiled, 0.0).astype(BF16)
        key_i = lax.broadcasted_iota(jnp.int32, (page, cols), 0)
        qry_i = lax.broadcasted_iota(jnp.int32, (page, cols), 1) % sq
        zeros = jnp.zeros((page - sq, width), F32)
        process([jnp.concatenate([kn_ref[0], zeros], axis=0).astype(BF16)],
                [jnp.concatenate([vn_ref[0], zeros], axis=0).astype(BF16)], key_i < qry_i)

    @pl.when(p > 0)
    def _():
        process([load_page(r) for r in kp_refs], [load_page(r) for r in vp_refs], None)

    @pl.when(p == pl.num_programs(1) - 1)
    def _():
        acc = jnp.where(own_head(), acc_s[...], 0.0)
        out = acc[0:sq, :]
        for h in range(1, heads):
            out = out + acc[h * sq:(h + 1) * sq, :]
        o_ref[...] = out.astype(o_ref.dtype)


def _sb_sample(qkv, pool_k, pool_v, layer, page_table, bias, bsz, sq, heads, hd, out_dtype):
    width = heads * hd
    n_pages = page_table.shape[1]
    page = pool_k.shape[2]
    cols = SB_COLS
    ppg = math.gcd(n_pages, SB_PAGES_PER_STEP)
    assert heads * sq <= cols and sq <= page
    bias_cols = jnp.pad(jnp.repeat(bias.astype(F32), sq), (0, cols - heads * sq)).reshape(1, cols)
    pool_k = pool_k.reshape(pool_k.shape[0], pool_k.shape[1], page * heads, hd)
    pool_v = pool_v.reshape(pool_v.shape[0], pool_v.shape[1], page * heads, hd)

    def page_spec(i):
        def idx(b, s, pt):
            return (layer, pt[b, n_pages - 1 - (jnp.maximum(s - 1, 0) * ppg + i)], 0, 0)
        return pl.BlockSpec((1, 1, page * heads, hd), idx)

    body = functools.partial(_sbs_body, sq=sq, heads=heads, hd=hd, page=page, ppg=ppg)
    grid_spec = pltpu.PrefetchScalarGridSpec(
        num_scalar_prefetch=1, grid=(bsz, n_pages // ppg + 1),
        in_specs=[
            pl.BlockSpec((1, sq, width), lambda b, s, pt: (0, b, 0)),
            pl.BlockSpec((1, sq, width), lambda b, s, pt: (1, b, 0)),
            pl.BlockSpec((1, sq, width), lambda b, s, pt: (2, b, 0)),
            *[page_spec(i) for i in range(ppg)],
            *[page_spec(i) for i in range(ppg)],
            pl.BlockSpec((1, cols), lambda b, s, pt: (0, 0)),
        ],
        out_specs=pl.BlockSpec((sq, width), lambda b, s, pt: (b, 0)),
        scratch_shapes=[pltpu.VMEM((1, cols), F32), pltpu.VMEM((cols, width), F32),
                        pltpu.VMEM((cols, width), BF16)],
    )
    return pl.pallas_call(
        body, grid_spec=grid_spec,
        out_shape=jax.ShapeDtypeStruct((bsz * sq, width), out_dtype),
        compiler_params=_params(("parallel", "arbitrary")), name="sb_sample",
    )(page_table, qkv, qkv, qkv, *([pool_k] * ppg), *([pool_v] * ppg), bias_cols)


def _xa_body(x_ref, g_ref, wq_ref, mk_ref, mv_ref, wo_ref, o_ref, *, heads, hd):
    scale = hd ** -0.5

    x = x_ref[...]
    xn = (x * lax.rsqrt(jnp.mean(x * x, axis=-1, keepdims=True) + NORM_EPS) * g_ref[...]).astype(BF16)
    q = _dot(xn, wq_ref[...]).astype(BF16)
    outs = []
    for h in range(heads):
        cs = slice(h * hd, (h + 1) * hd)
        s = _dot_nt(q[:, cs], mk_ref[0, :, cs].astype(BF16)) * scale
        e = jnp.exp(s - jnp.max(s, axis=1, keepdims=True))
        pr = e / jnp.sum(e, axis=1, keepdims=True)
        outs.append(_dot(pr.astype(BF16), mv_ref[0, :, cs].astype(BF16)).astype(BF16))
    o_ref[...] = x + _dot(jnp.concatenate(outs, axis=1), wo_ref[...])


def _cross_attn_layer(x, g, wq, mk, mv, mk_idx, mv_idx, wo, bsz, seq, mem_len, heads):
    d = x.shape[1]
    tq = math.gcd(seq, XA_QBLOCK)
    nq = seq // tq
    body = functools.partial(_xa_body, heads=heads, hd=d // heads)

    return pl.pallas_call(
        body, grid=(bsz, nq),
        in_specs=[
            pl.BlockSpec((tq, d), lambda b, i: (b * nq + i, 0)),
            pl.BlockSpec((1, d), lambda b, i: (0, 0)),
            pl.BlockSpec((d, d), lambda b, i: (0, 0)),
            pl.BlockSpec((1, mem_len, d), lambda b, i: (mk_idx, b, 0)),
            pl.BlockSpec((1, mem_len, d), lambda b, i: (mv_idx, b, 0)),
            pl.BlockSpec((d, d), lambda b, i: (0, 0)),
        ],
        out_specs=pl.BlockSpec((tq, d), lambda b, i: (b * nq + i, 0)),
        out_shape=jax.ShapeDtypeStruct((bsz * seq, d), F32),
        compiler_params=_params(("parallel", "parallel")), name="cross_attn",
    )(x, g.reshape(1, d), wq, mk, mv, wo)


def _prep_weights(p):
    ml_inner = p['ml_w_up'].shape[2]
    ssd_main = p['ssd_w_in'].shape[2] - p['ssd_dt_bias'].shape[1]
    ffn_hidden = p['ffn_w_out'].shape[1]

    def per_layer(a, lo=None, hi=None, pad_to=None, dtype=BF16):
        out = []
        for j in range(a.shape[0]):
            m = a[j, :, lo:hi] if a.ndim == 3 else a[j, lo:hi]
            if pad_to is not None:
                m = jnp.pad(m, [(0, 0)] * (m.ndim - 1) + [(0, pad_to - m.shape[-1])])
            out.append(m.astype(dtype))
        return out

    return {
        'ml_w_up': per_layer(p['ml_w_up']),
        'ml_w_qk': per_layer(p['ml_w_qk']),
        'ml_w_vo': per_layer(p['ml_w_vog'], 0, 2 * ml_inner),
        'ml_w_g': per_layer(p['ml_w_vog'], 2 * ml_inner, None, LANES),
        'ml_b_g': per_layer(p['ml_b_gate'], None, None, LANES, F32),
        'ml_w_down': per_layer(p['ml_w_down']),
        'sb_w_qkv': per_layer(p['sb_w_qkv']),
        'sb_w_o': per_layer(p['sb_w_o']),
        'ssd_w_main': per_layer(p['ssd_w_in'], 0, ssd_main),
        'ssd_w_dt': per_layer(p['ssd_w_in'], ssd_main, None, LANES),
        'ssd_w_out': per_layer(p['ssd_w_out']),
        'ssd_a': -jnp.exp(p['ssd_a_log'].astype(F32)),
        'xa_w_q': per_layer(p['xa_w_q']),
        'xa_w_kv': per_layer(p['xa_w_kv']),
        'xa_w_o': per_layer(p['xa_w_o']),
        'ffn_w_gate': per_layer(p['ffn_w_in'], 0, ffn_hidden),
        'ffn_w_up': per_layer(p['ffn_w_in'], ffn_hidden, None),
        'ffn_w_out': per_layer(p['ffn_w_out']),
    }


def _trunk(x3, mem_kv, ml_states, sb_pools, ssd_states, page_table, p, w, dims, act_dtype):
    bsz, seq, d = x3.shape
    depth = p['norm_mix'].shape[0]
    ml_heads, sb_heads, xa_heads, ssd_groups, ssd_state, ssd_hd, mem_len = dims
    x = x3.reshape(bsz * seq, d)
    ml_new, sb_new, ssd_new = [], [], []
    n_ml = n_sb = n_ssd = 0
    n_ml_layers = p['ml_w_up'].shape[0]
    ml_c = None
    for layer in range(depth):
        kind = layer % 3
        if kind == 0:
            j = n_ml
            n_ml += 1
            inner = p['ml_w_up'].shape[2]
            width = p['ml_conv_w'].shape[1]
            xm = _matmul(x, w['ml_w_up'][j], g=p['norm_mix'][layer])
            st = None if ml_states is None else ml_states[j]
            buf = None if st is None else st[3]
            xc = _causal_conv_silu(xm, 0, inner, buf, p['ml_conv_w'][j], p['ml_conv_b'][j], bsz, seq, act_dtype)
            qk = _matmul(xc, w['ml_w_qk'][j])
            vo, gates = _matmul(xm, w['ml_w_vo'][j], extra=(w['ml_w_g'][j], w['ml_b_g'][j]))
            hg, ml_c, n1, m1 = _mlstm_core(qk, vo, gates, p['ml_norm'][j], bsz, seq, ml_heads, j, n_ml_layers,
                                           None if st is None else st[:3], ml_c, act_dtype)
            x = _matmul(hg, w['ml_w_down'][j], res=x)
            ml_new.append((n1, m1, _conv_state(xm, 0, inner, buf, bsz, seq, width)))
        elif kind == 1:
            j = n_sb
            n_sb += 1
            hd = d // sb_heads
            qkv = _matmul(x, w['sb_w_qkv'][j], g=p['norm_mix'][layer], splits=3)
            if sb_pools is None:
                o = _sb_prompt(qkv, p['sb_bias'][j], bsz, seq, sb_heads, hd, act_dtype)
            else:
                o = _sb_sample(qkv, sb_pools[0], sb_pools[1], j, page_table, p['sb_bias'][j],
                               bsz, seq, sb_heads, hd, act_dtype)
            x = _matmul(o, w['sb_w_o'][j], res=x)
            shp = (bsz, seq, sb_heads, hd)
            sb_new.append((qkv[1].reshape(shp), qkv[2].reshape(shp)))
        else:
            j = n_ssd
            n_ssd += 1
            inner = p['ssd_norm'].shape[1]
            conv_ch = p['ssd_conv_w'].shape[2]
            width = p['ssd_conv_w'].shape[1]
            zx, dt_raw = _matmul(x, w['ssd_w_main'][j], g=p['norm_mix'][layer],
                                 extra=(w['ssd_w_dt'][j], jnp.zeros((LANES,), F32)))
            st = None if ssd_states is None else ssd_states[j]
            buf = None if st is None else st[1]
            xbc = _causal_conv_silu(zx, inner, conv_ch, buf, p['ssd_conv_w'][j], p['ssd_conv_b'][j],
                                    bsz, seq, F32)
            y, h1 = _ssd_core(zx, xbc, dt_raw, p['ssd_dt_bias'][j], w['ssd_a'][j],
                              p['ssd_d'][j], p['ssd_norm'][j], bsz, seq, ssd_groups, ssd_state, ssd_hd,
                              None if st is None else st[0], act_dtype)
            x = _matmul(y, w['ssd_w_out'][j], res=x)
            ssd_new.append((h1, _conv_state(zx, inner, conv_ch, buf, bsz, seq, width)))
        mk, mv, mk_idx, mv_idx = mem_kv[layer]
        x = _cross_attn_layer(x, p['norm_xa'][layer], w['xa_w_q'][layer], mk, mv, mk_idx, mv_idx,
                              w['xa_w_o'][layer], bsz, seq, mem_len, xa_heads)
        hid = _matmul(x, w['ffn_w_gate'][layer], g=p['norm_ffn'][layer], w2=w['ffn_w_up'][layer],
                      out_dtype=act_dtype)
        x = _matmul(hid, w['ffn_w_out'][layer], res=x)
    y = _final_norm(x, p['norm_final'])
    return y.reshape(bsz, seq, d), ml_c, ml_new, sb_new, ssd_new


def _norm_body(x_ref, g_ref, o_ref):
    x = x_ref[...]
    o_ref[...] = x * lax.rsqrt(jnp.mean(x * x, axis=-1, keepdims=True) + NORM_EPS) * g_ref[...]


def _final_norm(x, g):
    t, d = x.shape
    tm = min(ROW_TILE, t)
    return pl.pallas_call(
        _norm_body, grid=(t // tm,),
        in_specs=[pl.BlockSpec((tm, d), lambda i: (i, 0)), pl.BlockSpec((1, d), lambda i: (0, 0))],
        out_specs=pl.BlockSpec((tm, d), lambda i: (i, 0)),
        out_shape=jax.ShapeDtypeStruct((t, d), F32),
        compiler_params=_params(("parallel",)), name="final_norm",
    )(x, g.reshape(1, d))


def _stack(items, idx):
    return jnp.stack([it[idx] for it in items])


def kernel(x_prompt, x_sample, cache_mem_k, cache_mem_v, cache_sb_k, cache_sb_v, state_ml_c, state_ml_n, state_ml_m, state_ml_conv, state_ssd_h, state_ssd_conv, page_table, mem_prompt, norm_mix, norm_xa, norm_ffn, norm_mem, norm_final, ml_w_up, ml_conv_w, ml_conv_b, ml_w_qk, ml_w_vog, ml_b_gate, ml_norm, ml_w_down, sb_w_qkv, sb_bias, sb_w_o, ssd_w_in, ssd_conv_w, ssd_conv_b, ssd_dt_bias, ssd_a_log, ssd_d, ssd_norm, ssd_w_out, xa_w_q, xa_w_kv, xa_w_o, ffn_w_in, ffn_w_out):
    p = dict(norm_mix=norm_mix, norm_xa=norm_xa, norm_ffn=norm_ffn, norm_final=norm_final,
             ml_w_up=ml_w_up, ml_conv_w=ml_conv_w, ml_conv_b=ml_conv_b, ml_w_qk=ml_w_qk,
             ml_w_vog=ml_w_vog, ml_b_gate=ml_b_gate, ml_norm=ml_norm, ml_w_down=ml_w_down,
             sb_w_qkv=sb_w_qkv, sb_bias=sb_bias, sb_w_o=sb_w_o,
             ssd_w_in=ssd_w_in, ssd_conv_w=ssd_conv_w, ssd_conv_b=ssd_conv_b, ssd_dt_bias=ssd_dt_bias,
             ssd_a_log=ssd_a_log, ssd_d=ssd_d, ssd_norm=ssd_norm, ssd_w_out=ssd_w_out,
             xa_w_q=xa_w_q, xa_w_kv=xa_w_kv, xa_w_o=xa_w_o, ffn_w_in=ffn_w_in, ffn_w_out=ffn_w_out)
    w = _prep_weights(p)
    depth = norm_mix.shape[0]
    d = x_prompt.shape[2]
    bp, mem_len = mem_prompt.shape[0], mem_prompt.shape[1]
    bd = x_sample.shape[0]
    ml_heads = state_ml_c.shape[2]
    sb_heads = cache_sb_k.shape[3]
    xa_heads = cache_mem_k.shape[3]
    ssd_hd, ssd_state = state_ssd_h.shape[3], state_ssd_h.shape[4]
    ssd_groups = (ssd_conv_w.shape[2] - ssd_norm.shape[1]) // (2 * ssd_state)
    n_ml, n_ssd = state_ml_c.shape[0], state_ssd_h.shape[0]
    dims = (ml_heads, sb_heads, xa_heads, ssd_groups, ssd_state, ssd_hd, mem_len)

    mem2 = mem_prompt.reshape(bp * mem_len, d)
    mem_kv_p = [_matmul(mem2, w['xa_w_kv'][l], g=norm_mem[l], splits=2) for l in range(depth)]
    y_prompt, ml_c_p, ml_p, sb_p, ssd_p = _trunk(x_prompt, [(kv, kv, 0, 1) for kv in mem_kv_p], None, None, None,
                                         None, p, w, dims, BF16)
    shp = (bp, mem_len, xa_heads, d // xa_heads)
    mem_k_p = jnp.stack([kv[0].reshape(shp) for kv in mem_kv_p])
    mem_v_p = jnp.stack([kv[1].reshape(shp) for kv in mem_kv_p])

    ck = cache_mem_k.reshape(depth, bd * mem_len, d)
    cv = cache_mem_v.reshape(depth, bd * mem_len, d)
    mem_kv_s = [(ck, cv, l, l) for l in range(depth)]
    ml_cache = [(state_ml_c, state_ml_n[j], state_ml_m[j], state_ml_conv[j]) for j in range(n_ml)]
    ssd_cache = [(state_ssd_h[j], state_ssd_conv[j]) for j in range(n_ssd)]
    y_sample, ml_c_s, ml_s, sb_s, ssd_s = _trunk(x_sample, mem_kv_s, ml_cache, (cache_sb_k, cache_sb_v), ssd_cache,
                                         page_table, p, w, dims, F32)

    return (y_prompt, y_sample, mem_k_p, mem_v_p,
            _stack(sb_p, 0), _stack(sb_p, 1),
            ml_c_p, _stack(ml_p, 0), _stack(ml_p, 1), _stack(ml_p, 2),
            _stack(ssd_p, 0), _stack(ssd_p, 1),
            _stack(sb_s, 0), _stack(sb_s, 1),
            ml_c_s, _stack(ml_s, 0), _stack(ml_s, 1), _stack(ml_s, 2),
            _stack(ssd_s, 0), _stack(ssd_s, 1))
```

```python
import functools
import math

import jax
import jax.numpy as jnp
from jax import lax
from jax.experimental import pallas as pl
from jax.experimental.pallas import tpu as pltpu

F32 = jnp.float32
BF16 = jnp.bfloat16
NORM_EPS = 1e-6
NEG = -1e30
LOG2E = 1.4426950408889634
SUBLANES = 8
VMEM_LIMIT_BYTES = 56 * 1024 * 1024
LANES = 128
MM_VMEM_BUDGET_BYTES = 44 * 1024 * 1024
MXU_FLOPS = 1.0e15
HBM_BYTES_PER_S = 3.0e12
STEP_OVERHEAD_S = 0.35e-6
ROW_TILE = 256
ML_HEADS_PER_STEP = 4
SEQ_PAD = 16
SB_COLS = 128
ML_CHUNK = 256
SSD_CHUNK = 128
SSD_GROUPS_PER_STEP = 4
SB_BLOCK = 256
SB_HEADS_PER_STEP = 2
SB_PAGES_PER_STEP = 8
XA_QBLOCK = 512
CONV_COLS = 2048


def _params(sem):
    return pltpu.CompilerParams(dimension_semantics=sem, vmem_limit_bytes=VMEM_LIMIT_BYTES)


def _dot(a, b):
    return jnp.dot(a, b, preferred_element_type=F32)


def _dot_nt(a, b):
    return lax.dot_general(a, b, (((1,), (1,)), ((), ())), preferred_element_type=F32)


def _dot_tn(a, b):
    return lax.dot_general(a, b, (((0,), (0,)), ((), ())), preferred_element_type=F32)


def _split3(x):
    hi = x.astype(BF16)
    r = x - hi.astype(F32)
    mid = r.astype(BF16)
    lo = (r - mid.astype(F32)).astype(BF16)
    return hi, mid, lo


def _split2(x):
    hi = x.astype(BF16)
    lo = (x - hi.astype(F32)).astype(BF16)
    return hi, lo


def _cumsum_rows(x):
    n = x.shape[0]
    t = lax.broadcasted_iota(jnp.int32, (n, n), 0)
    s = lax.broadcasted_iota(jnp.int32, (n, n), 1)
    m = jnp.where(s <= t, 1.0, 0.0).astype(BF16)
    hi, mid, lo = _split3(x)
    return _dot(m, hi) + _dot(m, mid) + _dot(m, lo)


def _cumsum_lanes(x):
    n = x.shape[1]
    j = lax.broadcasted_iota(jnp.int32, (n, n), 0)
    s = lax.broadcasted_iota(jnp.int32, (n, n), 1)
    m = jnp.where(j <= s, 1.0, 0.0).astype(BF16)
    hi, mid, lo = _split3(x)
    return _dot(hi, m) + _dot(mid, m) + _dot(lo, m)


def _softplus(x):
    return jnp.maximum(x, 0.0) + jnp.log1p(jnp.exp(-jnp.abs(x)))


def _pad_front(x, pad):
    if pad == 0:
        return x
    return jnp.concatenate([jnp.zeros((pad, x.shape[1]), x.dtype), x], axis=0)


def _expand_lanes(cols, width):
    n, k = cols.shape
    lane = lax.broadcasted_iota(jnp.int32, (n, k * width), 1)
    out = jnp.broadcast_to(cols[:, k - 1:k], (n, k * width))
    for r in range(k - 2, -1, -1):
        out = jnp.where(lane < (r + 1) * width, cols[:, r:r + 1], out)
    return out


def _expand_rows(rows, height):
    k, n = rows.shape
    sub = lax.broadcasted_iota(jnp.int32, (k * height, n), 0)
    out = jnp.broadcast_to(rows[k - 1:k, :], (k * height, n))
    for r in range(k - 2, -1, -1):
        out = jnp.where(sub < (r + 1) * height, rows[r:r + 1, :], out)
    return out


def _mm_body(*refs, has_norm, swiglu, has_res, has_extra):
    it = iter(refs)
    x_ref = next(it)
    g_ref = next(it) if has_norm else None
    w_ref = next(it)
    w2_ref = next(it) if swiglu else None
    r_ref = next(it) if has_res else None
    we_ref, be_ref = (next(it), next(it)) if has_extra else (None, None)
    o_ref = next(it)
    x = x_ref[...]
    if has_norm:
        x32 = x.astype(F32)
        x = x32 * lax.rsqrt(jnp.mean(x32 * x32, axis=-1, keepdims=True) + NORM_EPS) * g_ref[...]
    xb = x.astype(BF16)
    acc = _dot(xb, w_ref[...])
    if swiglu:
        acc = acc * jax.nn.sigmoid(acc) * _dot(xb, w2_ref[...])
    if has_res:
        acc = r_ref[...] + acc
    o_ref[...] = acc.reshape(o_ref.shape).astype(o_ref.dtype)
    if has_extra:
        next(it)[0] = _dot(xb, we_ref[...]) + be_ref[...]


def _mm_tiles(t, k, ncols, x_bytes, out_bytes, n_w, has_res, n_passes):
    best = None
    for tm in sorted({min(t, m) for m in (256, 512, 1024, 2048)}):
        if t % tm:
            continue
        for d in range(1, ncols // LANES + 1):
            tn = ncols // d
            if ncols % d or tn % LANES:
                continue
            vmem = 2 * (tm * k * x_bytes + k * tn * 2 * n_w + tm * tn * (out_bytes + 4 * has_res)
                        + tm * LANES * 4 + k * LANES * 2)
            if vmem > MM_VMEM_BUDGET_BYTES:
                continue
            col_tiles = d * n_passes
            steps = col_tiles * (t // tm)
            hbm = col_tiles * t * k * x_bytes + t * ncols * n_passes * (out_bytes + 4 * has_res)
            cost = (max(2.0 * t * k * ncols * n_passes * n_w / MXU_FLOPS, hbm / HBM_BYTES_PER_S)
                    + steps * STEP_OVERHEAD_S)
            if best is None or cost < best[0]:
                best = (cost, tm, tn)
    assert best is not None
    return best[1], best[2]


def _matmul(x, w, *, g=None, w2=None, res=None, extra=None, splits=1, out_dtype=F32):
    t, k = x.shape
    n = w.shape[1]
    ncols = n // splits
    tm, tn = _mm_tiles(t, k, ncols, x.dtype.itemsize, jnp.dtype(out_dtype).itemsize,
                       2 if w2 is not None else 1, res is not None, splits)
    cps = ncols // tn
    grid = (n // tn, t // tm)
    ops = [x]
    specs = [pl.BlockSpec((tm, k), lambda j, i: (i, 0))]
    if g is not None:
        ops.append(g.reshape(1, k))
        specs.append(pl.BlockSpec((1, k), lambda j, i: (0, 0)))
    ops.append(w)
    specs.append(pl.BlockSpec((k, tn), lambda j, i: (0, j)))
    if w2 is not None:
        ops.append(w2)
        specs.append(pl.BlockSpec((k, tn), lambda j, i: (0, j)))
    if res is not None:
        ops.append(res)
        specs.append(pl.BlockSpec((tm, tn), lambda j, i: (i, j)))
    if extra is not None:
        ops += [extra[0], extra[1].reshape(1, LANES)]
        specs += [pl.BlockSpec((k, LANES), lambda j, i: (0, 0)), pl.BlockSpec((1, LANES), lambda j, i: (0, 0))]
    if splits == 1:
        out_specs = [pl.BlockSpec((tm, tn), lambda j, i: (i, j))]
        out_shape = [jax.ShapeDtypeStruct((t, n), out_dtype)]
    else:
        out_specs = [pl.BlockSpec((1, tm, tn), lambda j, i: (j // cps, i, j % cps))]
        out_shape = [jax.ShapeDtypeStruct((splits, t, ncols), out_dtype)]
    if extra is not None:
        out_specs.append(pl.BlockSpec((1, tm, LANES), lambda j, i: (j, i, 0)))
        out_shape.append(jax.ShapeDtypeStruct((grid[0], t, LANES), F32))
    body = functools.partial(_mm_body, has_norm=g is not None, swiglu=w2 is not None,
                             has_res=res is not None, has_extra=extra is not None)
    out = pl.pallas_call(
        body, grid=grid, in_specs=specs, out_specs=out_specs, out_shape=out_shape,
        compiler_params=_params(("arbitrary", "arbitrary")), name="matmul",
    )(*ops)
    return (out[0], out[1][0]) if extra is not None else out[0]


def _conv_body(x_ref, prev_ref, buf_ref, w_ref, b_ref, o_ref, *, width):
    ts = x_ref.shape[0]
    x = x_ref[...]
    hist = jnp.where(pl.program_id(1) == 0, buf_ref[0], prev_ref[...])
    xx = jnp.concatenate([hist, x], axis=0)
    w = w_ref[...]
    y = b_ref[...]
    for j in range(width):
        k = width - 1 - j
        win = x if k == 0 else pltpu.roll(xx, k, 0)[SUBLANES:, :]
        y = y + win * w[j:j + 1, :]
    o_ref[...] = (y * jax.nn.sigmoid(y)).astype(o_ref.dtype)


def _causal_conv_silu(x2, col0, chans, buf, w, b, bsz, seq, out_dtype):
    width = w.shape[0]
    assert width - 1 <= SUBLANES
    ts = math.gcd(seq, ROW_TILE)
    nt = seq // ts
    tc = math.gcd(chans, CONV_COLS, col0)
    assert ts % SUBLANES == 0
    cb0 = col0 // tc
    rb = ts // SUBLANES
    if buf is None:
        buf8 = jnp.zeros((bsz, SUBLANES, chans), x2.dtype)
    else:
        buf8 = jnp.pad(buf.astype(x2.dtype), ((0, 0), (SUBLANES - (width - 1), 0), (0, 0)))
    return pl.pallas_call(
        functools.partial(_conv_body, width=width), grid=(bsz, nt, chans // tc),
        in_specs=[
            pl.BlockSpec((ts, tc), lambda bi, i, c: (bi * nt + i, cb0 + c)),
            pl.BlockSpec((SUBLANES, tc), lambda bi, i, c: (jnp.maximum((bi * nt + i) * rb - 1, 0), cb0 + c)),
            pl.BlockSpec((1, SUBLANES, tc), lambda bi, i, c: (bi, 0, c)),
            pl.BlockSpec((width, tc), lambda bi, i, c: (0, c)),
            pl.BlockSpec((1, tc), lambda bi, i, c: (0, c)),
        ],
        out_specs=pl.BlockSpec((ts, tc), lambda bi, i, c: (bi * nt + i, c)),
        out_shape=jax.ShapeDtypeStruct((bsz * seq, chans), out_dtype),
        compiler_params=_params(("parallel", "parallel", "parallel")), name="conv_silu",
    )(x2, x2, buf8, w, b.reshape(1, chans))


def _conv_state(x2, col0, chans, buf, bsz, seq, width):
    tail = x2.reshape(bsz, seq, -1)[:, max(seq - (width - 1), 0):, col0:col0 + chans]
    if seq >= width - 1:
        return tail
    if buf is None:
        buf = jnp.zeros((bsz, width - 1, chans), x2.dtype)
    return jnp.concatenate([buf.astype(x2.dtype), tail], axis=1)[:, -(width - 1):]


def _mlstm_head(q, k, v, o, gc, gr, ng, c, n, m_prev, *, lp, pad, dh):
    q = q * (dh ** -0.5)
    ii_c, ii_r = gc[:, 0:1], gr[0:1, :]
    ff_c, ff_r = -_softplus(-gc[:, 1:2]), -_softplus(-gr[1:2, :])
    if pad:
        ok_c = lax.broadcasted_iota(jnp.int32, (lp, 1), 0) >= pad
        ok_r = lax.broadcasted_iota(jnp.int32, (1, lp), 1) >= pad
        ii_c, ii_r = jnp.where(ok_c, ii_c, NEG), jnp.where(ok_r, ii_r, NEG)
        ff_c, ff_r = jnp.where(ok_c, ff_c, 0.0), jnp.where(ok_r, ff_r, 0.0)
    b_c = _cumsum_rows(ff_c)
    b_r = _cumsum_lanes(ff_r)
    t_i = lax.broadcasted_iota(jnp.int32, (lp, lp), 0)
    s_i = lax.broadcasted_iota(jnp.int32, (lp, lp), 1)
    log_d = jnp.where(s_i <= t_i, b_c + (ii_r - b_r), NEG)
    log_inter = b_c + m_prev
    m_t = jnp.maximum(log_inter, jnp.max(log_d, axis=1, keepdims=True))
    d_mat = jnp.exp(log_d - m_t)
    w_inter = jnp.exp(log_inter - m_t)
    qb, kb, vb = q.astype(BF16), k.astype(BF16), v.astype(BF16)
    s = _dot_nt(qb, kb) * d_mat
    num = _dot(s.astype(BF16), vb) + w_inter * _dot(qb, c.astype(BF16))
    den = jnp.sum(s, axis=1, keepdims=True) + w_inter * jnp.sum(q * n, axis=1, keepdims=True)
    h = num / jnp.maximum(jnp.abs(den), jnp.exp(-m_t))
    h = h * lax.rsqrt(jnp.mean(h * h, axis=1, keepdims=True) + NORM_EPS) * ng
    hg = jax.nn.sigmoid(o) * h[pad:, :]

    b_last = b_c[lp - 1:lp, :]
    m_last = m_t[lp - 1:lp, :]
    w_end = jnp.exp(b_last - b_c + ii_c - m_last)
    f_end = jnp.exp(b_last + m_prev - m_last)
    kw = k * w_end
    c_new = f_end * c + _dot_tn(kw.astype(BF16), vb)
    n_new = f_end * n + jnp.sum(kw, axis=0, keepdims=True)
    return hg, c_new, n_new, m_last


def _mlstm_body(*refs, lr, lp, has_state, has_prev, dh, hps, layer):
    it = iter(refs)
    q_ref, k_ref, v_ref, o_ref, gc_ref, gr_ref, ng_ref = (next(it) for _ in range(7))
    c0_ref, n0_ref, m0_ref = (next(it), next(it), next(it)) if has_state else (None, None, None)
    if has_prev:
        next(it)
    hg_ref, c1_ref, n1_ref, m1_ref, c_s, n_s, m_s = it
    ci = pl.program_id(2)
    pad = lp - lr

    @pl.when(ci == 0)
    def _():
        if has_state:
            c_s[...] = c0_ref[0, 0]
            n_s[...] = n0_ref[0]
            m_s[...] = m0_ref[0]
        else:
            c_s[...] = jnp.zeros_like(c_s)
            n_s[...] = jnp.zeros_like(n_s)
            m_s[...] = jnp.zeros_like(m_s)

    args = []
    for u in range(hps):
        cs = slice(u * dh, (u + 1) * dh)
        args.append((_pad_front(q_ref[:, cs].astype(F32), pad), _pad_front(k_ref[:, cs].astype(F32), pad),
                     _pad_front(v_ref[:, cs].astype(F32), pad), o_ref[:, cs], gc_ref[u, 0], gr_ref[u, 0],
                     ng_ref[:, cs], c_s[u], n_s[u], m_s[u]))
    outs = [_mlstm_head(*a, lp=lp, pad=pad, dh=dh) for a in args]
    for u, (hg, c_new, n_new, m_new) in enumerate(outs):
        hg_ref[:, u * dh:(u + 1) * dh] = hg.astype(hg_ref.dtype)
        c_s[u] = c_new
        n_s[u] = n_new
        m_s[u] = m_new

    @pl.when(ci == pl.num_programs(2) - 1)
    def _():
        if has_prev:
            c1_ref[0, 0] = c_s[...]
        else:
            for l in range(c1_ref.shape[0]):
                c1_ref[l, 0] = c_s[...] if l == layer else jnp.zeros_like(c_s)
        n1_ref[0] = n_s[...]
        m1_ref[0] = m_s[...]


def _mlstm_core(qk, vo, gates, norm_g, bsz, seq, heads, layer, n_layers, state, c_prev, out_dtype):
    inner = qk.shape[1] // 2
    dh = inner // heads
    hps = math.gcd(heads, ML_HEADS_PER_STEP)
    nhg = heads // hps
    lr = math.gcd(seq, ML_CHUNK)
    lp = lr if lr % SEQ_PAD == 0 else SEQ_PAD
    nc = seq // lr
    pad = lp - lr
    gi = gates[:, :heads].reshape(bsz, seq, heads)
    gf = gates[:, heads:2 * heads].reshape(bsz, seq, heads)
    g = jnp.stack([gi, gf], axis=-1)
    g = g.reshape(bsz, nc, lr, heads, 2)
    g = jnp.pad(g, ((0, 0), (0, 0), (pad, 0), (0, 0), (0, 0)))
    gc = jnp.transpose(g, (3, 0, 1, 2, 4)).reshape(heads, bsz, nc * lp, 2)
    gr = jnp.transpose(g, (3, 0, 4, 1, 2)).reshape(heads, bsz, 2, nc * lp)
    has_state = state is not None
    ops = [qk, qk, vo, vo, gc, gr, norm_g.reshape(1, inner)]
    specs = [
        pl.BlockSpec((lr, hps * dh), lambda b, h, c: (b * nc + c, h)),
        pl.BlockSpec((lr, hps * dh), lambda b, h, c: (b * nc + c, nhg + h)),
        pl.BlockSpec((lr, hps * dh), lambda b, h, c: (b * nc + c, h)),
        pl.BlockSpec((lr, hps * dh), lambda b, h, c: (b * nc + c, nhg + h)),
        pl.BlockSpec((hps, 1, lp, 2), lambda b, h, c: (h, b, c, 0)),
        pl.BlockSpec((hps, 1, 2, lp), lambda b, h, c: (h, b, 0, c)),
        pl.BlockSpec((1, hps * dh), lambda b, h, c: (0, h)),
    ]
    if has_state:
        c0, n0, m0 = state
        ops += [c0, n0.reshape(bsz, heads, 1, dh), m0.reshape(bsz, heads, 1, 1)]
        specs += [
            pl.BlockSpec((1, 1, hps, dh, dh), lambda b, h, c: (layer, b, h, 0, 0)),
            pl.BlockSpec((1, hps, 1, dh), lambda b, h, c: (b, h, 0, 0)),
            pl.BlockSpec((1, hps, 1, 1), lambda b, h, c: (b, h, 0, 0)),
        ]
    aliases = {}
    if c_prev is not None:
        aliases[len(ops)] = 1
        ops.append(c_prev)
        specs.append(pl.BlockSpec(memory_space=pl.ANY))
    body = functools.partial(_mlstm_body, lr=lr, lp=lp, has_state=has_state, has_prev=c_prev is not None,
                             dh=dh, hps=hps, layer=layer)
    if c_prev is not None:
        c1_spec = pl.BlockSpec((1, 1, hps, dh, dh), lambda b, h, c: (layer, b, h, 0, 0))
    else:
        c1_spec = pl.BlockSpec((n_layers, 1, hps, dh, dh), lambda b, h, c: (0, b, h, 0, 0))
    hg, c1, n1, m1 = pl.pallas_call(
        body, grid=(bsz, nhg, nc), in_specs=specs, input_output_aliases=aliases,
        out_specs=[
            pl.BlockSpec((lr, hps * dh), lambda b, h, c: (b * nc + c, h)),
            c1_spec,
            pl.BlockSpec((1, hps, 1, dh), lambda b, h, c: (b, h, 0, 0)),
            pl.BlockSpec((1, hps, 1, 1), lambda b, h, c: (b, h, 0, 0)),
        ],
        out_shape=[
            jax.ShapeDtypeStruct((bsz * seq, inner), out_dtype),
            jax.ShapeDtypeStruct((n_layers, bsz, heads, dh, dh), F32),
            jax.ShapeDtypeStruct((bsz, heads, 1, dh), F32),
            jax.ShapeDtypeStruct((bsz, heads, 1, 1), F32),
        ],
        scratch_shapes=[pltpu.VMEM((hps, dh, dh), F32), pltpu.VMEM((hps, 1, dh), F32),
                        pltpu.VMEM((hps, 1, 1), F32)],
        compiler_params=_params(("parallel", "parallel", "arbitrary")), name="mlstm_core",
    )(*ops)
    return hg, c1, n1.reshape(bsz, heads, dh), m1.reshape(bsz, heads)


def _ssd_group(x, bm, cm, z, dtc, dtr, pr, pc, ng, h, *, lp, pad, hpg, hd):
    dd_c = _softplus(dtc + pr[0:1, :])
    dd_r = _softplus(dtr + pc[:, 0:1])
    if pad:
        dd_c = jnp.where(lax.broadcasted_iota(jnp.int32, (lp, hpg), 0) >= pad, dd_c, 0.0)
        dd_r = jnp.where(lax.broadcasted_iota(jnp.int32, (hpg, lp), 1) >= pad, dd_r, 0.0)
    cum_c = _cumsum_rows(dd_c * pr[1:2, :])
    cum_r = _cumsum_lanes(dd_r * pc[:, 1:2])
    cb = _dot_nt(cm, bm)
    t_i = lax.broadcasted_iota(jnp.int32, (lp, lp), 0)
    s_i = lax.broadcasted_iota(jnp.int32, (lp, lp), 1)
    tri = s_i <= t_i
    head_of_lane = lax.broadcasted_iota(jnp.int32, (1, hpg * hd), 1) // hd
    y = jnp.zeros((lp, hpg * hd), F32)
    for r in range(hpg):
        seg = jnp.where(tri, cum_c[:, r:r + 1] - cum_r[r:r + 1, :], NEG)
        w = jnp.exp(seg) * cb * dd_r[r:r + 1, :]
        xr = jnp.where(head_of_lane == r, x, 0.0).astype(BF16)
        y = y + _dot(w.astype(BF16), xr)
    y = y + _expand_lanes(jnp.exp(cum_c), hd) * _dot_nt(cm, h.astype(BF16))
    y = y + _expand_lanes(pr[2:3, :], hd) * x
    yv = y[pad:, :] * (z * jax.nn.sigmoid(z))
    yv = yv * lax.rsqrt(jnp.mean(yv * yv, axis=1, keepdims=True) + NORM_EPS) * ng
    w_end = jnp.exp(cum_c[lp - 1:lp, :] - cum_c) * dd_c
    xw = x * _expand_lanes(w_end, hd)
    decay = _expand_rows(jnp.exp(cum_r[:, lp - 1:lp]), hd)
    return yv, decay * h + _dot_tn(xw.astype(BF16), bm)


def _ssd_body(*refs, lr, lp, has_state, hpg, hd, gps):
    if has_state:
        (x_ref, bm_ref, cm_ref, z_ref, dtc_ref, dtr_ref, pr_ref, pc_ref, ng_ref, h0_ref,
         y_ref, h1_ref, h_s) = refs
    else:
        (x_ref, bm_ref, cm_ref, z_ref, dtc_ref, dtr_ref, pr_ref, pc_ref, ng_ref,
         y_ref, h1_ref, h_s) = refs
    ci = pl.program_id(2)
    pad = lp - lr
    gw = hpg * hd
    ns = bm_ref.shape[1] // gps

    @pl.when(ci == 0)
    def _():
        if has_state:
            h_s[...] = h0_ref[0]
        else:
            h_s[...] = jnp.zeros_like(h_s)

    args = []
    for gi in range(gps):
        xs = slice(gi * gw, (gi + 1) * gw)
        bs = slice(gi * ns, (gi + 1) * ns)
        args.append((_pad_front(x_ref[:, xs], pad), _pad_front(bm_ref[:, bs], pad).astype(BF16),
                     _pad_front(cm_ref[:, bs], pad).astype(BF16), z_ref[:, xs], dtc_ref[gi, 0], dtr_ref[gi, 0],
                     pr_ref[gi], pc_ref[gi], ng_ref[:, xs], h_s[gi]))
    outs = [_ssd_group(*a, lp=lp, pad=pad, hpg=hpg, hd=hd) for a in args]
    for gi, (yv, h_new) in enumerate(outs):
        y_ref[:, gi * gw:(gi + 1) * gw] = yv.astype(y_ref.dtype)
        h_s[gi] = h_new

    @pl.when(ci == pl.num_programs(2) - 1)
    def _():
        h1_ref[0] = h_s[...]


def _ssd_core(zx, xbc, dt_raw, dt_bias, a_neg, d_skip, norm_g, bsz, seq, groups, nstate, hd, h0, out_dtype):
    inner = norm_g.shape[0]
    heads = inner // hd
    hpg = heads // groups
    gw = hpg * hd
    gps = math.gcd(groups, SSD_GROUPS_PER_STEP)
    ngs = groups // gps
    lr = math.gcd(seq, SSD_CHUNK)
    lp = lr if lr % SEQ_PAD == 0 else SEQ_PAD
    nc = seq // lr
    pad = lp - lr
    d = dt_raw[:, :heads].reshape(bsz, nc, lr, groups, hpg)
    d = jnp.pad(d, ((0, 0), (0, 0), (pad, 0), (0, 0), (0, 0)))
    dtc = jnp.transpose(d, (3, 0, 1, 2, 4)).reshape(groups, bsz, nc * lp, hpg)
    dtr = jnp.transpose(d, (3, 0, 4, 1, 2)).reshape(groups, bsz, hpg, nc * lp)
    par = jnp.stack([dt_bias, a_neg, d_skip]).astype(F32).reshape(3, groups, hpg)
    pr = jnp.transpose(par, (1, 0, 2))
    pc = jnp.transpose(par, (1, 2, 0))
    b_off = inner // (gps * nstate)
    c_off = b_off + ngs
    has_state = h0 is not None
    ops = [xbc, xbc, xbc, zx, dtc, dtr, pr, pc, norm_g.reshape(1, inner)]
    specs = [
        pl.BlockSpec((lr, gps * gw), lambda b, g, c: (b * nc + c, g)),
        pl.BlockSpec((lr, gps * nstate), lambda b, g, c: (b * nc + c, b_off + g)),
        pl.BlockSpec((lr, gps * nstate), lambda b, g, c: (b * nc + c, c_off + g)),
        pl.BlockSpec((lr, gps * gw), lambda b, g, c: (b * nc + c, g)),
        pl.BlockSpec((gps, 1, lp, hpg), lambda b, g, c: (g, b, c, 0)),
        pl.BlockSpec((gps, 1, hpg, lp), lambda b, g, c: (g, b, 0, c)),
        pl.BlockSpec((gps, 3, hpg), lambda b, g, c: (g, 0, 0)),
        pl.BlockSpec((gps, hpg, 3), lambda b, g, c: (g, 0, 0)),
        pl.BlockSpec((1, gps * gw), lambda b, g, c: (0, g)),
    ]
    if has_state:
        ops.append(h0.reshape(bsz, groups, gw, nstate))
        specs.append(pl.BlockSpec((1, gps, gw, nstate), lambda b, g, c: (b, g, 0, 0)))
    body = functools.partial(_ssd_body, lr=lr, lp=lp, has_state=has_state, hpg=hpg, hd=hd, gps=gps)
    y, h1 = pl.pallas_call(
        body, grid=(bsz, ngs, nc), in_specs=specs,
        out_specs=[
            pl.BlockSpec((lr, gps * gw), lambda b, g, c: (b * nc + c, g)),
            pl.BlockSpec((1, gps, gw, nstate), lambda b, g, c: (b, g, 0, 0)),
        ],
        out_shape=[
            jax.ShapeDtypeStruct((bsz * seq, inner), out_dtype),
            jax.ShapeDtypeStruct((bsz, groups, gw, nstate), F32),
        ],
        scratch_shapes=[pltpu.VMEM((gps, gw, nstate), F32)],
        compiler_params=_params(("parallel", "parallel", "arbitrary")), name="ssd_core",
    )(*ops)
    return y, h1.reshape(bsz, heads, hd, nstate)


def _sb_logs2(w):
    t = jnp.log2(1.0 + jnp.exp2(-jnp.abs(w)))
    lb = jnp.minimum(w, 0.0) - t
    return lb, lb - w


def _sbp_body(bias_ref, q_ref, k_ref, v_ref, o_ref, *, blk, hd, hps):
    hg = pl.program_id(1)
    qi = pl.program_id(2)
    scale = hd ** -0.5
    bias = [bias_ref[hg * hps + u] * LOG2E for u in range(hps)]
    qb = [q_ref[0, :, u * hd:(u + 1) * hd].astype(BF16) for u in range(hps)]
    j_i = lax.broadcasted_iota(jnp.int32, (2 * blk, blk), 0)
    s_i = lax.broadcasted_iota(jnp.int32, (2 * blk, blk), 1)
    tri2 = jnp.where((j_i % blk) > s_i, 1.0, 0.0).astype(BF16)
    t_i = lax.broadcasted_iota(jnp.int32, (blk, blk), 0)
    strict = lax.broadcasted_iota(jnp.int32, (blk, blk), 1) < t_i

    def rows(kb):
        return pl.ds(pl.multiple_of(kb * blk, blk), blk)

    def scores(kb, u):
        return _dot_nt(qb[u], k_ref[0, rows(kb), u * hd:(u + 1) * hd].astype(BF16))

    def weights(s, run, u, mask):
        lb, lr = _sb_logs2(s * (scale * LOG2E) + bias[u])
        if mask is not None:
            lr = jnp.where(mask, lr, 0.0)
        hi, lo = _split2(lr)
        local = _dot(jnp.concatenate([hi, lo], axis=1), tri2)
        a = jnp.exp2(lb + (run + local))
        if mask is not None:
            a = jnp.where(mask, a, 0.0)
        return a.astype(BF16), run + (local[:, 0:1] + lr[:, 0:1])

    def weighted_values(a, kb, u):
        return _dot(a, v_ref[0, rows(kb), u * hd:(u + 1) * hd].astype(BF16))

    def step(j, carry):
        kb = qi - 1 - j
        out = []
        for u in range(hps):
            run, acc, s, a_prev = carry[u]
            s_next = scores(jnp.maximum(kb - 1, 0), u)
            acc = acc + weighted_values(a_prev, kb + 1, u)
            a, run = weights(s, run, u, None)
            out.append((run, acc, s_next, a))
        return tuple(out)

    carry = []
    for u in range(hps):
        a, run = weights(scores(qi, u), jnp.zeros((blk, 1), F32), u, strict)
        carry.append((run, jnp.zeros((blk, hd), F32), scores(jnp.maximum(qi - 1, 0), u), a))
    carry = lax.fori_loop(0, qi, step, tuple(carry))
    for u in range(hps):
        acc = carry[u][1] + weighted_values(carry[u][3], 0, u)
        o_ref[:, u * hd:(u + 1) * hd] = acc.astype(o_ref.dtype)


def _sb_prompt(qkv, bias, bsz, seq, heads, hd, out_dtype):
    blk = math.gcd(seq, SB_BLOCK)
    nq = seq // blk
    hps = math.gcd(heads, SB_HEADS_PER_STEP)
    nhg = heads // hps
    body = functools.partial(_sbp_body, blk=blk, hd=hd, hps=hps)
    return pl.pallas_call(
        body, grid=(bsz, nhg, nq),
        in_specs=[
            pl.BlockSpec(memory_space=pltpu.SMEM),
            pl.BlockSpec((1, blk, hps * hd), lambda b, h, i: (0, b * nq + i, h)),
            pl.BlockSpec((1, seq, hps * hd), lambda b, h, i: (1, b, h)),
            pl.BlockSpec((1, seq, hps * hd), lambda b, h, i: (2, b, h)),
        ],
        out_specs=pl.BlockSpec((blk, hps * hd), lambda b, h, i: (b * nq + i, h)),
        out_shape=jax.ShapeDtypeStruct((bsz * seq, heads * hd), out_dtype),
        compiler_params=_params(("parallel", "parallel", "arbitrary")), name="sb_prompt",
    )(bias.astype(F32), qkv, qkv, qkv)


def _sbs_body(pt_ref, q_ref, kn_ref, vn_ref, *rest, sq, heads, hd, page, ppg):
    del pt_ref
    kp_refs, vp_refs = rest[:ppg], rest[ppg:2 * ppg]
    bias_ref, o_ref, run_s, acc_s, qbd_s = rest[2 * ppg:]
    p = pl.program_id(1)
    width = heads * hd
    cols = qbd_s.shape[0]
    scale = hd ** -0.5

    def own_head():
        row_i = lax.broadcasted_iota(jnp.int32, (cols, width), 0)
        lane_i = lax.broadcasted_iota(jnp.int32, (cols, width), 1)
        return (row_i // sq) == (lane_i // hd)

    def process(kks, vvs, mask):
        n = len(kks)
        j_i = lax.broadcasted_iota(jnp.int32, (page, 2 * page), 1)
        s_i = lax.broadcasted_iota(jnp.int32, (page, 2 * page), 0)
        tri2 = jnp.where((j_i % page) > s_i, 1.0, 0.0).astype(BF16)
        kcat = kks[0] if n == 1 else jnp.concatenate(kks, axis=0)
        z = _dot_nt(kcat, qbd_s[...]) * (scale * LOG2E) + bias_ref[...] * LOG2E
        lb, lr = _sb_logs2(z)
        if mask is not None:
            lr = jnp.where(mask, lr, 0.0)
        run = run_s[...]
        parts = []
        for i in range(n):
            rs = slice(i * page, (i + 1) * page)
            hi, lo = _split2(lr[rs])
            local = _dot(tri2, jnp.concatenate([hi, lo], axis=0))
            a = jnp.exp2(lb[rs] + (run + local))
            if mask is not None:
                a = jnp.where(mask, a, 0.0)
            parts.append(a.astype(BF16))
            run = run + (local[0:1, :] + lr[rs][0:1, :])
        run_s[...] = run
        acat = parts[0] if n == 1 else jnp.concatenate(parts, axis=0)
        vcat = vvs[0] if n == 1 else jnp.concatenate(vvs, axis=0)
        acc_s[...] += _dot_tn(acat, vcat)

    def load_page(ref):
        return jnp.concatenate([ref[0, 0, pl.ds(h, page, stride=heads), :] for h in range(heads)],
                               axis=1).astype(BF16)

    @pl.when(p == 0)
    def _():
        run_s[...] = jnp.zeros_like(run_s)
        acc_s[...] = jnp.zeros_like(acc_s)
        q = q_ref[0]
        tiled = jnp.concatenate([q] * heads + [jnp.zeros((cols - heads * sq, width), F32)], axis=0)
        qbd_s[...] = jnp.where(own_head(), tiled, 0.0).astype(BF16)
        key_i = lax.broadcasted_iota(jnp.int32, (page, cols), 0)
        qry_i = lax.broadcasted_iota(jnp.int32, (page, cols), 1) % sq
        zeros = jnp.zeros((page - sq, width), F32)
        process([jnp.concatenate([kn_ref[0], zeros], axis=0).astype(BF16)],
                [jnp.concatenate([vn_ref[0], zeros], axis=0).astype(BF16)], key_i < qry_i)

    @pl.when(p > 0)
    def _():
        process([load_page(r) for r in kp_refs], [load_page(r) for r in vp_refs], None)

    @pl.when(p == pl.num_programs(1) - 1)
    def _():
        acc = jnp.where(own_head(), acc_s[...], 0.0)
        out = acc[0:sq, :]
        for h in range(1, heads):
            out = out + acc[h * sq:(h + 1) * sq, :]
        o_ref[...] = out.astype(o_ref.dtype)


def _sb_sample(qkv, pool_k, pool_v, layer, page_table, bias, bsz, sq, heads, hd, out_dtype):
    width = heads * hd
    n_pages = page_table.shape[1]
    page = pool_k.shape[2]
    cols = SB_COLS
    ppg = math.gcd(n_pages, SB_PAGES_PER_STEP)
    assert heads * sq <= cols and sq <= page
    bias_cols = jnp.pad(jnp.repeat(bias.astype(F32), sq), (0, cols - heads * sq)).reshape(1, cols)
    pool_k = pool_k.reshape(pool_k.shape[0], pool_k.shape[1], page * heads, hd)
    pool_v = pool_v.reshape(pool_v.shape[0], pool_v.shape[1], page * heads, hd)

    def page_spec(i):
        def idx(b, s, pt):
            return (layer, pt[b, n_pages - 1 - (jnp.maximum(s - 1, 0) * ppg + i)], 0, 0)
        return pl.BlockSpec((1, 1, page * heads, hd), idx)

    body = functools.partial(_sbs_body, sq=sq, heads=heads, hd=hd, page=page, ppg=ppg)
    grid_spec = pltpu.PrefetchScalarGridSpec(
        num_scalar_prefetch=1, grid=(bsz, n_pages // ppg + 1),
        in_specs=[
            pl.BlockSpec((1, sq, width), lambda b, s, pt: (0, b, 0)),
            pl.BlockSpec((1, sq, width), lambda b, s, pt: (1, b, 0)),
            pl.BlockSpec((1, sq, width), lambda b, s, pt: (2, b, 0)),
            *[page_spec(i) for i in range(ppg)],
            *[page_spec(i) for i in range(ppg)],
            pl.BlockSpec((1, cols), lambda b, s, pt: (0, 0)),
        ],
        out_specs=pl.BlockSpec((sq, width), lambda b, s, pt: (b, 0)),
        scratch_shapes=[pltpu.VMEM((1, cols), F32), pltpu.VMEM((cols, width), F32),
                        pltpu.VMEM((cols, width), BF16)],
    )
    return pl.pallas_call(
        body, grid_spec=grid_spec,
        out_shape=jax.ShapeDtypeStruct((bsz * sq, width), out_dtype),
        compiler_params=_params(("parallel", "arbitrary")), name="sb_sample",
    )(page_table, qkv, qkv, qkv, *([pool_k] * ppg), *([pool_v] * ppg), bias_cols)


def _xa_body(x_ref, g_ref, wq_ref, mk_ref, mv_ref, wo_ref, o_ref, *, heads, hd):
    scale = hd ** -0.5

    x = x_ref[...]
    xn = (x * lax.rsqrt(jnp.mean(x * x, axis=-1, keepdims=True) + NORM_EPS) * g_ref[...]).astype(BF16)
    q = _dot(xn, wq_ref[...]).astype(BF16)
    outs = []
    for h in range(heads):
        cs = slice(h * hd, (h + 1) * hd)
        s = _dot_nt(q[:, cs], mk_ref[0, :, cs].astype(BF16)) * scale
        e = jnp.exp(s - jnp.max(s, axis=1, keepdims=True))
        pr = e / jnp.sum(e, axis=1, keepdims=True)
        outs.append(_dot(pr.astype(BF16), mv_ref[0, :, cs].astype(BF16)).astype(BF16))
    o_ref[...] = x + _dot(jnp.concatenate(outs, axis=1), wo_ref[...])


def _xa_cache_body(x_ref, g_ref, wq_ref, wo_ref, ck_hbm, cv_hbm, o_ref, kbuf, vbuf, sem, *, layer, heads, hd):
    b = pl.program_id(0)
    slot = b % 2

    def copies(bi, sl):
        out = []
        for h in range(heads):
            out.append(pltpu.make_async_copy(ck_hbm.at[layer, bi, :, h, :], kbuf.at[sl, h], sem.at[sl, 0, h]))
            out.append(pltpu.make_async_copy(cv_hbm.at[layer, bi, :, h, :], vbuf.at[sl, h], sem.at[sl, 1, h]))
        return out

    @pl.when(b == 0)
    def _():
        for c in copies(0, 0):
            c.start()

    @pl.when(b + 1 < pl.num_programs(0))
    def _():
        for c in copies(b + 1, 1 - slot):
            c.start()

    for c in copies(b, slot):
        c.wait()

    scale = hd ** -0.5
    x = x_ref[...]
    xn = (x * lax.rsqrt(jnp.mean(x * x, axis=-1, keepdims=True) + NORM_EPS) * g_ref[...]).astype(BF16)
    q = _dot(xn, wq_ref[...]).astype(BF16)
    outs = []
    for h in range(heads):
        s = _dot_nt(q[:, h * hd:(h + 1) * hd], kbuf[slot, h].astype(BF16)) * scale
        e = jnp.exp(s - jnp.max(s, axis=1, keepdims=True))
        pr = e / jnp.sum(e, axis=1, keepdims=True)
        outs.append(_dot(pr.astype(BF16), vbuf[slot, h].astype(BF16)).astype(BF16))
    o_ref[...] = x + _dot(jnp.concatenate(outs, axis=1), wo_ref[...])


def _cross_attn_cached(x, g, wq, cache_k, cache_v, layer, wo, bsz, seq, mem_len, heads):
    d = x.shape[1]
    hd = d // heads
    body = functools.partial(_xa_cache_body, layer=layer, heads=heads, hd=hd)
    return pl.pallas_call(
        body, grid=(bsz,),
        in_specs=[
            pl.BlockSpec((seq, d), lambda b: (b, 0)),
            pl.BlockSpec((1, d), lambda b: (0, 0)),
            pl.BlockSpec((d, d), lambda b: (0, 0)),
            pl.BlockSpec((d, d), lambda b: (0, 0)),
            pl.BlockSpec(memory_space=pl.ANY),
            pl.BlockSpec(memory_space=pl.ANY),
        ],
        out_specs=pl.BlockSpec((seq, d), lambda b: (b, 0)),
        out_shape=jax.ShapeDtypeStruct((bsz * seq, d), F32),
        scratch_shapes=[pltpu.VMEM((2, heads, mem_len, hd), cache_k.dtype),
                        pltpu.VMEM((2, heads, mem_len, hd), cache_v.dtype),
                        pltpu.SemaphoreType.DMA((2, 2, heads))],
        compiler_params=_params(("arbitrary",)), name="cross_attn_cached",
    )(x, g.reshape(1, d), wq, wo, cache_k, cache_v)


def _cross_attn_layer(x, g, wq, mk, mv, mk_idx, mv_idx, wo, bsz, seq, mem_len, heads):
    d = x.shape[1]
    tq = math.gcd(seq, XA_QBLOCK)
    nq = seq // tq
    body = functools.partial(_xa_body, heads=heads, hd=d // heads)

    return pl.pallas_call(
        body, grid=(bsz, nq),
        in_specs=[
            pl.BlockSpec((tq, d), lambda b, i: (b * nq + i, 0)),
            pl.BlockSpec((1, d), lambda b, i: (0, 0)),
            pl.BlockSpec((d, d), lambda b, i: (0, 0)),
            pl.BlockSpec((1, mem_len, d), lambda b, i: (mk_idx, b, 0)),
            pl.BlockSpec((1, mem_len, d), lambda b, i: (mv_idx, b, 0)),
            pl.BlockSpec((d, d), lambda b, i: (0, 0)),
        ],
        out_specs=pl.BlockSpec((tq, d), lambda b, i: (b * nq + i, 0)),
        out_shape=jax.ShapeDtypeStruct((bsz * seq, d), F32),
        compiler_params=_params(("parallel", "parallel")), name="cross_attn",
    )(x, g.reshape(1, d), wq, mk, mv, wo)


def _prep_weights(p):
    ml_inner = p['ml_w_up'].shape[2]
    ssd_main = p['ssd_w_in'].shape[2] - p['ssd_dt_bias'].shape[1]
    ffn_hidden = p['ffn_w_out'].shape[1]

    def per_layer(a, lo=None, hi=None, pad_to=None, dtype=BF16):
        out = []
        for j in range(a.shape[0]):
            m = a[j, :, lo:hi] if a.ndim == 3 else a[j, lo:hi]
            if pad_to is not None:
                m = jnp.pad(m, [(0, 0)] * (m.ndim - 1) + [(0, pad_to - m.shape[-1])])
            out.append(m.astype(dtype))
        return out

    return {
        'ml_w_up': per_layer(p['ml_w_up']),
        'ml_w_qk': per_layer(p['ml_w_qk']),
        'ml_w_vo': per_layer(p['ml_w_vog'], 0, 2 * ml_inner),
        'ml_w_g': per_layer(p['ml_w_vog'], 2 * ml_inner, None, LANES),
        'ml_b_g': per_layer(p['ml_b_gate'], None, None, LANES, F32),
        'ml_w_down': per_layer(p['ml_w_down']),
        'sb_w_qkv': per_layer(p['sb_w_qkv']),
        'sb_w_o': per_layer(p['sb_w_o']),
        'ssd_w_main': per_layer(p['ssd_w_in'], 0, ssd_main),
        'ssd_w_dt': per_layer(p['ssd_w_in'], ssd_main, None, LANES),
        'ssd_w_out': per_layer(p['ssd_w_out']),
        'ssd_a': -jnp.exp(p['ssd_a_log'].astype(F32)),
        'xa_w_q': per_layer(p['xa_w_q']),
        'xa_w_kv': per_layer(p['xa_w_kv']),
        'xa_w_o': per_layer(p['xa_w_o']),
        'ffn_w_gate': per_layer(p['ffn_w_in'], 0, ffn_hidden),
        'ffn_w_up': per_layer(p['ffn_w_in'], ffn_hidden, None),
        'ffn_w_out': per_layer(p['ffn_w_out']),
    }


def _trunk(x3, mem_kv, ml_states, sb_pools, ssd_states, page_table, p, w, dims, act_dtype):
    bsz, seq, d = x3.shape
    depth = p['norm_mix'].shape[0]
    ml_heads, sb_heads, xa_heads, ssd_groups, ssd_state, ssd_hd, mem_len = dims
    x = x3.reshape(bsz * seq, d)
    ml_new, sb_new, ssd_new = [], [], []
    n_ml = n_sb = n_ssd = 0
    n_ml_layers = p['ml_w_up'].shape[0]
    ml_c = None
    for layer in range(depth):
        kind = layer % 3
        if kind == 0:
            j = n_ml
            n_ml += 1
            inner = p['ml_w_up'].shape[2]
            width = p['ml_conv_w'].shape[1]
            xm = _matmul(x, w['ml_w_up'][j], g=p['norm_mix'][layer])
            st = None if ml_states is None else ml_states[j]
            buf = None if st is None else st[3]
            xc = _causal_conv_silu(xm, 0, inner, buf, p['ml_conv_w'][j], p['ml_conv_b'][j], bsz, seq, act_dtype)
            qk = _matmul(xc, w['ml_w_qk'][j])
            vo, gates = _matmul(xm, w['ml_w_vo'][j], extra=(w['ml_w_g'][j], w['ml_b_g'][j]))
            hg, ml_c, n1, m1 = _mlstm_core(qk, vo, gates, p['ml_norm'][j], bsz, seq, ml_heads, j, n_ml_layers,
                                           None if st is None else st[:3], ml_c, act_dtype)
            x = _matmul(hg, w['ml_w_down'][j], res=x)
            ml_new.append((n1, m1, _conv_state(xm, 0, inner, buf, bsz, seq, width)))
        elif kind == 1:
            j = n_sb
            n_sb += 1
            hd = d // sb_heads
            qkv = _matmul(x, w['sb_w_qkv'][j], g=p['norm_mix'][layer], splits=3)
            if sb_pools is None:
                o = _sb_prompt(qkv, p['sb_bias'][j], bsz, seq, sb_heads, hd, act_dtype)
            else:
                o = _sb_sample(qkv, sb_pools[0], sb_pools[1], j, page_table, p['sb_bias'][j],
                               bsz, seq, sb_heads, hd, act_dtype)
            x = _matmul(o, w['sb_w_o'][j], res=x)
            shp = (bsz, seq, sb_heads, hd)
            sb_new.append((qkv[1].reshape(shp), qkv[2].reshape(shp)))
        else:
            j = n_ssd
            n_ssd += 1
            inner = p['ssd_norm'].shape[1]
            conv_ch = p['ssd_conv_w'].shape[2]
            width = p['ssd_conv_w'].shape[1]
            zx, dt_raw = _matmul(x, w['ssd_w_main'][j], g=p['norm_mix'][layer],
                                 extra=(w['ssd_w_dt'][j], jnp.zeros((LANES,), F32)))
            st = None if ssd_states is None else ssd_states[j]
            buf = None if st is None else st[1]
            xbc = _causal_conv_silu(zx, inner, conv_ch, buf, p['ssd_conv_w'][j], p['ssd_conv_b'][j],
                                    bsz, seq, F32)
            y, h1 = _ssd_core(zx, xbc, dt_raw, p['ssd_dt_bias'][j], w['ssd_a'][j],
                              p['ssd_d'][j], p['ssd_norm'][j], bsz, seq, ssd_groups, ssd_state, ssd_hd,
                              None if st is None else st[0], act_dtype)
            x = _matmul(y, w['ssd_w_out'][j], res=x)
            ssd_new.append((h1, _conv_state(zx, inner, conv_ch, buf, bsz, seq, width)))
        mk, mv, mk_idx, mv_idx = mem_kv[layer]
        if mk.ndim == 5:
            x = _cross_attn_cached(x, p['norm_xa'][layer], w['xa_w_q'][layer], mk, mv, mk_idx,
                                   w['xa_w_o'][layer], bsz, seq, mem_len, xa_heads)
        else:
            x = _cross_attn_layer(x, p['norm_xa'][layer], w['xa_w_q'][layer], mk, mv, mk_idx, mv_idx,
                                  w['xa_w_o'][layer], bsz, seq, mem_len, xa_heads)
        hid = _matmul(x, w['ffn_w_gate'][layer], g=p['norm_ffn'][layer], w2=w['ffn_w_up'][layer],
                      out_dtype=act_dtype)
        x = _matmul(hid, w['ffn_w_out'][layer], res=x)
    y = _final_norm(x, p['norm_final'])
    return y.reshape(bsz, seq, d), ml_c, ml_new, sb_new, ssd_new


def _norm_body(x_ref, g_ref, o_ref):
    x = x_ref[...]
    o_ref[...] = x * lax.rsqrt(jnp.mean(x * x, axis=-1, keepdims=True) + NORM_EPS) * g_ref[...]


def _final_norm(x, g):
    t, d = x.shape
    tm = min(ROW_TILE, t)
    return pl.pallas_call(
        _norm_body, grid=(t // tm,),
        in_specs=[pl.BlockSpec((tm, d), lambda i: (i, 0)), pl.BlockSpec((1, d), lambda i: (0, 0))],
        out_specs=pl.BlockSpec((tm, d), lambda i: (i, 0)),
        out_shape=jax.ShapeDtypeStruct((t, d), F32),
        compiler_params=_params(("parallel",)), name="final_norm",
    )(x, g.reshape(1, d))


def _stack(items, idx):
    return jnp.stack([it[idx] for it in items])


def kernel(x_prompt, x_sample, cache_mem_k, cache_mem_v, cache_sb_k, cache_sb_v, state_ml_c, state_ml_n, state_ml_m, state_ml_conv, state_ssd_h, state_ssd_conv, page_table, mem_prompt, norm_mix, norm_xa, norm_ffn, norm_mem, norm_final, ml_w_up, ml_conv_w, ml_conv_b, ml_w_qk, ml_w_vog, ml_b_gate, ml_norm, ml_w_down, sb_w_qkv, sb_bias, sb_w_o, ssd_w_in, ssd_conv_w, ssd_conv_b, ssd_dt_bias, ssd_a_log, ssd_d, ssd_norm, ssd_w_out, xa_w_q, xa_w_kv, xa_w_o, ffn_w_in, ffn_w_out):
    p = dict(norm_mix=norm_mix, norm_xa=norm_xa, norm_ffn=norm_ffn, norm_final=norm_final,
             ml_w_up=ml_w_up, ml_conv_w=ml_conv_w, ml_conv_b=ml_conv_b, ml_w_qk=ml_w_qk,
             ml_w_vog=ml_w_vog, ml_b_gate=ml_b_gate, ml_norm=ml_norm, ml_w_down=ml_w_down,
             sb_w_qkv=sb_w_qkv, sb_bias=sb_bias, sb_w_o=sb_w_o,
             ssd_w_in=ssd_w_in, ssd_conv_w=ssd_conv_w, ssd_conv_b=ssd_conv_b, ssd_dt_bias=ssd_dt_bias,
             ssd_a_log=ssd_a_log, ssd_d=ssd_d, ssd_norm=ssd_norm, ssd_w_out=ssd_w_out,
             xa_w_q=xa_w_q, xa_w_kv=xa_w_kv, xa_w_o=xa_w_o, ffn_w_in=ffn_w_in, ffn_w_out=ffn_w_out)
    w = _prep_weights(p)
    depth = norm_mix.shape[0]
    d = x_prompt.shape[2]
    bp, mem_len = mem_prompt.shape[0], mem_prompt.shape[1]
    ml_heads = state_ml_c.shape[2]
    sb_heads = cache_sb_k.shape[3]
    xa_heads = cache_mem_k.shape[3]
    ssd_hd, ssd_state = state_ssd_h.shape[3], state_ssd_h.shape[4]
    ssd_groups = (ssd_conv_w.shape[2] - ssd_norm.shape[1]) // (2 * ssd_state)
    n_ml, n_ssd = state_ml_c.shape[0], state_ssd_h.shape[0]
    dims = (ml_heads, sb_heads, xa_heads, ssd_groups, ssd_state, ssd_hd, mem_len)

    mem2 = mem_prompt.reshape(bp * mem_len, d)
    mem_kv_p = [_matmul(mem2, w['xa_w_kv'][l], g=norm_mem[l], splits=2) for l in range(depth)]
    y_prompt, ml_c_p, ml_p, sb_p, ssd_p = _trunk(x_prompt, [(kv, kv, 0, 1) for kv in mem_kv_p], None, None, None,
                                         None, p, w, dims, BF16)
    shp = (bp, mem_len, xa_heads, d // xa_heads)
    mem_k_p = jnp.stack([kv[0].reshape(shp) for kv in mem_kv_p])
    mem_v_p = jnp.stack([kv[1].reshape(shp) for kv in mem_kv_p])

    mem_kv_s = [(cache_mem_k, cache_mem_v, l, l) for l in range(depth)]
    ml_cache = [(state_ml_c, state_ml_n[j], state_ml_m[j], state_ml_conv[j]) for j in range(n_ml)]
    ssd_cache = [(state_ssd_h[j], state_ssd_conv[j]) for j in range(n_ssd)]
    y_sample, ml_c_s, ml_s, sb_s, ssd_s = _trunk(x_sample, mem_kv_s, ml_cache, (cache_sb_k, cache_sb_v), ssd_cache,
                                         page_table, p, w, dims, F32)

    return (y_prompt, y_sample, mem_k_p, mem_v_p,
            _stack(sb_p, 0), _stack(sb_p, 1),
            ml_c_p, _stack(ml_p, 0), _stack(ml_p, 1), _stack(ml_p, 2),
            _stack(ssd_p, 0), _stack(ssd_p, 1),
            _stack(sb_s, 0), _stack(sb_s, 1),
            ml_c_s, _stack(ml_s, 0), _stack(ml_s, 1), _stack(ml_s, 2),
            _stack(ssd_s, 0), _stack(ssd_s, 1))
```

```python
import functools
import math

import jax
import jax.numpy as jnp
from jax import lax
from jax.experimental import pallas as pl
from jax.experimental.pallas import tpu as pltpu

F32 = jnp.float32
BF16 = jnp.bfloat16
NORM_EPS = 1e-6
NEG = -1e30
LOG2E = 1.4426950408889634
SUBLANES = 8
VMEM_LIMIT_BYTES = 56 * 1024 * 1024
LANES = 128
MM_VMEM_BUDGET_BYTES = 44 * 1024 * 1024
MXU_FLOPS = 1.0e15
HBM_BYTES_PER_S = 3.0e12
STEP_OVERHEAD_S = 0.35e-6
ROW_TILE = 1024
ML_HEADS_PER_STEP = 4
SEQ_PAD = 16
SB_COLS = 128
ML_CHUNK = 256
SSD_CHUNK = 128
SSD_GROUPS_PER_STEP = 4
SB_BLOCK = 256
SB_HEADS_PER_STEP = 2
SB_PAGES_PER_STEP = 8
XA_QBLOCK = 512
XA_ROWS_PER_STEP = 4
CONV_COLS = 2048
CONV_ROWS = 512


def _params(sem):
    return pltpu.CompilerParams(dimension_semantics=sem, vmem_limit_bytes=VMEM_LIMIT_BYTES)


def _dot(a, b):
    return jnp.dot(a, b, preferred_element_type=F32)


def _dot_nt(a, b):
    return lax.dot_general(a, b, (((1,), (1,)), ((), ())), preferred_element_type=F32)


def _dot_tn(a, b):
    return lax.dot_general(a, b, (((0,), (0,)), ((), ())), preferred_element_type=F32)


def _split3(x):
    hi = x.astype(BF16)
    r = x - hi.astype(F32)
    mid = r.astype(BF16)
    lo = (r - mid.astype(F32)).astype(BF16)
    return hi, mid, lo


def _split2(x):
    hi = x.astype(BF16)
    lo = (x - hi.astype(F32)).astype(BF16)
    return hi, lo


def _cumsum_rows(x):
    n = x.shape[0]
    t = lax.broadcasted_iota(jnp.int32, (n, n), 0)
    s = lax.broadcasted_iota(jnp.int32, (n, n), 1)
    m = jnp.where(s <= t, 1.0, 0.0).astype(BF16)
    hi, mid, lo = _split3(x)
    return _dot(m, hi) + _dot(m, mid) + _dot(m, lo)


def _cumsum_lanes(x):
    n = x.shape[1]
    j = lax.broadcasted_iota(jnp.int32, (n, n), 0)
    s = lax.broadcasted_iota(jnp.int32, (n, n), 1)
    m = jnp.where(j <= s, 1.0, 0.0).astype(BF16)
    hi, mid, lo = _split3(x)
    return _dot(hi, m) + _dot(mid, m) + _dot(lo, m)


def _softplus(x):
    return jnp.maximum(x, 0.0) + jnp.log1p(jnp.exp(-jnp.abs(x)))


def _pad_front(x, pad):
    if pad == 0:
        return x
    return jnp.concatenate([jnp.zeros((pad, x.shape[1]), x.dtype), x], axis=0)


def _expand_lanes(cols, width):
    n, k = cols.shape
    lane = lax.broadcasted_iota(jnp.int32, (n, k * width), 1)
    out = jnp.broadcast_to(cols[:, k - 1:k], (n, k * width))
    for r in range(k - 2, -1, -1):
        out = jnp.where(lane < (r + 1) * width, cols[:, r:r + 1], out)
    return out


def _expand_rows(rows, height):
    k, n = rows.shape
    sub = lax.broadcasted_iota(jnp.int32, (k * height, n), 0)
    out = jnp.broadcast_to(rows[k - 1:k, :], (k * height, n))
    for r in range(k - 2, -1, -1):
        out = jnp.where(sub < (r + 1) * height, rows[r:r + 1, :], out)
    return out


def _mm_body(*refs, has_norm, swiglu, has_res, has_extra):
    it = iter(refs)
    x_ref = next(it)
    g_ref = next(it) if has_norm else None
    w_ref = next(it)
    w2_ref = next(it) if swiglu else None
    r_ref = next(it) if has_res else None
    we_ref, be_ref = (next(it), next(it)) if has_extra else (None, None)
    o_ref = next(it)
    x = x_ref[...]
    if has_norm:
        x32 = x.astype(F32)
        x = x32 * lax.rsqrt(jnp.mean(x32 * x32, axis=-1, keepdims=True) + NORM_EPS) * g_ref[...]
    xb = x.astype(BF16)
    acc = _dot(xb, w_ref[...])
    if swiglu:
        acc = acc * jax.nn.sigmoid(acc) * _dot(xb, w2_ref[...])
    if has_res:
        acc = r_ref[...] + acc
    o_ref[...] = acc.reshape(o_ref.shape).astype(o_ref.dtype)
    if has_extra:
        next(it)[0] = _dot(xb, we_ref[...]) + be_ref[...]


def _mm_tiles(t, k, ncols, x_bytes, out_bytes, n_w, has_res, n_passes):
    best = None
    for tm in sorted({min(t, m) for m in (256, 512, 1024, 2048)}):
        if t % tm:
            continue
        for d in range(1, ncols // LANES + 1):
            tn = ncols // d
            if ncols % d or tn % LANES:
                continue
            vmem = 2 * (tm * k * x_bytes + k * tn * 2 * n_w + tm * tn * (out_bytes + 4 * has_res)
                        + tm * LANES * 4 + k * LANES * 2)
            if vmem > MM_VMEM_BUDGET_BYTES:
                continue
            col_tiles = d * n_passes
            steps = col_tiles * (t // tm)
            hbm = col_tiles * t * k * x_bytes + t * ncols * n_passes * (out_bytes + 4 * has_res)
            cost = (max(2.0 * t * k * ncols * n_passes * n_w / MXU_FLOPS, hbm / HBM_BYTES_PER_S)
                    + steps * STEP_OVERHEAD_S)
            if best is None or cost < best[0]:
                best = (cost, tm, tn)
    assert best is not None
    return best[1], best[2]


def _matmul(x, w, *, g=None, w2=None, res=None, extra=None, splits=1, out_dtype=F32):
    t, k = x.shape
    n = w.shape[1]
    ncols = n // splits
    tm, tn = _mm_tiles(t, k, ncols, x.dtype.itemsize, jnp.dtype(out_dtype).itemsize,
                       2 if w2 is not None else 1, res is not None, splits)
    cps = ncols // tn
    grid = (n // tn, t // tm)
    ops = [x]
    specs = [pl.BlockSpec((tm, k), lambda j, i: (i, 0))]
    if g is not None:
        ops.append(g.reshape(1, k))
        specs.append(pl.BlockSpec((1, k), lambda j, i: (0, 0)))
    ops.append(w)
    specs.append(pl.BlockSpec((k, tn), lambda j, i: (0, j)))
    if w2 is not None:
        ops.append(w2)
        specs.append(pl.BlockSpec((k, tn), lambda j, i: (0, j)))
    if res is not None:
        ops.append(res)
        specs.append(pl.BlockSpec((tm, tn), lambda j, i: (i, j)))
    if extra is not None:
        ops += [extra[0], extra[1].reshape(1, LANES)]
        specs += [pl.BlockSpec((k, LANES), lambda j, i: (0, 0)), pl.BlockSpec((1, LANES), lambda j, i: (0, 0))]
    if splits == 1:
        out_specs = [pl.BlockSpec((tm, tn), lambda j, i: (i, j))]
        out_shape = [jax.ShapeDtypeStruct((t, n), out_dtype)]
    else:
        out_specs = [pl.BlockSpec((1, tm, tn), lambda j, i: (j // cps, i, j % cps))]
        out_shape = [jax.ShapeDtypeStruct((splits, t, ncols), out_dtype)]
    if extra is not None:
        out_specs.append(pl.BlockSpec((1, tm, LANES), lambda j, i: (j, i, 0)))
        out_shape.append(jax.ShapeDtypeStruct((grid[0], t, LANES), F32))
    body = functools.partial(_mm_body, has_norm=g is not None, swiglu=w2 is not None,
                             has_res=res is not None, has_extra=extra is not None)
    out = pl.pallas_call(
        body, grid=grid, in_specs=specs, out_specs=out_specs, out_shape=out_shape,
        compiler_params=_params(("arbitrary", "arbitrary")), name="matmul",
    )(*ops)
    return (out[0], out[1][0]) if extra is not None else out[0]


def _conv_body(x_ref, prev_ref, buf_ref, w_ref, b_ref, o_ref, *, width):
    ts = x_ref.shape[0]
    x = x_ref[...]
    hist = jnp.where(pl.program_id(1) == 0, buf_ref[0], prev_ref[...])
    xx = jnp.concatenate([hist, x], axis=0)
    w = w_ref[...]
    y = b_ref[...]
    for j in range(width):
        k = width - 1 - j
        win = x if k == 0 else pltpu.roll(xx, k, 0)[SUBLANES:, :]
        y = y + win * w[j:j + 1, :]
    o_ref[...] = (y * jax.nn.sigmoid(y)).astype(o_ref.dtype)


def _causal_conv_silu(x2, col0, chans, buf, w, b, bsz, seq, out_dtype):
    width = w.shape[0]
    assert width - 1 <= SUBLANES
    ts = math.gcd(seq, CONV_ROWS)
    nt = seq // ts
    tc = math.gcd(chans, CONV_COLS, col0)
    assert ts % SUBLANES == 0
    cb0 = col0 // tc
    rb = ts // SUBLANES
    if buf is None:
        buf8 = jnp.zeros((bsz, SUBLANES, chans), x2.dtype)
    else:
        buf8 = jnp.pad(buf.astype(x2.dtype), ((0, 0), (SUBLANES - (width - 1), 0), (0, 0)))
    return pl.pallas_call(
        functools.partial(_conv_body, width=width), grid=(bsz, nt, chans // tc),
        in_specs=[
            pl.BlockSpec((ts, tc), lambda bi, i, c: (bi * nt + i, cb0 + c)),
            pl.BlockSpec((SUBLANES, tc), lambda bi, i, c: (jnp.maximum((bi * nt + i) * rb - 1, 0), cb0 + c)),
            pl.BlockSpec((1, SUBLANES, tc), lambda bi, i, c: (bi, 0, c)),
            pl.BlockSpec((width, tc), lambda bi, i, c: (0, c)),
            pl.BlockSpec((1, tc), lambda bi, i, c: (0, c)),
        ],
        out_specs=pl.BlockSpec((ts, tc), lambda bi, i, c: (bi * nt + i, c)),
        out_shape=jax.ShapeDtypeStruct((bsz * seq, chans), out_dtype),
        compiler_params=_params(("parallel", "parallel", "parallel")), name="conv_silu",
    )(x2, x2, buf8, w, b.reshape(1, chans))


def _conv_state(x2, col0, chans, buf, bsz, seq, width):
    tail = x2.reshape(bsz, seq, -1)[:, max(seq - (width - 1), 0):, col0:col0 + chans]
    if seq >= width - 1:
        return tail
    if buf is None:
        buf = jnp.zeros((bsz, width - 1, chans), x2.dtype)
    return jnp.concatenate([buf.astype(x2.dtype), tail], axis=1)[:, -(width - 1):]


def _mlstm_head(q, k, v, o, gc, gr, ng, c, n, m_prev, *, lp, pad, dh):
    q = q * (dh ** -0.5)
    ii_c, ii_r = gc[:, 0:1], gr[0:1, :]
    ff_c, ff_r = -_softplus(-gc[:, 1:2]), -_softplus(-gr[1:2, :])
    if pad:
        ok_c = lax.broadcasted_iota(jnp.int32, (lp, 1), 0) >= pad
        ok_r = lax.broadcasted_iota(jnp.int32, (1, lp), 1) >= pad
        ii_c, ii_r = jnp.where(ok_c, ii_c, NEG), jnp.where(ok_r, ii_r, NEG)
        ff_c, ff_r = jnp.where(ok_c, ff_c, 0.0), jnp.where(ok_r, ff_r, 0.0)
    b_c = _cumsum_rows(ff_c)
    b_r = _cumsum_lanes(ff_r)
    t_i = lax.broadcasted_iota(jnp.int32, (lp, lp), 0)
    s_i = lax.broadcasted_iota(jnp.int32, (lp, lp), 1)
    log_d = jnp.where(s_i <= t_i, b_c + (ii_r - b_r), NEG)
    log_inter = b_c + m_prev
    m_t = jnp.maximum(log_inter, jnp.max(log_d, axis=1, keepdims=True))
    d_mat = jnp.exp(log_d - m_t)
    w_inter = jnp.exp(log_inter - m_t)
    qb, kb, vb = q.astype(BF16), k.astype(BF16), v.astype(BF16)
    s = _dot_nt(qb, kb) * d_mat
    num = _dot(s.astype(BF16), vb) + w_inter * _dot(qb, c.astype(BF16))
    den = jnp.sum(s, axis=1, keepdims=True) + w_inter * jnp.sum(q * n, axis=1, keepdims=True)
    h = num / jnp.maximum(jnp.abs(den), jnp.exp(-m_t))
    h = h * lax.rsqrt(jnp.mean(h * h, axis=1, keepdims=True) + NORM_EPS) * ng
    hg = jax.nn.sigmoid(o) * h[pad:, :]

    b_last = b_c[lp - 1:lp, :]
    m_last = m_t[lp - 1:lp, :]
    w_end = jnp.exp(b_last - b_c + ii_c - m_last)
    f_end = jnp.exp(b_last + m_prev - m_last)
    kw = k * w_end
    c_new = f_end * c + _dot_tn(kw.astype(BF16), vb)
    n_new = f_end * n + jnp.sum(kw, axis=0, keepdims=True)
    return hg, c_new, n_new, m_last


def _mlstm_body(*refs, lr, lp, has_state, has_prev, dh, hps, layer):
    it = iter(refs)
    q_ref, k_ref, v_ref, o_ref, gc_ref, gr_ref, ng_ref = (next(it) for _ in range(7))
    c0_ref, n0_ref, m0_ref = (next(it), next(it), next(it)) if has_state else (None, None, None)
    if has_prev:
        next(it)
    hg_ref, c1_ref, n1_ref, m1_ref, c_s, n_s, m_s = it
    ci = pl.program_id(2)
    pad = lp - lr

    @pl.when(ci == 0)
    def _():
        if has_state:
            c_s[...] = c0_ref[0, 0]
            n_s[...] = n0_ref[0]
            m_s[...] = m0_ref[0]
        else:
            c_s[...] = jnp.zeros_like(c_s)
            n_s[...] = jnp.zeros_like(n_s)
            m_s[...] = jnp.zeros_like(m_s)

    args = []
    for u in range(hps):
        cs = slice(u * dh, (u + 1) * dh)
        args.append((_pad_front(q_ref[:, cs].astype(F32), pad), _pad_front(k_ref[:, cs].astype(F32), pad),
                     _pad_front(v_ref[:, cs].astype(F32), pad), o_ref[:, cs], gc_ref[u, 0], gr_ref[u, 0],
                     ng_ref[:, cs], c_s[u], n_s[u], m_s[u]))
    outs = [_mlstm_head(*a, lp=lp, pad=pad, dh=dh) for a in args]
    for u, (hg, c_new, n_new, m_new) in enumerate(outs):
        hg_ref[:, u * dh:(u + 1) * dh] = hg.astype(hg_ref.dtype)
        c_s[u] = c_new
        n_s[u] = n_new
        m_s[u] = m_new

    @pl.when(ci == pl.num_programs(2) - 1)
    def _():
        if has_prev:
            c1_ref[0, 0] = c_s[...]
        else:
            for l in range(c1_ref.shape[0]):
                c1_ref[l, 0] = c_s[...] if l == layer else jnp.zeros_like(c_s)
        n1_ref[0] = n_s[...]
        m1_ref[0] = m_s[...]


def _mlstm_core(qk, vo, gates, norm_g, bsz, seq, heads, layer, n_layers, state, c_prev, out_dtype):
    inner = qk.shape[1] // 2
    dh = inner // heads
    hps = math.gcd(heads, ML_HEADS_PER_STEP)
    nhg = heads // hps
    lr = math.gcd(seq, ML_CHUNK)
    lp = lr if lr % SEQ_PAD == 0 else SEQ_PAD
    nc = seq // lr
    pad = lp - lr
    gi = gates[:, :heads].reshape(bsz, seq, heads)
    gf = gates[:, heads:2 * heads].reshape(bsz, seq, heads)
    g = jnp.stack([gi, gf], axis=-1)
    g = g.reshape(bsz, nc, lr, heads, 2)
    g = jnp.pad(g, ((0, 0), (0, 0), (pad, 0), (0, 0), (0, 0)))
    gc = jnp.transpose(g, (3, 0, 1, 2, 4)).reshape(heads, bsz, nc * lp, 2)
    gr = jnp.transpose(g, (3, 0, 4, 1, 2)).reshape(heads, bsz, 2, nc * lp)
    has_state = state is not None
    ops = [qk, qk, vo, vo, gc, gr, norm_g.reshape(1, inner)]
    specs = [
        pl.BlockSpec((lr, hps * dh), lambda b, h, c: (b * nc + c, h)),
        pl.BlockSpec((lr, hps * dh), lambda b, h, c: (b * nc + c, nhg + h)),
        pl.BlockSpec((lr, hps * dh), lambda b, h, c: (b * nc + c, h)),
        pl.BlockSpec((lr, hps * dh), lambda b, h, c: (b * nc + c, nhg + h)),
        pl.BlockSpec((hps, 1, lp, 2), lambda b, h, c: (h, b, c, 0)),
        pl.BlockSpec((hps, 1, 2, lp), lambda b, h, c: (h, b, 0, c)),
        pl.BlockSpec((1, hps * dh), lambda b, h, c: (0, h)),
    ]
    if has_state:
        c0, n0, m0 = state
        ops += [c0, n0.reshape(bsz, heads, 1, dh), m0.reshape(bsz, heads, 1, 1)]
        specs += [
            pl.BlockSpec((1, 1, hps, dh, dh), lambda b, h, c: (layer, b, h, 0, 0)),
            pl.BlockSpec((1, hps, 1, dh), lambda b, h, c: (b, h, 0, 0)),
            pl.BlockSpec((1, hps, 1, 1), lambda b, h, c: (b, h, 0, 0)),
        ]
    aliases = {}
    if c_prev is not None:
        aliases[len(ops)] = 1
        ops.append(c_prev)
        specs.append(pl.BlockSpec(memory_space=pl.ANY))
    body = functools.partial(_mlstm_body, lr=lr, lp=lp, has_state=has_state, has_prev=c_prev is not None,
                             dh=dh, hps=hps, layer=layer)
    if c_prev is not None:
        c1_spec = pl.BlockSpec((1, 1, hps, dh, dh), lambda b, h, c: (layer, b, h, 0, 0))
    else:
        c1_spec = pl.BlockSpec((n_layers, 1, hps, dh, dh), lambda b, h, c: (0, b, h, 0, 0))
    hg, c1, n1, m1 = pl.pallas_call(
        body, grid=(bsz, nhg, nc), in_specs=specs, input_output_aliases=aliases,
        out_specs=[
            pl.BlockSpec((lr, hps * dh), lambda b, h, c: (b * nc + c, h)),
            c1_spec,
            pl.BlockSpec((1, hps, 1, dh), lambda b, h, c: (b, h, 0, 0)),
            pl.BlockSpec((1, hps, 1, 1), lambda b, h, c: (b, h, 0, 0)),
        ],
        out_shape=[
            jax.ShapeDtypeStruct((bsz * seq, inner), out_dtype),
            jax.ShapeDtypeStruct((n_layers, bsz, heads, dh, dh), F32),
            jax.ShapeDtypeStruct((bsz, heads, 1, dh), F32),
            jax.ShapeDtypeStruct((bsz, heads, 1, 1), F32),
        ],
        scratch_shapes=[pltpu.VMEM((hps, dh, dh), F32), pltpu.VMEM((hps, 1, dh), F32),
                        pltpu.VMEM((hps, 1, 1), F32)],
        compiler_params=_params(("parallel", "parallel", "arbitrary")), name="mlstm_core",
    )(*ops)
    return hg, c1, n1.reshape(bsz, heads, dh), m1.reshape(bsz, heads)


def _ssd_group(x, bm, cm, z, dtc, dtr, pr, pc, ng, h, *, lp, pad, hpg, hd):
    dd_c = _softplus(dtc + pr[0:1, :])
    dd_r = _softplus(dtr + pc[:, 0:1])
    if pad:
        dd_c = jnp.where(lax.broadcasted_iota(jnp.int32, (lp, hpg), 0) >= pad, dd_c, 0.0)
        dd_r = jnp.where(lax.broadcasted_iota(jnp.int32, (hpg, lp), 1) >= pad, dd_r, 0.0)
    cum_c = _cumsum_rows(dd_c * pr[1:2, :])
    cum_r = _cumsum_lanes(dd_r * pc[:, 1:2])
    cb = _dot_nt(cm, bm)
    t_i = lax.broadcasted_iota(jnp.int32, (lp, lp), 0)
    s_i = lax.broadcasted_iota(jnp.int32, (lp, lp), 1)
    tri = s_i <= t_i
    head_of_lane = lax.broadcasted_iota(jnp.int32, (1, hpg * hd), 1) // hd
    y = jnp.zeros((lp, hpg * hd), F32)
    for r in range(hpg):
        seg = jnp.where(tri, cum_c[:, r:r + 1] - cum_r[r:r + 1, :], NEG)
        w = jnp.exp(seg) * cb * dd_r[r:r + 1, :]
        xr = jnp.where(head_of_lane == r, x, 0.0).astype(BF16)
        y = y + _dot(w.astype(BF16), xr)
    y = y + _expand_lanes(jnp.exp(cum_c), hd) * _dot_nt(cm, h.astype(BF16))
    y = y + _expand_lanes(pr[2:3, :], hd) * x
    yv = y[pad:, :] * (z * jax.nn.sigmoid(z))
    yv = yv * lax.rsqrt(jnp.mean(yv * yv, axis=1, keepdims=True) + NORM_EPS) * ng
    w_end = jnp.exp(cum_c[lp - 1:lp, :] - cum_c) * dd_c
    xw = x * _expand_lanes(w_end, hd)
    decay = _expand_rows(jnp.exp(cum_r[:, lp - 1:lp]), hd)
    return yv, decay * h + _dot_tn(xw.astype(BF16), bm)


def _ssd_body(*refs, lr, lp, has_state, hpg, hd, gps):
    if has_state:
        (x_ref, bm_ref, cm_ref, z_ref, dtc_ref, dtr_ref, pr_ref, pc_ref, ng_ref, h0_ref,
         y_ref, h1_ref, h_s) = refs
    else:
        (x_ref, bm_ref, cm_ref, z_ref, dtc_ref, dtr_ref, pr_ref, pc_ref, ng_ref,
         y_ref, h1_ref, h_s) = refs
    ci = pl.program_id(2)
    pad = lp - lr
    gw = hpg * hd
    ns = bm_ref.shape[1] // gps

    @pl.when(ci == 0)
    def _():
        if has_state:
            h_s[...] = h0_ref[0]
        else:
            h_s[...] = jnp.zeros_like(h_s)

    args = []
    for gi in range(gps):
        xs = slice(gi * gw, (gi + 1) * gw)
        bs = slice(gi * ns, (gi + 1) * ns)
        args.append((_pad_front(x_ref[:, xs], pad), _pad_front(bm_ref[:, bs], pad).astype(BF16),
                     _pad_front(cm_ref[:, bs], pad).astype(BF16), z_ref[:, xs], dtc_ref[gi, 0], dtr_ref[gi, 0],
                     pr_ref[gi], pc_ref[gi], ng_ref[:, xs], h_s[gi]))
    outs = [_ssd_group(*a, lp=lp, pad=pad, hpg=hpg, hd=hd) for a in args]
    for gi, (yv, h_new) in enumerate(outs):
        y_ref[:, gi * gw:(gi + 1) * gw] = yv.astype(y_ref.dtype)
        h_s[gi] = h_new

    @pl.when(ci == pl.num_programs(2) - 1)
    def _():
        h1_ref[0] = h_s[...]


def _ssd_core(zx, xbc, dt_raw, dt_bias, a_neg, d_skip, norm_g, bsz, seq, groups, nstate, hd, h0, out_dtype):
    inner = norm_g.shape[0]
    heads = inner // hd
    hpg = heads // groups
    gw = hpg * hd
    gps = math.gcd(groups, SSD_GROUPS_PER_STEP)
    ngs = groups // gps
    lr = math.gcd(seq, SSD_CHUNK)
    lp = lr if lr % SEQ_PAD == 0 else SEQ_PAD
    nc = seq // lr
    pad = lp - lr
    d = dt_raw[:, :heads].reshape(bsz, nc, lr, groups, hpg)
    d = jnp.pad(d, ((0, 0), (0, 0), (pad, 0), (0, 0), (0, 0)))
    dtc = jnp.transpose(d, (3, 0, 1, 2, 4)).reshape(groups, bsz, nc * lp, hpg)
    dtr = jnp.transpose(d, (3, 0, 4, 1, 2)).reshape(groups, bsz, hpg, nc * lp)
    par = jnp.stack([dt_bias, a_neg, d_skip]).astype(F32).reshape(3, groups, hpg)
    pr = jnp.transpose(par, (1, 0, 2))
    pc = jnp.transpose(par, (1, 2, 0))
    b_off = inner // (gps * nstate)
    c_off = b_off + ngs
    has_state = h0 is not None
    ops = [xbc, xbc, xbc, zx, dtc, dtr, pr, pc, norm_g.reshape(1, inner)]
    specs = [
        pl.BlockSpec((lr, gps * gw), lambda b, g, c: (b * nc + c, g)),
        pl.BlockSpec((lr, gps * nstate), lambda b, g, c: (b * nc + c, b_off + g)),
        pl.BlockSpec((lr, gps * nstate), lambda b, g, c: (b * nc + c, c_off + g)),
        pl.BlockSpec((lr, gps * gw), lambda b, g, c: (b * nc + c, g)),
        pl.BlockSpec((gps, 1, lp, hpg), lambda b, g, c: (g, b, c, 0)),
        pl.BlockSpec((gps, 1, hpg, lp), lambda b, g, c: (g, b, 0, c)),
        pl.BlockSpec((gps, 3, hpg), lambda b, g, c: (g, 0, 0)),
        pl.BlockSpec((gps, hpg, 3), lambda b, g, c: (g, 0, 0)),
        pl.BlockSpec((1, gps * gw), lambda b, g, c: (0, g)),
    ]
    if has_state:
        ops.append(h0.reshape(bsz, groups, gw, nstate))
        specs.append(pl.BlockSpec((1, gps, gw, nstate), lambda b, g, c: (b, g, 0, 0)))
    body = functools.partial(_ssd_body, lr=lr, lp=lp, has_state=has_state, hpg=hpg, hd=hd, gps=gps)
    y, h1 = pl.pallas_call(
        body, grid=(bsz, ngs, nc), in_specs=specs,
        out_specs=[
            pl.BlockSpec((lr, gps * gw), lambda b, g, c: (b * nc + c, g)),
            pl.BlockSpec((1, gps, gw, nstate), lambda b, g, c: (b, g, 0, 0)),
        ],
        out_shape=[
            jax.ShapeDtypeStruct((bsz * seq, inner), out_dtype),
            jax.ShapeDtypeStruct((bsz, groups, gw, nstate), F32),
        ],
        scratch_shapes=[pltpu.VMEM((gps, gw, nstate), F32)],
        compiler_params=_params(("parallel", "parallel", "arbitrary")), name="ssd_core",
    )(*ops)
    return y, h1.reshape(bsz, heads, hd, nstate)


def _sb_logs2(w):
    t = jnp.log2(1.0 + jnp.exp2(-jnp.abs(w)))
    lb = jnp.minimum(w, 0.0) - t
    return lb, lb - w


def _sbp_body(bias_ref, q_ref, k_ref, v_ref, o_ref, *, blk, hd, hps):
    hg = pl.program_id(1)
    qi = pl.program_id(2)
    scale = hd ** -0.5
    bias = [bias_ref[hg * hps + u] * LOG2E for u in range(hps)]
    qb = [q_ref[0, :, u * hd:(u + 1) * hd].astype(BF16) for u in range(hps)]
    j_i = lax.broadcasted_iota(jnp.int32, (2 * blk, blk), 0)
    s_i = lax.broadcasted_iota(jnp.int32, (2 * blk, blk), 1)
    tri2 = jnp.where((j_i % blk) > s_i, 1.0, 0.0).astype(BF16)
    t_i = lax.broadcasted_iota(jnp.int32, (blk, blk), 0)
    strict = lax.broadcasted_iota(jnp.int32, (blk, blk), 1) < t_i

    def rows(kb):
        return pl.ds(pl.multiple_of(kb * blk, blk), blk)

    def scores(kb, u):
        return _dot_nt(qb[u], k_ref[0, rows(kb), u * hd:(u + 1) * hd].astype(BF16))

    def weights(s, run, u, mask):
        lb, lr = _sb_logs2(s * (scale * LOG2E) + bias[u])
        if mask is not None:
            lr = jnp.where(mask, lr, 0.0)
        hi, lo = _split2(lr)
        local = _dot(jnp.concatenate([hi, lo], axis=1), tri2)
        a = jnp.exp2(lb + (run + local))
        if mask is not None:
            a = jnp.where(mask, a, 0.0)
        return a.astype(BF16), run + (local[:, 0:1] + lr[:, 0:1])

    def weighted_values(a, kb, u):
        return _dot(a, v_ref[0, rows(kb), u * hd:(u + 1) * hd].astype(BF16))

    def step(j, carry):
        kb = qi - 1 - j
        out = []
        for u in range(hps):
            run, acc, s, a_prev = carry[u]
            s_next = scores(jnp.maximum(kb - 1, 0), u)
            acc = acc + weighted_values(a_prev, kb + 1, u)
            a, run = weights(s, run, u, None)
            out.append((run, acc, s_next, a))
        return tuple(out)

    carry = []
    for u in range(hps):
        a, run = weights(scores(qi, u), jnp.zeros((blk, 1), F32), u, strict)
        carry.append((run, jnp.zeros((blk, hd), F32), scores(jnp.maximum(qi - 1, 0), u), a))
    carry = lax.fori_loop(0, qi, step, tuple(carry))
    for u in range(hps):
        acc = carry[u][1] + weighted_values(carry[u][3], 0, u)
        o_ref[:, u * hd:(u + 1) * hd] = acc.astype(o_ref.dtype)


def _sb_prompt(qkv, bias, bsz, seq, heads, hd, out_dtype):
    blk = math.gcd(seq, SB_BLOCK)
    nq = seq // blk
    hps = math.gcd(heads, SB_HEADS_PER_STEP)
    nhg = heads // hps
    body = functools.partial(_sbp_body, blk=blk, hd=hd, hps=hps)
    return pl.pallas_call(
        body, grid=(bsz, nhg, nq),
        in_specs=[
            pl.BlockSpec(memory_space=pltpu.SMEM),
            pl.BlockSpec((1, blk, hps * hd), lambda b, h, i: (0, b * nq + i, h)),
            pl.BlockSpec((1, seq, hps * hd), lambda b, h, i: (1, b, h)),
            pl.BlockSpec((1, seq, hps * hd), lambda b, h, i: (2, b, h)),
        ],
        out_specs=pl.BlockSpec((blk, hps * hd), lambda b, h, i: (b * nq + i, h)),
        out_shape=jax.ShapeDtypeStruct((bsz * seq, heads * hd), out_dtype),
        compiler_params=_params(("parallel", "parallel", "arbitrary")), name="sb_prompt",
    )(bias.astype(F32), qkv, qkv, qkv)


def _sbs_body(pt_ref, q_ref, kn_ref, vn_ref, *rest, sq, heads, hd, page, ppg):
    del pt_ref
    kp_refs, vp_refs = rest[:ppg], rest[ppg:2 * ppg]
    bias_ref, o_ref, run_s, acc_s, qbd_s = rest[2 * ppg:]
    p = pl.program_id(1)
    width = heads * hd
    cols = qbd_s.shape[0]
    scale = hd ** -0.5

    def own_head():
        row_i = lax.broadcasted_iota(jnp.int32, (cols, width), 0)
        lane_i = lax.broadcasted_iota(jnp.int32, (cols, width), 1)
        return (row_i // sq) == (lane_i // hd)

    def process(kks, vvs, mask):
        n = len(kks)
        j_i = lax.broadcasted_iota(jnp.int32, (page, 2 * page), 1)
        s_i = lax.broadcasted_iota(jnp.int32, (page, 2 * page), 0)
        tri2 = jnp.where((j_i % page) > s_i, 1.0, 0.0).astype(BF16)
        kcat = kks[0] if n == 1 else jnp.concatenate(kks, axis=0)
        z = _dot_nt(kcat, qbd_s[...]) * (scale * LOG2E) + bias_ref[...] * LOG2E
        lb, lr = _sb_logs2(z)
        if mask is not None:
            lr = jnp.where(mask, lr, 0.0)
        run = run_s[...]
        parts = []
        for i in range(n):
            rs = slice(i * page, (i + 1) * page)
            hi, lo = _split2(lr[rs])
            local = _dot(tri2, jnp.concatenate([hi, lo], axis=0))
            a = jnp.exp2(lb[rs] + (run + local))
            if mask is not None:
                a = jnp.where(mask, a, 0.0)
            parts.append(a.astype(BF16))
            run = run + (local[0:1, :] + lr[rs][0:1, :])
        run_s[...] = run
        acat = parts[0] if n == 1 else jnp.concatenate(parts, axis=0)
        vcat = vvs[0] if n == 1 else jnp.concatenate(vvs, axis=0)
        acc_s[...] += _dot_tn(acat, vcat)

    def load_page(ref):
        return jnp.concatenate([ref[0, 0, pl.ds(h, page, stride=heads), :] for h in range(heads)],
                               axis=1).astype(BF16)

    @pl.when(p == 0)
    def _():
        run_s[...] = jnp.zeros_like(run_s)
        acc_s[...] = jnp.zeros_like(acc_s)
        q = q_ref[0]
        tiled = jnp.concatenate([q] * heads + [jnp.zeros((cols - heads * sq, width), F32)], axis=0)
        qbd_s[...] = jnp.where(own_head(), tiled, 0.0).astype(BF16)
        key_i = lax.broadcasted_iota(jnp.int32, (page, cols), 0)
        qry_i = lax.broadcasted_iota(jnp.int32, (page, cols), 1) % sq
        zeros = jnp.zeros((page - sq, width), F32)
        process([jnp.concatenate([kn_ref[0], zeros], axis=0).astype(BF16)],
                [jnp.concatenate([vn_ref[0], zeros], axis=0).astype(BF16)], key_i < qry_i)

    @pl.when(p > 0)
    def _():
        process([load_page(r) for r in kp_refs], [load_page(r) for r in vp_refs], None)

    @pl.when(p == pl.num_programs(1) - 1)
    def _():
        acc = jnp.where(own_head(), acc_s[...], 0.0)
        out = acc[0:sq, :]
        for h in range(1, heads):
            out = out + acc[h * sq:(h + 1) * sq, :]
        o_ref[...] = out.astype(o_ref.dtype)


def _sb_sample(qkv, pool_k, pool_v, layer, page_table, bias, bsz, sq, heads, hd, out_dtype):
    width = heads * hd
    n_pages = page_table.shape[1]
    page = pool_k.shape[2]
    cols = SB_COLS
    ppg = math.gcd(n_pages, SB_PAGES_PER_STEP)
    assert heads * sq <= cols and sq <= page
    bias_cols = jnp.pad(jnp.repeat(bias.astype(F32), sq), (0, cols - heads * sq)).reshape(1, cols)
    pool_k = pool_k.reshape(pool_k.shape[0], pool_k.shape[1], page * heads, hd)
    pool_v = pool_v.reshape(pool_v.shape[0], pool_v.shape[1], page * heads, hd)

    def page_spec(i):
        def idx(b, s, pt):
            return (layer, pt[b, n_pages - 1 - (jnp.maximum(s - 1, 0) * ppg + i)], 0, 0)
        return pl.BlockSpec((1, 1, page * heads, hd), idx)

    body = functools.partial(_sbs_body, sq=sq, heads=heads, hd=hd, page=page, ppg=ppg)
    grid_spec = pltpu.PrefetchScalarGridSpec(
        num_scalar_prefetch=1, grid=(bsz, n_pages // ppg + 1),
        in_specs=[
            pl.BlockSpec((1, sq, width), lambda b, s, pt: (0, b, 0)),
            pl.BlockSpec((1, sq, width), lambda b, s, pt: (1, b, 0)),
            pl.BlockSpec((1, sq, width), lambda b, s, pt: (2, b, 0)),
            *[page_spec(i) for i in range(ppg)],
            *[page_spec(i) for i in range(ppg)],
            pl.BlockSpec((1, cols), lambda b, s, pt: (0, 0)),
        ],
        out_specs=pl.BlockSpec((sq, width), lambda b, s, pt: (b, 0)),
        scratch_shapes=[pltpu.VMEM((1, cols), F32), pltpu.VMEM((cols, width), F32),
                        pltpu.VMEM((cols, width), BF16)],
    )
    return pl.pallas_call(
        body, grid_spec=grid_spec,
        out_shape=jax.ShapeDtypeStruct((bsz * sq, width), out_dtype),
        compiler_params=_params(("parallel", "arbitrary")), name="sb_sample",
    )(page_table, qkv, qkv, qkv, *([pool_k] * ppg), *([pool_v] * ppg), bias_cols)


def _xa_body(x_ref, g_ref, wq_ref, mk_ref, mv_ref, wo_ref, o_ref, *, heads, hd):
    scale = hd ** -0.5

    x = x_ref[...]
    xn = (x * lax.rsqrt(jnp.mean(x * x, axis=-1, keepdims=True) + NORM_EPS) * g_ref[...]).astype(BF16)
    q = _dot(xn, wq_ref[...]).astype(BF16)
    outs = []
    for h in range(heads):
        cs = slice(h * hd, (h + 1) * hd)
        s = _dot_nt(q[:, cs], mk_ref[0, :, cs].astype(BF16)) * scale
        e = jnp.exp(s - jnp.max(s, axis=1, keepdims=True))
        pr = e / jnp.sum(e, axis=1, keepdims=True)
        outs.append(_dot(pr.astype(BF16), mv_ref[0, :, cs].astype(BF16)).astype(BF16))
    o_ref[...] = x + _dot(jnp.concatenate(outs, axis=1), wo_ref[...])


def _xa_cache_body(x_ref, g_ref, wq_ref, wo_ref, ck_hbm, cv_hbm, o_ref, kbuf, vbuf, sem, *, layer, heads, hd, bps, seq):
    step = pl.program_id(0)
    slot = step % 2

    def copies(st, sl):
        out = []
        for r in range(bps):
            for h in range(heads):
                src = (layer, st * bps + r, slice(None), h, slice(None))
                out.append(pltpu.make_async_copy(ck_hbm.at[src], kbuf.at[sl, r, h], sem.at[sl, 0, r, h]))
                out.append(pltpu.make_async_copy(cv_hbm.at[src], vbuf.at[sl, r, h], sem.at[sl, 1, r, h]))
        return out

    @pl.when(step == 0)
    def _():
        for c in copies(0, 0):
            c.start()

    @pl.when(step + 1 < pl.num_programs(0))
    def _():
        for c in copies(step + 1, 1 - slot):
            c.start()

    for c in copies(step, slot):
        c.wait()

    scale = hd ** -0.5
    x = x_ref[...]
    xn = (x * lax.rsqrt(jnp.mean(x * x, axis=-1, keepdims=True) + NORM_EPS) * g_ref[...]).astype(BF16)
    q = _dot(xn, wq_ref[...]).astype(BF16)
    rows = []
    for r in range(bps):
        outs = []
        for h in range(heads):
            s = _dot_nt(q[r * seq:(r + 1) * seq, h * hd:(h + 1) * hd], kbuf[slot, r, h].astype(BF16)) * scale
            e = jnp.exp(s - jnp.max(s, axis=1, keepdims=True))
            pr = e / jnp.sum(e, axis=1, keepdims=True)
            outs.append(_dot(pr.astype(BF16), vbuf[slot, r, h].astype(BF16)))
        rows.append(jnp.concatenate(outs, axis=1))
    o_ref[...] = x + _dot(jnp.concatenate(rows, axis=0).astype(BF16), wo_ref[...])


def _cross_attn_cached(x, g, wq, cache_k, cache_v, layer, wo, bsz, seq, mem_len, heads):
    d = x.shape[1]
    hd = d // heads
    bps = math.gcd(bsz, XA_ROWS_PER_STEP)
    body = functools.partial(_xa_cache_body, layer=layer, heads=heads, hd=hd, bps=bps, seq=seq)
    return pl.pallas_call(
        body, grid=(bsz // bps,),
        in_specs=[
            pl.BlockSpec((bps * seq, d), lambda b: (b, 0)),
            pl.BlockSpec((1, d), lambda b: (0, 0)),
            pl.BlockSpec((d, d), lambda b: (0, 0)),
            pl.BlockSpec((d, d), lambda b: (0, 0)),
            pl.BlockSpec(memory_space=pl.ANY),
            pl.BlockSpec(memory_space=pl.ANY),
        ],
        out_specs=pl.BlockSpec((bps * seq, d), lambda b: (b, 0)),
        out_shape=jax.ShapeDtypeStruct((bsz * seq, d), F32),
        scratch_shapes=[pltpu.VMEM((2, bps, heads, mem_len, hd), cache_k.dtype),
                        pltpu.VMEM((2, bps, heads, mem_len, hd), cache_v.dtype),
                        pltpu.SemaphoreType.DMA((2, 2, bps, heads))],
        compiler_params=_params(("arbitrary",)), name="cross_attn_cached",
    )(x, g.reshape(1, d), wq, wo, cache_k, cache_v)


def _cross_attn_layer(x, g, wq, mk, mv, mk_idx, mv_idx, wo, bsz, seq, mem_len, heads):
    d = x.shape[1]
    tq = math.gcd(seq, XA_QBLOCK)
    nq = seq // tq
    body = functools.partial(_xa_body, heads=heads, hd=d // heads)

    return pl.pallas_call(
        body, grid=(bsz, nq),
        in_specs=[
            pl.BlockSpec((tq, d), lambda b, i: (b * nq + i, 0)),
            pl.BlockSpec((1, d), lambda b, i: (0, 0)),
            pl.BlockSpec((d, d), lambda b, i: (0, 0)),
            pl.BlockSpec((1, mem_len, d), lambda b, i: (mk_idx, b, 0)),
            pl.BlockSpec((1, mem_len, d), lambda b, i: (mv_idx, b, 0)),
            pl.BlockSpec((d, d), lambda b, i: (0, 0)),
        ],
        out_specs=pl.BlockSpec((tq, d), lambda b, i: (b * nq + i, 0)),
        out_shape=jax.ShapeDtypeStruct((bsz * seq, d), F32),
        compiler_params=_params(("parallel", "parallel")), name="cross_attn",
    )(x, g.reshape(1, d), wq, mk, mv, wo)


def _prep_weights(p):
    ml_inner = p['ml_w_up'].shape[2]
    ssd_main = p['ssd_w_in'].shape[2] - p['ssd_dt_bias'].shape[1]
    ffn_hidden = p['ffn_w_out'].shape[1]

    def per_layer(a, lo=None, hi=None, pad_to=None, dtype=BF16):
        out = []
        for j in range(a.shape[0]):
            m = a[j, :, lo:hi] if a.ndim == 3 else a[j, lo:hi]
            if pad_to is not None:
                m = jnp.pad(m, [(0, 0)] * (m.ndim - 1) + [(0, pad_to - m.shape[-1])])
            out.append(m.astype(dtype))
        return out

    return {
        'ml_w_up': per_layer(p['ml_w_up']),
        'ml_w_qk': per_layer(p['ml_w_qk']),
        'ml_w_vo': per_layer(p['ml_w_vog'], 0, 2 * ml_inner),
        'ml_w_g': per_layer(p['ml_w_vog'], 2 * ml_inner, None, LANES),
        'ml_b_g': per_layer(p['ml_b_gate'], None, None, LANES, F32),
        'ml_w_down': per_layer(p['ml_w_down']),
        'sb_w_qkv': per_layer(p['sb_w_qkv']),
        'sb_w_o': per_layer(p['sb_w_o']),
        'ssd_w_main': per_layer(p['ssd_w_in'], 0, ssd_main),
        'ssd_w_dt': per_layer(p['ssd_w_in'], ssd_main, None, LANES),
        'ssd_w_out': per_layer(p['ssd_w_out']),
        'ssd_a': -jnp.exp(p['ssd_a_log'].astype(F32)),
        'xa_w_q': per_layer(p['xa_w_q']),
        'xa_w_kv': per_layer(p['xa_w_kv']),
        'xa_w_o': per_layer(p['xa_w_o']),
        'ffn_w_gate': per_layer(p['ffn_w_in'], 0, ffn_hidden),
        'ffn_w_up': per_layer(p['ffn_w_in'], ffn_hidden, None),
        'ffn_w_out': per_layer(p['ffn_w_out']),
    }


def _trunk(x3, mem_kv, ml_states, sb_pools, ssd_states, page_table, p, w, dims, act_dtype):
    bsz, seq, d = x3.shape
    depth = p['norm_mix'].shape[0]
    ml_heads, sb_heads, xa_heads, ssd_groups, ssd_state, ssd_hd, mem_len = dims
    x = x3.reshape(bsz * seq, d)
    ml_new, sb_new, ssd_new = [], [], []
    n_ml = n_sb = n_ssd = 0
    n_ml_layers = p['ml_w_up'].shape[0]
    ml_c = None
    for layer in range(depth):
        kind = layer % 3
        if kind == 0:
            j = n_ml
            n_ml += 1
            inner = p['ml_w_up'].shape[2]
            width = p['ml_conv_w'].shape[1]
            xm = _matmul(x, w['ml_w_up'][j], g=p['norm_mix'][layer])
            st = None if ml_states is None else ml_states[j]
            buf = None if st is None else st[3]
            xc = _causal_conv_silu(xm, 0, inner, buf, p['ml_conv_w'][j], p['ml_conv_b'][j], bsz, seq, act_dtype)
            qk = _matmul(xc, w['ml_w_qk'][j])
            vo, gates = _matmul(xm, w['ml_w_vo'][j], extra=(w['ml_w_g'][j], w['ml_b_g'][j]))
            hg, ml_c, n1, m1 = _mlstm_core(qk, vo, gates, p['ml_norm'][j], bsz, seq, ml_heads, j, n_ml_layers,
                                           None if st is None else st[:3], ml_c, act_dtype)
            x = _matmul(hg, w['ml_w_down'][j], res=x)
            ml_new.append((n1, m1, _conv_state(xm, 0, inner, buf, bsz, seq, width)))
        elif kind == 1:
            j = n_sb
            n_sb += 1
            hd = d // sb_heads
            qkv = _matmul(x, w['sb_w_qkv'][j], g=p['norm_mix'][layer], splits=3)
            if sb_pools is None:
                o = _sb_prompt(qkv, p['sb_bias'][j], bsz, seq, sb_heads, hd, act_dtype)
            else:
                o = _sb_sample(qkv, sb_pools[0], sb_pools[1], j, page_table, p['sb_bias'][j],
                               bsz, seq, sb_heads, hd, act_dtype)
            x = _matmul(o, w['sb_w_o'][j], res=x)
            shp = (bsz, seq, sb_heads, hd)
            sb_new.append((qkv[1].reshape(shp), qkv[2].reshape(shp)))
        else:
            j = n_ssd
            n_ssd += 1
            inner = p['ssd_norm'].shape[1]
            conv_ch = p['ssd_conv_w'].shape[2]
            width = p['ssd_conv_w'].shape[1]
            zx, dt_raw = _matmul(x, w['ssd_w_main'][j], g=p['norm_mix'][layer],
                                 extra=(w['ssd_w_dt'][j], jnp.zeros((LANES,), F32)))
            st = None if ssd_states is None else ssd_states[j]
            buf = None if st is None else st[1]
            xbc = _causal_conv_silu(zx, inner, conv_ch, buf, p['ssd_conv_w'][j], p['ssd_conv_b'][j],
                                    bsz, seq, F32)
            y, h1 = _ssd_core(zx, xbc, dt_raw, p['ssd_dt_bias'][j], w['ssd_a'][j],
                              p['ssd_d'][j], p['ssd_norm'][j], bsz, seq, ssd_groups, ssd_state, ssd_hd,
                              None if st is None else st[0], act_dtype)
            x = _matmul(y, w['ssd_w_out'][j], res=x)
            ssd_new.append((h1, _conv_state(zx, inner, conv_ch, buf, bsz, seq, width)))
        mk, mv, mk_idx, mv_idx = mem_kv[layer]
        if mk.ndim == 5:
            x = _cross_attn_cached(x, p['norm_xa'][layer], w['xa_w_q'][layer], mk, mv, mk_idx,
                                   w['xa_w_o'][layer], bsz, seq, mem_len, xa_heads)
        else:
            x = _cross_attn_layer(x, p['norm_xa'][layer], w['xa_w_q'][layer], mk, mv, mk_idx, mv_idx,
                                  w['xa_w_o'][layer], bsz, seq, mem_len, xa_heads)
        hid = _matmul(x, w['ffn_w_gate'][layer], g=p['norm_ffn'][layer], w2=w['ffn_w_up'][layer],
                      out_dtype=act_dtype)
        x = _matmul(hid, w['ffn_w_out'][layer], res=x)
    y = _final_norm(x, p['norm_final'])
    return y.reshape(bsz, seq, d), ml_c, ml_new, sb_new, ssd_new


def _norm_body(x_ref, g_ref, o_ref):
    x = x_ref[...]
    o_ref[...] = x * lax.rsqrt(jnp.mean(x * x, axis=-1, keepdims=True) + NORM_EPS) * g_ref[...]


def _final_norm(x, g):
    t, d = x.shape
    tm = min(ROW_TILE, t)
    return pl.pallas_call(
        _norm_body, grid=(t // tm,),
        in_specs=[pl.BlockSpec((tm, d), lambda i: (i, 0)), pl.BlockSpec((1, d), lambda i: (0, 0))],
        out_specs=pl.BlockSpec((tm, d), lambda i: (i, 0)),
        out_shape=jax.ShapeDtypeStruct((t, d), F32),
        compiler_params=_params(("parallel",)), name="final_norm",
    )(x, g.reshape(1, d))


def _stack(items, idx):
    return jnp.stack([it[idx] for it in items])


def kernel(x_prompt, x_sample, cache_mem_k, cache_mem_v, cache_sb_k, cache_sb_v, state_ml_c, state_ml_n, state_ml_m, state_ml_conv, state_ssd_h, state_ssd_conv, page_table, mem_prompt, norm_mix, norm_xa, norm_ffn, norm_mem, norm_final, ml_w_up, ml_conv_w, ml_conv_b, ml_w_qk, ml_w_vog, ml_b_gate, ml_norm, ml_w_down, sb_w_qkv, sb_bias, sb_w_o, ssd_w_in, ssd_conv_w, ssd_conv_b, ssd_dt_bias, ssd_a_log, ssd_d, ssd_norm, ssd_w_out, xa_w_q, xa_w_kv, xa_w_o, ffn_w_in, ffn_w_out):
    p = dict(norm_mix=norm_mix, norm_xa=norm_xa, norm_ffn=norm_ffn, norm_final=norm_final,
             ml_w_up=ml_w_up, ml_conv_w=ml_conv_w, ml_conv_b=ml_conv_b, ml_w_qk=ml_w_qk,
             ml_w_vog=ml_w_vog, ml_b_gate=ml_b_gate, ml_norm=ml_norm, ml_w_down=ml_w_down,
             sb_w_qkv=sb_w_qkv, sb_bias=sb_bias, sb_w_o=sb_w_o,
             ssd_w_in=ssd_w_in, ssd_conv_w=ssd_conv_w, ssd_conv_b=ssd_conv_b, ssd_dt_bias=ssd_dt_bias,
             ssd_a_log=ssd_a_log, ssd_d=ssd_d, ssd_norm=ssd_norm, ssd_w_out=ssd_w_out,
             xa_w_q=xa_w_q, xa_w_kv=xa_w_kv, xa_w_o=xa_w_o, ffn_w_in=ffn_w_in, ffn_w_out=ffn_w_out)
    w = _prep_weights(p)
    depth = norm_mix.shape[0]
    d = x_prompt.shape[2]
    bp, mem_len = mem_prompt.shape[0], mem_prompt.shape[1]
    ml_heads = state_ml_c.shape[2]
    sb_heads = cache_sb_k.shape[3]
    xa_heads = cache_mem_k.shape[3]
    ssd_hd, ssd_state = state_ssd_h.shape[3], state_ssd_h.shape[4]
    ssd_groups = (ssd_conv_w.shape[2] - ssd_norm.shape[1]) // (2 * ssd_state)
    n_ml, n_ssd = state_ml_c.shape[0], state_ssd_h.shape[0]
    dims = (ml_heads, sb_heads, xa_heads, ssd_groups, ssd_state, ssd_hd, mem_len)

    mem2 = mem_prompt.reshape(bp * mem_len, d)
    mem_kv_p = [_matmul(mem2, w['xa_w_kv'][l], g=norm_mem[l], splits=2) for l in range(depth)]
    y_prompt, ml_c_p, ml_p, sb_p, ssd_p = _trunk(x_prompt, [(kv, kv, 0, 1) for kv in mem_kv_p], None, None, None,
                                         None, p, w, dims, BF16)
    shp = (bp, mem_len, xa_heads, d // xa_heads)
    mem_k_p = jnp.stack([kv[0].reshape(shp) for kv in mem_kv_p])
    mem_v_p = jnp.stack([kv[1].reshape(shp) for kv in mem_kv_p])

    mem_kv_s = [(cache_mem_k, cache_mem_v, l, l) for l in range(depth)]
    ml_cache = [(state_ml_c, state_ml_n[j], state_ml_m[j], state_ml_conv[j]) for j in range(n_ml)]
    ssd_cache = [(state_ssd_h[j], state_ssd_conv[j]) for j in range(n_ssd)]
    y_sample, ml_c_s, ml_s, sb_s, ssd_s = _trunk(x_sample, mem_kv_s, ml_cache, (cache_sb_k, cache_sb_v), ssd_cache,
                                         page_table, p, w, dims, F32)

    return (y_prompt, y_sample, mem_k_p, mem_v_p,
            _stack(sb_p, 0), _stack(sb_p, 1),
            ml_c_p, _stack(ml_p, 0), _stack(ml_p, 1), _stack(ml_p, 2),
            _stack(ssd_p, 0), _stack(ssd_p, 1),
            _stack(sb_s, 0), _stack(sb_s, 1),
            ml_c_s, _stack(ml_s, 0), _stack(ml_s, 1), _stack(ml_s, 2),
            _stack(ssd_s, 0), _stack(ssd_s, 1))
```

```python
import functools
import math

import jax
import jax.numpy as jnp
from jax import lax
from jax.experimental import pallas as pl
from jax.experimental.pallas import tpu as pltpu

F32 = jnp.float32
BF16 = jnp.bfloat16
NORM_EPS = 1e-6
NEG = -1e30
LOG2E = 1.4426950408889634
SUBLANES = 8
VMEM_LIMIT_BYTES = 56 * 1024 * 1024
LANES = 128
MM_VMEM_BUDGET_BYTES = 44 * 1024 * 1024
MXU_FLOPS = 1.0e15
HBM_BYTES_PER_S = 3.0e12
STEP_OVERHEAD_S = 0.35e-6
ROW_TILE = 1024
ML_HEADS_PER_STEP = 4
SEQ_PAD = 16
SB_COLS = 128
ML_CHUNK = 256
SSD_CHUNK = 128
SSD_GROUPS_PER_STEP = 4
SB_BLOCK = 256
SB_HEADS_PER_STEP = 4
SB_PAGES_PER_STEP = 16
XA_QBLOCK = 1024
XA_ROWS_PER_STEP = 4
CONV_COLS = 2048
CONV_ROWS = 512


def _params(sem):
    return pltpu.CompilerParams(dimension_semantics=sem, vmem_limit_bytes=VMEM_LIMIT_BYTES)


def _dot(a, b):
    return jnp.dot(a, b, preferred_element_type=F32)


def _dot_nt(a, b):
    return lax.dot_general(a, b, (((1,), (1,)), ((), ())), preferred_element_type=F32)


def _dot_tn(a, b):
    return lax.dot_general(a, b, (((0,), (0,)), ((), ())), preferred_element_type=F32)


def _split3(x):
    hi = x.astype(BF16)
    r = x - hi.astype(F32)
    mid = r.astype(BF16)
    lo = (r - mid.astype(F32)).astype(BF16)
    return hi, mid, lo


def _split2(x):
    hi = x.astype(BF16)
    lo = (x - hi.astype(F32)).astype(BF16)
    return hi, lo


def _cumsum_rows(x):
    n = x.shape[0]
    t = lax.broadcasted_iota(jnp.int32, (n, n), 0)
    s = lax.broadcasted_iota(jnp.int32, (n, n), 1)
    m = jnp.where(s <= t, 1.0, 0.0).astype(BF16)
    hi, mid, lo = _split3(x)
    return _dot(m, hi) + _dot(m, mid) + _dot(m, lo)


def _cumsum_lanes(x):
    n = x.shape[1]
    j = lax.broadcasted_iota(jnp.int32, (n, n), 0)
    s = lax.broadcasted_iota(jnp.int32, (n, n), 1)
    m = jnp.where(j <= s, 1.0, 0.0).astype(BF16)
    hi, mid, lo = _split3(x)
    return _dot(hi, m) + _dot(mid, m) + _dot(lo, m)


def _softplus(x):
    return jnp.maximum(x, 0.0) + jnp.log1p(jnp.exp(-jnp.abs(x)))


def _pad_front(x, pad):
    if pad == 0:
        return x
    return jnp.concatenate([jnp.zeros((pad, x.shape[1]), x.dtype), x], axis=0)


def _expand_lanes(cols, width):
    n, k = cols.shape
    lane = lax.broadcasted_iota(jnp.int32, (n, k * width), 1)
    out = jnp.broadcast_to(cols[:, k - 1:k], (n, k * width))
    for r in range(k - 2, -1, -1):
        out = jnp.where(lane < (r + 1) * width, cols[:, r:r + 1], out)
    return out


def _expand_rows(rows, height):
    k, n = rows.shape
    sub = lax.broadcasted_iota(jnp.int32, (k * height, n), 0)
    out = jnp.broadcast_to(rows[k - 1:k, :], (k * height, n))
    for r in range(k - 2, -1, -1):
        out = jnp.where(sub < (r + 1) * height, rows[r:r + 1, :], out)
    return out


def _mm_body(*refs, has_norm, swiglu, has_res, has_extra):
    it = iter(refs)
    x_ref = next(it)
    g_ref = next(it) if has_norm else None
    w_ref = next(it)
    w2_ref = next(it) if swiglu else None
    r_ref = next(it) if has_res else None
    we_ref, be_ref = (next(it), next(it)) if has_extra else (None, None)
    o_ref = next(it)
    x = x_ref[...]
    if has_norm:
        x32 = x.astype(F32)
        x = x32 * lax.rsqrt(jnp.mean(x32 * x32, axis=-1, keepdims=True) + NORM_EPS) * g_ref[...]
    xb = x.astype(BF16)
    acc = _dot(xb, w_ref[...])
    if swiglu:
        acc = acc * jax.nn.sigmoid(acc) * _dot(xb, w2_ref[...])
    if has_res:
        acc = r_ref[...] + acc
    o_ref[...] = acc.reshape(o_ref.shape).astype(o_ref.dtype)
    if has_extra:
        next(it)[0] = _dot(xb, we_ref[...]) + be_ref[...]


def _mm_tiles(t, k, ncols, x_bytes, out_bytes, n_w, has_res, n_passes):
    best = None
    for tm in sorted({min(t, m) for m in (256, 512, 1024, 2048)}):
        if t % tm:
            continue
        for d in range(1, ncols // LANES + 1):
            tn = ncols // d
            if ncols % d or tn % LANES:
                continue
            vmem = 2 * (tm * k * x_bytes + k * tn * 2 * n_w + tm * tn * (out_bytes + 4 * has_res)
                        + tm * LANES * 4 + k * LANES * 2)
            if vmem > MM_VMEM_BUDGET_BYTES:
                continue
            col_tiles = d * n_passes
            steps = col_tiles * (t // tm)
            hbm = col_tiles * t * k * x_bytes + t * ncols * n_passes * (out_bytes + 4 * has_res)
            cost = (max(2.0 * t * k * ncols * n_passes * n_w / MXU_FLOPS, hbm / HBM_BYTES_PER_S)
                    + steps * STEP_OVERHEAD_S)
            if best is None or cost < best[0]:
                best = (cost, tm, tn)
    assert best is not None
    return best[1], best[2]


def _matmul(x, w, *, g=None, w2=None, res=None, extra=None, splits=1, out_dtype=F32):
    t, k = x.shape
    n = w.shape[1]
    ncols = n // splits
    tm, tn = _mm_tiles(t, k, ncols, x.dtype.itemsize, jnp.dtype(out_dtype).itemsize,
                       2 if w2 is not None else 1, res is not None, splits)
    cps = ncols // tn
    grid = (n // tn, t // tm)
    ops = [x]
    specs = [pl.BlockSpec((tm, k), lambda j, i: (i, 0))]
    if g is not None:
        ops.append(g.reshape(1, k))
        specs.append(pl.BlockSpec((1, k), lambda j, i: (0, 0)))
    ops.append(w)
    specs.append(pl.BlockSpec((k, tn), lambda j, i: (0, j)))
    if w2 is not None:
        ops.append(w2)
        specs.append(pl.BlockSpec((k, tn), lambda j, i: (0, j)))
    if res is not None:
        ops.append(res)
        specs.append(pl.BlockSpec((tm, tn), lambda j, i: (i, j)))
    if extra is not None:
        ops += [extra[0], extra[1].reshape(1, LANES)]
        specs += [pl.BlockSpec((k, LANES), lambda j, i: (0, 0)), pl.BlockSpec((1, LANES), lambda j, i: (0, 0))]
    if splits == 1:
        out_specs = [pl.BlockSpec((tm, tn), lambda j, i: (i, j))]
        out_shape = [jax.ShapeDtypeStruct((t, n), out_dtype)]
    else:
        out_specs = [pl.BlockSpec((1, tm, tn), lambda j, i: (j // cps, i, j % cps))]
        out_shape = [jax.ShapeDtypeStruct((splits, t, ncols), out_dtype)]
    if extra is not None:
        out_specs.append(pl.BlockSpec((1, tm, LANES), lambda j, i: (j, i, 0)))
        out_shape.append(jax.ShapeDtypeStruct((grid[0], t, LANES), F32))
    body = functools.partial(_mm_body, has_norm=g is not None, swiglu=w2 is not None,
                             has_res=res is not None, has_extra=extra is not None)
    out = pl.pallas_call(
        body, grid=grid, in_specs=specs, out_specs=out_specs, out_shape=out_shape,
        compiler_params=_params(("arbitrary", "arbitrary")), name="matmul",
    )(*ops)
    return (out[0], out[1][0]) if extra is not None else out[0]


def _conv_body(x_ref, prev_ref, buf_ref, w_ref, b_ref, o_ref, *, width):
    ts = x_ref.shape[0]
    x = x_ref[...]
    hist = jnp.where(pl.program_id(1) == 0, buf_ref[0], prev_ref[...])
    xx = jnp.concatenate([hist, x], axis=0)
    w = w_ref[...]
    y = b_ref[...]
    for j in range(width):
        k = width - 1 - j
        win = x if k == 0 else pltpu.roll(xx, k, 0)[SUBLANES:, :]
        y = y + win * w[j:j + 1, :]
    o_ref[...] = (y * jax.nn.sigmoid(y)).astype(o_ref.dtype)


def _causal_conv_silu(x2, col0, chans, buf, w, b, bsz, seq, out_dtype):
    width = w.shape[0]
    assert width - 1 <= SUBLANES
    ts = math.gcd(seq, CONV_ROWS)
    nt = seq // ts
    tc = math.gcd(chans, CONV_COLS, col0)
    assert ts % SUBLANES == 0
    cb0 = col0 // tc
    rb = ts // SUBLANES
    if buf is None:
        buf8 = jnp.zeros((bsz, SUBLANES, chans), x2.dtype)
    else:
        buf8 = jnp.pad(buf.astype(x2.dtype), ((0, 0), (SUBLANES - (width - 1), 0), (0, 0)))
    return pl.pallas_call(
        functools.partial(_conv_body, width=width), grid=(bsz, nt, chans // tc),
        in_specs=[
            pl.BlockSpec((ts, tc), lambda bi, i, c: (bi * nt + i, cb0 + c)),
            pl.BlockSpec((SUBLANES, tc), lambda bi, i, c: (jnp.maximum((bi * nt + i) * rb - 1, 0), cb0 + c)),
            pl.BlockSpec((1, SUBLANES, tc), lambda bi, i, c: (bi, 0, c)),
            pl.BlockSpec((width, tc), lambda bi, i, c: (0, c)),
            pl.BlockSpec((1, tc), lambda bi, i, c: (0, c)),
        ],
        out_specs=pl.BlockSpec((ts, tc), lambda bi, i, c: (bi * nt + i, c)),
        out_shape=jax.ShapeDtypeStruct((bsz * seq, chans), out_dtype),
        compiler_params=_params(("parallel", "parallel", "parallel")), name="conv_silu",
    )(x2, x2, buf8, w, b.reshape(1, chans))


def _conv_state(x2, col0, chans, buf, bsz, seq, width):
    tail = x2.reshape(bsz, seq, -1)[:, max(seq - (width - 1), 0):, col0:col0 + chans]
    if seq >= width - 1:
        return tail
    if buf is None:
        buf = jnp.zeros((bsz, width - 1, chans), x2.dtype)
    return jnp.concatenate([buf.astype(x2.dtype), tail], axis=1)[:, -(width - 1):]


def _mlstm_head(q, k, v, o, gc, gr, ng, c, n, m_prev, *, lp, pad, dh):
    q = q * (dh ** -0.5)
    ii_c, ii_r = gc[:, 0:1], gr[0:1, :]
    ff_c, ff_r = -_softplus(-gc[:, 1:2]), -_softplus(-gr[1:2, :])
    if pad:
        ok_c = lax.broadcasted_iota(jnp.int32, (lp, 1), 0) >= pad
        ok_r = lax.broadcasted_iota(jnp.int32, (1, lp), 1) >= pad
        ii_c, ii_r = jnp.where(ok_c, ii_c, NEG), jnp.where(ok_r, ii_r, NEG)
        ff_c, ff_r = jnp.where(ok_c, ff_c, 0.0), jnp.where(ok_r, ff_r, 0.0)
    b_c = _cumsum_rows(ff_c)
    b_r = _cumsum_lanes(ff_r)
    t_i = lax.broadcasted_iota(jnp.int32, (lp, lp), 0)
    s_i = lax.broadcasted_iota(jnp.int32, (lp, lp), 1)
    log_d = jnp.where(s_i <= t_i, b_c + (ii_r - b_r), NEG)
    log_inter = b_c + m_prev
    m_t = jnp.maximum(log_inter, jnp.max(log_d, axis=1, keepdims=True))
    d_mat = jnp.exp(log_d - m_t)
    w_inter = jnp.exp(log_inter - m_t)
    qb, kb, vb = q.astype(BF16), k.astype(BF16), v.astype(BF16)
    s = _dot_nt(qb, kb) * d_mat
    num = _dot(s.astype(BF16), vb) + w_inter * _dot(qb, c.astype(BF16))
    den = jnp.sum(s, axis=1, keepdims=True) + w_inter * jnp.sum(q * n, axis=1, keepdims=True)
    h = num / jnp.maximum(jnp.abs(den), jnp.exp(-m_t))
    h = h * lax.rsqrt(jnp.mean(h * h, axis=1, keepdims=True) + NORM_EPS) * ng
    hg = jax.nn.sigmoid(o) * h[pad:, :]

    b_last = b_c[lp - 1:lp, :]
    m_last = m_t[lp - 1:lp, :]
    w_end = jnp.exp(b_last - b_c + ii_c - m_last)
    f_end = jnp.exp(b_last + m_prev - m_last)
    kw = k * w_end
    c_new = f_end * c + _dot_tn(kw.astype(BF16), vb)
    n_new = f_end * n + jnp.sum(kw, axis=0, keepdims=True)
    return hg, c_new, n_new, m_last


def _mlstm_body(*refs, lr, lp, has_state, has_prev, dh, hps, layer):
    it = iter(refs)
    q_ref, k_ref, v_ref, o_ref, gc_ref, gr_ref, ng_ref = (next(it) for _ in range(7))
    c0_ref, n0_ref, m0_ref = (next(it), next(it), next(it)) if has_state else (None, None, None)
    if has_prev:
        next(it)
    hg_ref, c1_ref, n1_ref, m1_ref, c_s, n_s, m_s = it
    ci = pl.program_id(2)
    pad = lp - lr

    @pl.when(ci == 0)
    def _():
        if has_state:
            c_s[...] = c0_ref[0, 0]
            n_s[...] = n0_ref[0]
            m_s[...] = m0_ref[0]
        else:
            c_s[...] = jnp.zeros_like(c_s)
            n_s[...] = jnp.zeros_like(n_s)
            m_s[...] = jnp.zeros_like(m_s)

    args = []
    for u in range(hps):
        cs = slice(u * dh, (u + 1) * dh)
        args.append((_pad_front(q_ref[:, cs].astype(F32), pad), _pad_front(k_ref[:, cs].astype(F32), pad),
                     _pad_front(v_ref[:, cs].astype(F32), pad), o_ref[:, cs], gc_ref[u, 0], gr_ref[u, 0],
                     ng_ref[:, cs], c_s[u], n_s[u], m_s[u]))
    outs = [_mlstm_head(*a, lp=lp, pad=pad, dh=dh) for a in args]
    for u, (hg, c_new, n_new, m_new) in enumerate(outs):
        hg_ref[:, u * dh:(u + 1) * dh] = hg.astype(hg_ref.dtype)
        c_s[u] = c_new
        n_s[u] = n_new
        m_s[u] = m_new

    @pl.when(ci == pl.num_programs(2) - 1)
    def _():
        if has_prev:
            c1_ref[0, 0] = c_s[...]
        else:
            for l in range(c1_ref.shape[0]):
                c1_ref[l, 0] = c_s[...] if l == layer else jnp.zeros_like(c_s)
        n1_ref[0] = n_s[...]
        m1_ref[0] = m_s[...]


def _mlstm_core(qk, vo, gates, norm_g, bsz, seq, heads, layer, n_layers, state, c_prev, out_dtype):
    inner = qk.shape[1] // 2
    dh = inner // heads
    hps = math.gcd(heads, ML_HEADS_PER_STEP)
    nhg = heads // hps
    lr = math.gcd(seq, ML_CHUNK)
    lp = lr if lr % SEQ_PAD == 0 else SEQ_PAD
    nc = seq // lr
    pad = lp - lr
    gi = gates[:, :heads].reshape(bsz, seq, heads)
    gf = gates[:, heads:2 * heads].reshape(bsz, seq, heads)
    g = jnp.stack([gi, gf], axis=-1)
    g = g.reshape(bsz, nc, lr, heads, 2)
    g = jnp.pad(g, ((0, 0), (0, 0), (pad, 0), (0, 0), (0, 0)))
    gc = jnp.transpose(g, (3, 0, 1, 2, 4)).reshape(heads, bsz, nc * lp, 2)
    gr = jnp.transpose(g, (3, 0, 4, 1, 2)).reshape(heads, bsz, 2, nc * lp)
    has_state = state is not None
    ops = [qk, qk, vo, vo, gc, gr, norm_g.reshape(1, inner)]
    specs = [
        pl.BlockSpec((lr, hps * dh), lambda b, h, c: (b * nc + c, h)),
        pl.BlockSpec((lr, hps * dh), lambda b, h, c: (b * nc + c, nhg + h)),
        pl.BlockSpec((lr, hps * dh), lambda b, h, c: (b * nc + c, h)),
        pl.BlockSpec((lr, hps * dh), lambda b, h, c: (b * nc + c, nhg + h)),
        pl.BlockSpec((hps, 1, lp, 2), lambda b, h, c: (h, b, c, 0)),
        pl.BlockSpec((hps, 1, 2, lp), lambda b, h, c: (h, b, 0, c)),
        pl.BlockSpec((1, hps * dh), lambda b, h, c: (0, h)),
    ]
    if has_state:
        c0, n0, m0 = state
        ops += [c0, n0.reshape(bsz, heads, 1, dh), m0.reshape(bsz, heads, 1, 1)]
        specs += [
            pl.BlockSpec((1, 1, hps, dh, dh), lambda b, h, c: (layer, b, h, 0, 0)),
            pl.BlockSpec((1, hps, 1, dh), lambda b, h, c: (b, h, 0, 0)),
            pl.BlockSpec((1, hps, 1, 1), lambda b, h, c: (b, h, 0, 0)),
        ]
    aliases = {}
    if c_prev is not None:
        aliases[len(ops)] = 1
        ops.append(c_prev)
        specs.append(pl.BlockSpec(memory_space=pl.ANY))
    body = functools.partial(_mlstm_body, lr=lr, lp=lp, has_state=has_state, has_prev=c_prev is not None,
                             dh=dh, hps=hps, layer=layer)
    if c_prev is not None:
        c1_spec = pl.BlockSpec((1, 1, hps, dh, dh), lambda b, h, c: (layer, b, h, 0, 0))
    else:
        c1_spec = pl.BlockSpec((n_layers, 1, hps, dh, dh), lambda b, h, c: (0, b, h, 0, 0))
    hg, c1, n1, m1 = pl.pallas_call(
        body, grid=(bsz, nhg, nc), in_specs=specs, input_output_aliases=aliases,
        out_specs=[
            pl.BlockSpec((lr, hps * dh), lambda b, h, c: (b * nc + c, h)),
            c1_spec,
            pl.BlockSpec((1, hps, 1, dh), lambda b, h, c: (b, h, 0, 0)),
            pl.BlockSpec((1, hps, 1, 1), lambda b, h, c: (b, h, 0, 0)),
        ],
        out_shape=[
            jax.ShapeDtypeStruct((bsz * seq, inner), out_dtype),
            jax.ShapeDtypeStruct((n_layers, bsz, heads, dh, dh), F32),
            jax.ShapeDtypeStruct((bsz, heads, 1, dh), F32),
            jax.ShapeDtypeStruct((bsz, heads, 1, 1), F32),
        ],
        scratch_shapes=[pltpu.VMEM((hps, dh, dh), F32), pltpu.VMEM((hps, 1, dh), F32),
                        pltpu.VMEM((hps, 1, 1), F32)],
        compiler_params=_params(("parallel", "parallel", "arbitrary")), name="mlstm_core",
    )(*ops)
    return hg, c1, n1.reshape(bsz, heads, dh), m1.reshape(bsz, heads)


def _ssd_group(x, bm, cm, z, dtc, dtr, pr, pc, ng, h, *, lp, pad, hpg, hd):
    dd_c = _softplus(dtc + pr[0:1, :])
    dd_r = _softplus(dtr + pc[:, 0:1])
    if pad:
        dd_c = jnp.where(lax.broadcasted_iota(jnp.int32, (lp, hpg), 0) >= pad, dd_c, 0.0)
        dd_r = jnp.where(lax.broadcasted_iota(jnp.int32, (hpg, lp), 1) >= pad, dd_r, 0.0)
    cum_c = _cumsum_rows(dd_c * pr[1:2, :])
    cum_r = _cumsum_lanes(dd_r * pc[:, 1:2])
    cb = _dot_nt(cm, bm)
    t_i = lax.broadcasted_iota(jnp.int32, (lp, lp), 0)
    s_i = lax.broadcasted_iota(jnp.int32, (lp, lp), 1)
    tri = s_i <= t_i
    head_of_lane = lax.broadcasted_iota(jnp.int32, (1, hpg * hd), 1) // hd
    y = jnp.zeros((lp, hpg * hd), F32)
    for r in range(hpg):
        seg = jnp.where(tri, cum_c[:, r:r + 1] - cum_r[r:r + 1, :], NEG)
        w = jnp.exp(seg) * cb * dd_r[r:r + 1, :]
        xr = jnp.where(head_of_lane == r, x, 0.0).astype(BF16)
        y = y + _dot(w.astype(BF16), xr)
    y = y + _expand_lanes(jnp.exp(cum_c), hd) * _dot_nt(cm, h.astype(BF16))
    y = y + _expand_lanes(pr[2:3, :], hd) * x
    yv = y[pad:, :] * (z * jax.nn.sigmoid(z))
    yv = yv * lax.rsqrt(jnp.mean(yv * yv, axis=1, keepdims=True) + NORM_EPS) * ng
    w_end = jnp.exp(cum_c[lp - 1:lp, :] - cum_c) * dd_c
    xw = x * _expand_lanes(w_end, hd)
    decay = _expand_rows(jnp.exp(cum_r[:, lp - 1:lp]), hd)
    return yv, decay * h + _dot_tn(xw.astype(BF16), bm)


def _ssd_body(*refs, lr, lp, has_state, hpg, hd, gps):
    if has_state:
        (x_ref, bm_ref, cm_ref, z_ref, dtc_ref, dtr_ref, pr_ref, pc_ref, ng_ref, h0_ref,
         y_ref, h1_ref, h_s) = refs
    else:
        (x_ref, bm_ref, cm_ref, z_ref, dtc_ref, dtr_ref, pr_ref, pc_ref, ng_ref,
         y_ref, h1_ref, h_s) = refs
    ci = pl.program_id(2)
    pad = lp - lr
    gw = hpg * hd
    ns = bm_ref.shape[1] // gps

    @pl.when(ci == 0)
    def _():
        if has_state:
            h_s[...] = h0_ref[0]
        else:
            h_s[...] = jnp.zeros_like(h_s)

    args = []
    for gi in range(gps):
        xs = slice(gi * gw, (gi + 1) * gw)
        bs = slice(gi * ns, (gi + 1) * ns)
        args.append((_pad_front(x_ref[:, xs], pad), _pad_front(bm_ref[:, bs], pad).astype(BF16),
                     _pad_front(cm_ref[:, bs], pad).astype(BF16), z_ref[:, xs], dtc_ref[gi, 0], dtr_ref[gi, 0],
                     pr_ref[gi], pc_ref[gi], ng_ref[:, xs], h_s[gi]))
    outs = [_ssd_group(*a, lp=lp, pad=pad, hpg=hpg, hd=hd) for a in args]
    for gi, (yv, h_new) in enumerate(outs):
        y_ref[:, gi * gw:(gi + 1) * gw] = yv.astype(y_ref.dtype)
        h_s[gi] = h_new

    @pl.when(ci == pl.num_programs(2) - 1)
    def _():
        h1_ref[0] = h_s[...]


def _ssd_core(zx, xbc, dt_raw, dt_bias, a_neg, d_skip, norm_g, bsz, seq, groups, nstate, hd, h0, out_dtype):
    inner = norm_g.shape[0]
    heads = inner // hd
    hpg = heads // groups
    gw = hpg * hd
    gps = math.gcd(groups, SSD_GROUPS_PER_STEP)
    ngs = groups // gps
    lr = math.gcd(seq, SSD_CHUNK)
    lp = lr if lr % SEQ_PAD == 0 else SEQ_PAD
    nc = seq // lr
    pad = lp - lr
    d = dt_raw[:, :heads].reshape(bsz, nc, lr, groups, hpg)
    d = jnp.pad(d, ((0, 0), (0, 0), (pad, 0), (0, 0), (0, 0)))
    dtc = jnp.transpose(d, (3, 0, 1, 2, 4)).reshape(groups, bsz, nc * lp, hpg)
    dtr = jnp.transpose(d, (3, 0, 4, 1, 2)).reshape(groups, bsz, hpg, nc * lp)
    par = jnp.stack([dt_bias, a_neg, d_skip]).astype(F32).reshape(3, groups, hpg)
    pr = jnp.transpose(par, (1, 0, 2))
    pc = jnp.transpose(par, (1, 2, 0))
    b_off = inner // (gps * nstate)
    c_off = b_off + ngs
    has_state = h0 is not None
    ops = [xbc, xbc, xbc, zx, dtc, dtr, pr, pc, norm_g.reshape(1, inner)]
    specs = [
        pl.BlockSpec((lr, gps * gw), lambda b, g, c: (b * nc + c, g)),
        pl.BlockSpec((lr, gps * nstate), lambda b, g, c: (b * nc + c, b_off + g)),
        pl.BlockSpec((lr, gps * nstate), lambda b, g, c: (b * nc + c, c_off + g)),
        pl.BlockSpec((lr, gps * gw), lambda b, g, c: (b * nc + c, g)),
        pl.BlockSpec((gps, 1, lp, hpg), lambda b, g, c: (g, b, c, 0)),
        pl.BlockSpec((gps, 1, hpg, lp), lambda b, g, c: (g, b, 0, c)),
        pl.BlockSpec((gps, 3, hpg), lambda b, g, c: (g, 0, 0)),
        pl.BlockSpec((gps, hpg, 3), lambda b, g, c: (g, 0, 0)),
        pl.BlockSpec((1, gps * gw), lambda b, g, c: (0, g)),
    ]
    if has_state:
        ops.append(h0.reshape(bsz, groups, gw, nstate))
        specs.append(pl.BlockSpec((1, gps, gw, nstate), lambda b, g, c: (b, g, 0, 0)))
    body = functools.partial(_ssd_body, lr=lr, lp=lp, has_state=has_state, hpg=hpg, hd=hd, gps=gps)
    y, h1 = pl.pallas_call(
        body, grid=(bsz, ngs, nc), in_specs=specs,
        out_specs=[
            pl.BlockSpec((lr, gps * gw), lambda b, g, c: (b * nc + c, g)),
            pl.BlockSpec((1, gps, gw, nstate), lambda b, g, c: (b, g, 0, 0)),
        ],
        out_shape=[
            jax.ShapeDtypeStruct((bsz * seq, inner), out_dtype),
            jax.ShapeDtypeStruct((bsz, groups, gw, nstate), F32),
        ],
        scratch_shapes=[pltpu.VMEM((gps, gw, nstate), F32)],
        compiler_params=_params(("parallel", "parallel", "arbitrary")), name="ssd_core",
    )(*ops)
    return y, h1.reshape(bsz, heads, hd, nstate)


def _sb_logs2(w):
    t = jnp.log2(1.0 + jnp.exp2(-jnp.abs(w)))
    lb = jnp.minimum(w, 0.0) - t
    return lb, lb - w


def _sbp_body(bias_ref, q_ref, k_ref, v_ref, o_ref, *, blk, hd, hps):
    hg = pl.program_id(1)
    qi = pl.program_id(2)
    scale = hd ** -0.5
    bias = [bias_ref[hg * hps + u] * LOG2E for u in range(hps)]
    qb = [q_ref[0, :, u * hd:(u + 1) * hd].astype(BF16) for u in range(hps)]
    j_i = lax.broadcasted_iota(jnp.int32, (2 * blk, blk), 0)
    s_i = lax.broadcasted_iota(jnp.int32, (2 * blk, blk), 1)
    tri2 = jnp.where((j_i % blk) > s_i, 1.0, 0.0).astype(BF16)
    t_i = lax.broadcasted_iota(jnp.int32, (blk, blk), 0)
    strict = lax.broadcasted_iota(jnp.int32, (blk, blk), 1) < t_i

    def rows(kb):
        return pl.ds(pl.multiple_of(kb * blk, blk), blk)

    def scores(kb, u):
        return _dot_nt(qb[u], k_ref[0, rows(kb), u * hd:(u + 1) * hd].astype(BF16))

    def weights(s, run, u, mask):
        lb, lr = _sb_logs2(s * (scale * LOG2E) + bias[u])
        if mask is not None:
            lr = jnp.where(mask, lr, 0.0)
        hi, lo = _split2(lr)
        local = _dot(jnp.concatenate([hi, lo], axis=1), tri2)
        a = jnp.exp2(lb + (run + local))
        if mask is not None:
            a = jnp.where(mask, a, 0.0)
        return a.astype(BF16), run + (local[:, 0:1] + lr[:, 0:1])

    def weighted_values(a, kb, u):
        return _dot(a, v_ref[0, rows(kb), u * hd:(u + 1) * hd].astype(BF16))

    def step(j, carry):
        kb = qi - 1 - j
        out = []
        for u in range(hps):
            run, acc, s, a_prev = carry[u]
            s_next = scores(jnp.maximum(kb - 1, 0), u)
            acc = acc + weighted_values(a_prev, kb + 1, u)
            a, run = weights(s, run, u, None)
            out.append((run, acc, s_next, a))
        return tuple(out)

    carry = []
    for u in range(hps):
        a, run = weights(scores(qi, u), jnp.zeros((blk, 1), F32), u, strict)
        carry.append((run, jnp.zeros((blk, hd), F32), scores(jnp.maximum(qi - 1, 0), u), a))
    carry = lax.fori_loop(0, qi, step, tuple(carry))
    for u in range(hps):
        acc = carry[u][1] + weighted_values(carry[u][3], 0, u)
        o_ref[:, u * hd:(u + 1) * hd] = acc.astype(o_ref.dtype)


def _sb_prompt(qkv, bias, bsz, seq, heads, hd, out_dtype):
    blk = math.gcd(seq, SB_BLOCK)
    nq = seq // blk
    hps = math.gcd(heads, SB_HEADS_PER_STEP)
    nhg = heads // hps
    body = functools.partial(_sbp_body, blk=blk, hd=hd, hps=hps)
    return pl.pallas_call(
        body, grid=(bsz, nhg, nq),
        in_specs=[
            pl.BlockSpec(memory_space=pltpu.SMEM),
            pl.BlockSpec((1, blk, hps * hd), lambda b, h, i: (0, b * nq + i, h)),
            pl.BlockSpec((1, seq, hps * hd), lambda b, h, i: (1, b, h)),
            pl.BlockSpec((1, seq, hps * hd), lambda b, h, i: (2, b, h)),
        ],
        out_specs=pl.BlockSpec((blk, hps * hd), lambda b, h, i: (b * nq + i, h)),
        out_shape=jax.ShapeDtypeStruct((bsz * seq, heads * hd), out_dtype),
        compiler_params=_params(("parallel", "parallel", "arbitrary")), name="sb_prompt",
    )(bias.astype(F32), qkv, qkv, qkv)


def _sbs_body(pt_ref, q_ref, kn_ref, vn_ref, *rest, sq, heads, hd, page, ppg):
    del pt_ref
    kp_refs, vp_refs = rest[:ppg], rest[ppg:2 * ppg]
    bias_ref, o_ref, run_s, acc_s, qbd_s = rest[2 * ppg:]
    p = pl.program_id(1)
    width = heads * hd
    cols = qbd_s.shape[0]
    scale = hd ** -0.5

    def own_head():
        row_i = lax.broadcasted_iota(jnp.int32, (cols, width), 0)
        lane_i = lax.broadcasted_iota(jnp.int32, (cols, width), 1)
        return (row_i // sq) == (lane_i // hd)

    def process(kks, vvs, mask):
        n = len(kks)
        j_i = lax.broadcasted_iota(jnp.int32, (page, 2 * page), 1)
        s_i = lax.broadcasted_iota(jnp.int32, (page, 2 * page), 0)
        tri2 = jnp.where((j_i % page) > s_i, 1.0, 0.0).astype(BF16)
        kcat = kks[0] if n == 1 else jnp.concatenate(kks, axis=0)
        z = _dot_nt(kcat, qbd_s[...]) * (scale * LOG2E) + bias_ref[...] * LOG2E
        lb, lr = _sb_logs2(z)
        if mask is not None:
            lr = jnp.where(mask, lr, 0.0)
        run = run_s[...]
        parts = []
        for i in range(n):
            rs = slice(i * page, (i + 1) * page)
            hi, lo = _split2(lr[rs])
            local = _dot(tri2, jnp.concatenate([hi, lo], axis=0))
            a = jnp.exp2(lb[rs] + (run + local))
            if mask is not None:
                a = jnp.where(mask, a, 0.0)
            parts.append(a.astype(BF16))
            run = run + (local[0:1, :] + lr[rs][0:1, :])
        run_s[...] = run
        acat = parts[0] if n == 1 else jnp.concatenate(parts, axis=0)
        vcat = vvs[0] if n == 1 else jnp.concatenate(vvs, axis=0)
        acc_s[...] += _dot_tn(acat, vcat)

    def load_page(ref):
        return jnp.concatenate([ref[0, 0, pl.ds(h, page, stride=heads), :] for h in range(heads)],
                               axis=1).astype(BF16)

    @pl.when(p == 0)
    def _():
        run_s[...] = jnp.zeros_like(run_s)
        acc_s[...] = jnp.zeros_like(acc_s)
        q = q_ref[0]
        tiled = jnp.concatenate([q] * heads + [jnp.zeros((cols - heads * sq, width), F32)], axis=0)
        qbd_s[...] = jnp.where(own_head(), tiled, 0.0).astype(BF16)
        key_i = lax.broadcasted_iota(jnp.int32, (page, cols), 0)
        qry_i = lax.broadcasted_iota(jnp.int32, (page, cols), 1) % sq
        zeros = jnp.zeros((page - sq, width), F32)
        process([jnp.concatenate([kn_ref[0], zeros], axis=0).astype(BF16)],
                [jnp.concatenate([vn_ref[0], zeros], axis=0).astype(BF16)], key_i < qry_i)

    @pl.when(p > 0)
    def _():
        process([load_page(r) for r in kp_refs], [load_page(r) for r in vp_refs], None)

    @pl.when(p == pl.num_programs(1) - 1)
    def _():
        acc = jnp.where(own_head(), acc_s[...], 0.0)
        out = acc[0:sq, :]
        for h in range(1, heads):
            out = out + acc[h * sq:(h + 1) * sq, :]
        o_ref[...] = out.astype(o_ref.dtype)


def _sb_sample(qkv, pool_k, pool_v, layer, page_table, bias, bsz, sq, heads, hd, out_dtype):
    width = heads * hd
    n_pages = page_table.shape[1]
    page = pool_k.shape[2]
    cols = SB_COLS
    ppg = math.gcd(n_pages, SB_PAGES_PER_STEP)
    assert heads * sq <= cols and sq <= page
    bias_cols = jnp.pad(jnp.repeat(bias.astype(F32), sq), (0, cols - heads * sq)).reshape(1, cols)
    pool_k = pool_k.reshape(pool_k.shape[0], pool_k.shape[1], page * heads, hd)
    pool_v = pool_v.reshape(pool_v.shape[0], pool_v.shape[1], page * heads, hd)

    def page_spec(i):
        def idx(b, s, pt):
            return (layer, pt[b, n_pages - 1 - (jnp.maximum(s - 1, 0) * ppg + i)], 0, 0)
        return pl.BlockSpec((1, 1, page * heads, hd), idx)

    body = functools.partial(_sbs_body, sq=sq, heads=heads, hd=hd, page=page, ppg=ppg)
    grid_spec = pltpu.PrefetchScalarGridSpec(
        num_scalar_prefetch=1, grid=(bsz, n_pages // ppg + 1),
        in_specs=[
            pl.BlockSpec((1, sq, width), lambda b, s, pt: (0, b, 0)),
            pl.BlockSpec((1, sq, width), lambda b, s, pt: (1, b, 0)),
            pl.BlockSpec((1, sq, width), lambda b, s, pt: (2, b, 0)),
            *[page_spec(i) for i in range(ppg)],
            *[page_spec(i) for i in range(ppg)],
            pl.BlockSpec((1, cols), lambda b, s, pt: (0, 0)),
        ],
        out_specs=pl.BlockSpec((sq, width), lambda b, s, pt: (b, 0)),
        scratch_shapes=[pltpu.VMEM((1, cols), F32), pltpu.VMEM((cols, width), F32),
                        pltpu.VMEM((cols, width), BF16)],
    )
    return pl.pallas_call(
        body, grid_spec=grid_spec,
        out_shape=jax.ShapeDtypeStruct((bsz * sq, width), out_dtype),
        compiler_params=_params(("parallel", "arbitrary")), name="sb_sample",
    )(page_table, qkv, qkv, qkv, *([pool_k] * ppg), *([pool_v] * ppg), bias_cols)


def _xa_body(x_ref, g_ref, wq_ref, mk_ref, mv_ref, wo_ref, o_ref, *, heads, hd):
    scale = hd ** -0.5

    x = x_ref[...]
    xn = (x * lax.rsqrt(jnp.mean(x * x, axis=-1, keepdims=True) + NORM_EPS) * g_ref[...]).astype(BF16)
    q = _dot(xn, wq_ref[...]).astype(BF16)
    outs = []
    for h in range(heads):
        cs = slice(h * hd, (h + 1) * hd)
        s = _dot_nt(q[:, cs], mk_ref[0, :, cs].astype(BF16)) * scale
        e = jnp.exp(s - jnp.max(s, axis=1, keepdims=True))
        pr = e / jnp.sum(e, axis=1, keepdims=True)
        outs.append(_dot(pr.astype(BF16), mv_ref[0, :, cs].astype(BF16)).astype(BF16))
    o_ref[...] = x + _dot(jnp.concatenate(outs, axis=1), wo_ref[...])


def _xa_cache_body(x_ref, g_ref, wq_ref, wo_ref, ck_hbm, cv_hbm, o_ref, kbuf, vbuf, sem, *, layer, heads, hd, bps, seq):
    step = pl.program_id(0)
    slot = step % 2

    def copies(st, sl):
        out = []
        for r in range(bps):
            for h in range(heads):
                src = (layer, st * bps + r, slice(None), h, slice(None))
                out.append(pltpu.make_async_copy(ck_hbm.at[src], kbuf.at[sl, r, h], sem.at[sl, 0, r, h]))
                out.append(pltpu.make_async_copy(cv_hbm.at[src], vbuf.at[sl, r, h], sem.at[sl, 1, r, h]))
        return out

    @pl.when(step == 0)
    def _():
        for c in copies(0, 0):
            c.start()

    @pl.when(step + 1 < pl.num_programs(0))
    def _():
        for c in copies(step + 1, 1 - slot):
            c.start()

    for c in copies(step, slot):
        c.wait()

    scale = hd ** -0.5
    x = x_ref[...]
    xn = (x * lax.rsqrt(jnp.mean(x * x, axis=-1, keepdims=True) + NORM_EPS) * g_ref[...]).astype(BF16)
    q = _dot(xn, wq_ref[...]).astype(BF16)
    rows = []
    for r in range(bps):
        outs = []
        for h in range(heads):
            s = _dot_nt(q[r * seq:(r + 1) * seq, h * hd:(h + 1) * hd], kbuf[slot, r, h].astype(BF16)) * scale
            e = jnp.exp(s - jnp.max(s, axis=1, keepdims=True))
            pr = e / jnp.sum(e, axis=1, keepdims=True)
            outs.append(_dot(pr.astype(BF16), vbuf[slot, r, h].astype(BF16)))
        rows.append(jnp.concatenate(outs, axis=1))
    o_ref[...] = x + _dot(jnp.concatenate(rows, axis=0).astype(BF16), wo_ref[...])


def _cross_attn_cached(x, g, wq, cache_k, cache_v, layer, wo, bsz, seq, mem_len, heads):
    d = x.shape[1]
    hd = d // heads
    bps = math.gcd(bsz, XA_ROWS_PER_STEP)
    body = functools.partial(_xa_cache_body, layer=layer, heads=heads, hd=hd, bps=bps, seq=seq)
    return pl.pallas_call(
        body, grid=(bsz // bps,),
        in_specs=[
            pl.BlockSpec((bps * seq, d), lambda b: (b, 0)),
            pl.BlockSpec((1, d), lambda b: (0, 0)),
            pl.BlockSpec((d, d), lambda b: (0, 0)),
            pl.BlockSpec((d, d), lambda b: (0, 0)),
            pl.BlockSpec(memory_space=pl.ANY),
            pl.BlockSpec(memory_space=pl.ANY),
        ],
        out_specs=pl.BlockSpec((bps * seq, d), lambda b: (b, 0)),
        out_shape=jax.ShapeDtypeStruct((bsz * seq, d), F32),
        scratch_shapes=[pltpu.VMEM((2, bps, heads, mem_len, hd), cache_k.dtype),
                        pltpu.VMEM((2, bps, heads, mem_len, hd), cache_v.dtype),
                        pltpu.SemaphoreType.DMA((2, 2, bps, heads))],
        compiler_params=_params(("arbitrary",)), name="cross_attn_cached",
    )(x, g.reshape(1, d), wq, wo, cache_k, cache_v)


def _cross_attn_layer(x, g, wq, mk, mv, mk_idx, mv_idx, wo, bsz, seq, mem_len, heads):
    d = x.shape[1]
    tq = math.gcd(seq, XA_QBLOCK)
    nq = seq // tq
    body = functools.partial(_xa_body, heads=heads, hd=d // heads)

    return pl.pallas_call(
        body, grid=(bsz, nq),
        in_specs=[
            pl.BlockSpec((tq, d), lambda b, i: (b * nq + i, 0)),
            pl.BlockSpec((1, d), lambda b, i: (0, 0)),
            pl.BlockSpec((d, d), lambda b, i: (0, 0)),
            pl.BlockSpec((1, mem_len, d), lambda b, i: (mk_idx, b, 0)),
            pl.BlockSpec((1, mem_len, d), lambda b, i: (mv_idx, b, 0)),
            pl.BlockSpec((d, d), lambda b, i: (0, 0)),
        ],
        out_specs=pl.BlockSpec((tq, d), lambda b, i: (b * nq + i, 0)),
        out_shape=jax.ShapeDtypeStruct((bsz * seq, d), F32),
        compiler_params=_params(("parallel", "parallel")), name="cross_attn",
    )(x, g.reshape(1, d), wq, mk, mv, wo)


def _prep_weights(p):
    ml_inner = p['ml_w_up'].shape[2]
    ssd_main = p['ssd_w_in'].shape[2] - p['ssd_dt_bias'].shape[1]
    ffn_hidden = p['ffn_w_out'].shape[1]

    def per_layer(a, lo=None, hi=None, pad_to=None, dtype=BF16):
        out = []
        for j in range(a.shape[0]):
            m = a[j, :, lo:hi] if a.ndim == 3 else a[j, lo:hi]
            if pad_to is not None:
                m = jnp.pad(m, [(0, 0)] * (m.ndim - 1) + [(0, pad_to - m.shape[-1])])
            out.append(m.astype(dtype))
        return out

    return {
        'ml_w_up': per_layer(p['ml_w_up']),
        'ml_w_qk': per_layer(p['ml_w_qk']),
        'ml_w_vo': per_layer(p['ml_w_vog'], 0, 2 * ml_inner),
        'ml_w_g': per_layer(p['ml_w_vog'], 2 * ml_inner, None, LANES),
        'ml_b_g': per_layer(p['ml_b_gate'], None, None, LANES, F32),
        'ml_w_down': per_layer(p['ml_w_down']),
        'sb_w_qkv': per_layer(p['sb_w_qkv']),
        'sb_w_o': per_layer(p['sb_w_o']),
        'ssd_w_main': per_layer(p['ssd_w_in'], 0, ssd_main),
        'ssd_w_dt': per_layer(p['ssd_w_in'], ssd_main, None, LANES),
        'ssd_w_out': per_layer(p['ssd_w_out']),
        'ssd_a': -jnp.exp(p['ssd_a_log'].astype(F32)),
        'xa_w_q': per_layer(p['xa_w_q']),
        'xa_w_kv': per_layer(p['xa_w_kv']),
        'xa_w_o': per_layer(p['xa_w_o']),
        'ffn_w_gate': per_layer(p['ffn_w_in'], 0, ffn_hidden),
        'ffn_w_up': per_layer(p['ffn_w_in'], ffn_hidden, None),
        'ffn_w_out': per_layer(p['ffn_w_out']),
    }


def _trunk(x3, mem_kv, ml_states, sb_pools, ssd_states, page_table, p, w, dims, act_dtype):
    bsz, seq, d = x3.shape
    depth = p['norm_mix'].shape[0]
    ml_heads, sb_heads, xa_heads, ssd_groups, ssd_state, ssd_hd, mem_len = dims
    x = x3.reshape(bsz * seq, d)
    ml_new, sb_new, ssd_new = [], [], []
    n_ml = n_sb = n_ssd = 0
    n_ml_layers = p['ml_w_up'].shape[0]
    ml_c = None
    for layer in range(depth):
        kind = layer % 3
        if kind == 0:
            j = n_ml
            n_ml += 1
            inner = p['ml_w_up'].shape[2]
            width = p['ml_conv_w'].shape[1]
            xm = _matmul(x, w['ml_w_up'][j], g=p['norm_mix'][layer])
            st = None if ml_states is None else ml_states[j]
            buf = None if st is None else st[3]
            xc = _causal_conv_silu(xm, 0, inner, buf, p['ml_conv_w'][j], p['ml_conv_b'][j], bsz, seq, act_dtype)
            qk = _matmul(xc, w['ml_w_qk'][j])
            vo, gates = _matmul(xm, w['ml_w_vo'][j], extra=(w['ml_w_g'][j], w['ml_b_g'][j]))
            hg, ml_c, n1, m1 = _mlstm_core(qk, vo, gates, p['ml_norm'][j], bsz, seq, ml_heads, j, n_ml_layers,
                                           None if st is None else st[:3], ml_c, act_dtype)
            x = _matmul(hg, w['ml_w_down'][j], res=x)
            ml_new.append((n1, m1, _conv_state(xm, 0, inner, buf, bsz, seq, width)))
        elif kind == 1:
            j = n_sb
            n_sb += 1
            hd = d // sb_heads
            qkv = _matmul(x, w['sb_w_qkv'][j], g=p['norm_mix'][layer], splits=3)
            if sb_pools is None:
                o = _sb_prompt(qkv, p['sb_bias'][j], bsz, seq, sb_heads, hd, act_dtype)
            else:
                o = _sb_sample(qkv, sb_pools[0], sb_pools[1], j, page_table, p['sb_bias'][j],
                               bsz, seq, sb_heads, hd, act_dtype)
            x = _matmul(o, w['sb_w_o'][j], res=x)
            shp = (bsz, seq, sb_heads, hd)
            sb_new.append((qkv[1].reshape(shp), qkv[2].reshape(shp)))
        else:
            j = n_ssd
            n_ssd += 1
            inner = p['ssd_norm'].shape[1]
            conv_ch = p['ssd_conv_w'].shape[2]
            width = p['ssd_conv_w'].shape[1]
            zx, dt_raw = _matmul(x, w['ssd_w_main'][j], g=p['norm_mix'][layer],
                                 extra=(w['ssd_w_dt'][j], jnp.zeros((LANES,), F32)))
            st = None if ssd_states is None else ssd_states[j]
            buf = None if st is None else st[1]
            xbc = _causal_conv_silu(zx, inner, conv_ch, buf, p['ssd_conv_w'][j], p['ssd_conv_b'][j],
                                    bsz, seq, F32)
            y, h1 = _ssd_core(zx, xbc, dt_raw, p['ssd_dt_bias'][j], w['ssd_a'][j],
                              p['ssd_d'][j], p['ssd_norm'][j], bsz, seq, ssd_groups, ssd_state, ssd_hd,
                              None if st is None else st[0], act_dtype)
            x = _matmul(y, w['ssd_w_out'][j], res=x)
            ssd_new.append((h1, _conv_state(zx, inner, conv_ch, buf, bsz, seq, width)))
        mk, mv, mk_idx, mv_idx = mem_kv[layer]
        if mk.ndim == 5:
            x = _cross_attn_cached(x, p['norm_xa'][layer], w['xa_w_q'][layer], mk, mv, mk_idx,
                                   w['xa_w_o'][layer], bsz, seq, mem_len, xa_heads)
        else:
            x = _cross_attn_layer(x, p['norm_xa'][layer], w['xa_w_q'][layer], mk, mv, mk_idx, mv_idx,
                                  w['xa_w_o'][layer], bsz, seq, mem_len, xa_heads)
        hid = _matmul(x, w['ffn_w_gate'][layer], g=p['norm_ffn'][layer], w2=w['ffn_w_up'][layer],
                      out_dtype=act_dtype)
        x = _matmul(hid, w['ffn_w_out'][layer], res=x)
    y = _final_norm(x, p['norm_final'])
    return y.reshape(bsz, seq, d), ml_c, ml_new, sb_new, ssd_new


def _norm_body(x_ref, g_ref, o_ref):
    x = x_ref[...]
    o_ref[...] = x * lax.rsqrt(jnp.mean(x * x, axis=-1, keepdims=True) + NORM_EPS) * g_ref[...]


def _final_norm(x, g):
    t, d = x.shape
    tm = min(ROW_TILE, t)
    return pl.pallas_call(
        _norm_body, grid=(t // tm,),
        in_specs=[pl.BlockSpec((tm, d), lambda i: (i, 0)), pl.BlockSpec((1, d), lambda i: (0, 0))],
        out_specs=pl.BlockSpec((tm, d), lambda i: (i, 0)),
        out_shape=jax.ShapeDtypeStruct((t, d), F32),
        compiler_params=_params(("parallel",)), name="final_norm",
    )(x, g.reshape(1, d))


def _stack(items, idx):
    return jnp.stack([it[idx] for it in items])


def kernel(x_prompt, x_sample, cache_mem_k, cache_mem_v, cache_sb_k, cache_sb_v, state_ml_c, state_ml_n, state_ml_m, state_ml_conv, state_ssd_h, state_ssd_conv, page_table, mem_prompt, norm_mix, norm_xa, norm_ffn, norm_mem, norm_final, ml_w_up, ml_conv_w, ml_conv_b, ml_w_qk, ml_w_vog, ml_b_gate, ml_norm, ml_w_down, sb_w_qkv, sb_bias, sb_w_o, ssd_w_in, ssd_conv_w, ssd_conv_b, ssd_dt_bias, ssd_a_log, ssd_d, ssd_norm, ssd_w_out, xa_w_q, xa_w_kv, xa_w_o, ffn_w_in, ffn_w_out):
    p = dict(norm_mix=norm_mix, norm_xa=norm_xa, norm_ffn=norm_ffn, norm_final=norm_final,
             ml_w_up=ml_w_up, ml_conv_w=ml_conv_w, ml_conv_b=ml_conv_b, ml_w_qk=ml_w_qk,
             ml_w_vog=ml_w_vog, ml_b_gate=ml_b_gate, ml_norm=ml_norm, ml_w_down=ml_w_down,
             sb_w_qkv=sb_w_qkv, sb_bias=sb_bias, sb_w_o=sb_w_o,
             ssd_w_in=ssd_w_in, ssd_conv_w=ssd_conv_w, ssd_conv_b=ssd_conv_b, ssd_dt_bias=ssd_dt_bias,
             ssd_a_log=ssd_a_log, ssd_d=ssd_d, ssd_norm=ssd_norm, ssd_w_out=ssd_w_out,
             xa_w_q=xa_w_q, xa_w_kv=xa_w_kv, xa_w_o=xa_w_o, ffn_w_in=ffn_w_in, ffn_w_out=ffn_w_out)
    w = _prep_weights(p)
    depth = norm_mix.shape[0]
    d = x_prompt.shape[2]
    bp, mem_len = mem_prompt.shape[0], mem_prompt.shape[1]
    ml_heads = state_ml_c.shape[2]
    sb_heads = cache_sb_k.shape[3]
    xa_heads = cache_mem_k.shape[3]
    ssd_hd, ssd_state = state_ssd_h.shape[3], state_ssd_h.shape[4]
    ssd_groups = (ssd_conv_w.shape[2] - ssd_norm.shape[1]) // (2 * ssd_state)
    n_ml, n_ssd = state_ml_c.shape[0], state_ssd_h.shape[0]
    dims = (ml_heads, sb_heads, xa_heads, ssd_groups, ssd_state, ssd_hd, mem_len)

    mem2 = mem_prompt.reshape(bp * mem_len, d)
    mem_kv_p = [_matmul(mem2, w['xa_w_kv'][l], g=norm_mem[l], splits=2) for l in range(depth)]
    y_prompt, ml_c_p, ml_p, sb_p, ssd_p = _trunk(x_prompt, [(kv, kv, 0, 1) for kv in mem_kv_p], None, None, None,
                                         None, p, w, dims, BF16)
    shp = (bp, mem_len, xa_heads, d // xa_heads)
    mem_k_p = jnp.stack([kv[0].reshape(shp) for kv in mem_kv_p])
    mem_v_p = jnp.stack([kv[1].reshape(shp) for kv in mem_kv_p])

    mem_kv_s = [(cache_mem_k, cache_mem_v, l, l) for l in range(depth)]
    ml_cache = [(state_ml_c, state_ml_n[j], state_ml_m[j], state_ml_conv[j]) for j in range(n_ml)]
    ssd_cache = [(state_ssd_h[j], state_ssd_conv[j]) for j in range(n_ssd)]
    y_sample, ml_c_s, ml_s, sb_s, ssd_s = _trunk(x_sample, mem_kv_s, ml_cache, (cache_sb_k, cache_sb_v), ssd_cache,
                                         page_table, p, w, dims, F32)

    return (y_prompt, y_sample, mem_k_p, mem_v_p,
            _stack(sb_p, 0), _stack(sb_p, 1),
            ml_c_p, _stack(ml_p, 0), _stack(ml_p, 1), _stack(ml_p, 2),
            _stack(ssd_p, 0), _stack(ssd_p, 1),
            _stack(sb_s, 0), _stack(sb_s, 1),
            ml_c_s, _stack(ml_s, 0), _stack(ml_s, 1), _stack(ml_s, 2),
            _stack(ssd_s, 0), _stack(ssd_s, 1))
```

```python
import functools
import math

import jax
import jax.numpy as jnp
from jax import lax
from jax.experimental import pallas as pl
from jax.experimental.pallas import tpu as pltpu

F32 = jnp.float32
BF16 = jnp.bfloat16
NORM_EPS = 1e-6
NEG = -1e30
LOG2E = 1.4426950408889634
SUBLANES = 8
VMEM_LIMIT_BYTES = 56 * 1024 * 1024
LANES = 128
MM_VMEM_BUDGET_BYTES = 44 * 1024 * 1024
MXU_FLOPS = 1.0e15
HBM_BYTES_PER_S = 3.0e12
STEP_OVERHEAD_S = 0.35e-6
ML_HEADS_PER_STEP = 4
SEQ_PAD = 16
SB_COLS = 128
ML_CHUNK = 256
SSD_CHUNK = 128
SSD_GROUPS_PER_STEP = 4
SB_BLOCK = 256
SB_HEADS_PER_STEP = 4
SB_PAGES_PER_STEP = 16
XA_QBLOCK = 1024
XA_ROWS_PER_STEP = 4
CONV_COLS = 2048
CONV_ROWS = 512


def _params(sem):
    return pltpu.CompilerParams(dimension_semantics=sem, vmem_limit_bytes=VMEM_LIMIT_BYTES)


def _dot(a, b):
    return jnp.dot(a, b, preferred_element_type=F32)


def _dot_nt(a, b):
    return lax.dot_general(a, b, (((1,), (1,)), ((), ())), preferred_element_type=F32)


def _dot_tn(a, b):
    return lax.dot_general(a, b, (((0,), (0,)), ((), ())), preferred_element_type=F32)


def _split3(x):
    hi = x.astype(BF16)
    r = x - hi.astype(F32)
    mid = r.astype(BF16)
    lo = (r - mid.astype(F32)).astype(BF16)
    return hi, mid, lo


def _split2(x):
    hi = x.astype(BF16)
    lo = (x - hi.astype(F32)).astype(BF16)
    return hi, lo


def _cumsum_rows(x):
    n = x.shape[0]
    t = lax.broadcasted_iota(jnp.int32, (n, n), 0)
    s = lax.broadcasted_iota(jnp.int32, (n, n), 1)
    m = jnp.where(s <= t, 1.0, 0.0).astype(BF16)
    hi, mid, lo = _split3(x)
    return _dot(m, hi) + _dot(m, mid) + _dot(m, lo)


def _cumsum_lanes(x):
    n = x.shape[1]
    j = lax.broadcasted_iota(jnp.int32, (n, n), 0)
    s = lax.broadcasted_iota(jnp.int32, (n, n), 1)
    m = jnp.where(j <= s, 1.0, 0.0).astype(BF16)
    hi, mid, lo = _split3(x)
    return _dot(hi, m) + _dot(mid, m) + _dot(lo, m)


def _softplus(x):
    return jnp.maximum(x, 0.0) + jnp.log1p(jnp.exp(-jnp.abs(x)))


def _pad_front(x, pad):
    if pad == 0:
        return x
    return jnp.concatenate([jnp.zeros((pad, x.shape[1]), x.dtype), x], axis=0)


def _expand_lanes(cols, width):
    n, k = cols.shape
    lane = lax.broadcasted_iota(jnp.int32, (n, k * width), 1)
    out = jnp.broadcast_to(cols[:, k - 1:k], (n, k * width))
    for r in range(k - 2, -1, -1):
        out = jnp.where(lane < (r + 1) * width, cols[:, r:r + 1], out)
    return out


def _expand_rows(rows, height):
    k, n = rows.shape
    sub = lax.broadcasted_iota(jnp.int32, (k * height, n), 0)
    out = jnp.broadcast_to(rows[k - 1:k, :], (k * height, n))
    for r in range(k - 2, -1, -1):
        out = jnp.where(sub < (r + 1) * height, rows[r:r + 1, :], out)
    return out


def _mm_body(*refs, has_norm, swiglu, has_res, has_extra, has_out_norm):
    it = iter(refs)
    x_ref = next(it)
    g_ref = next(it) if has_norm else None
    w_ref = next(it)
    w2_ref = next(it) if swiglu else None
    r_ref = next(it) if has_res else None
    we_ref, be_ref = (next(it), next(it)) if has_extra else (None, None)
    og_ref = next(it) if has_out_norm else None
    o_ref = next(it)
    x = x_ref[...]
    if has_norm:
        x32 = x.astype(F32)
        x = x32 * lax.rsqrt(jnp.mean(x32 * x32, axis=-1, keepdims=True) + NORM_EPS) * g_ref[...]
    xb = x.astype(BF16)
    acc = _dot(xb, w_ref[...])
    if swiglu:
        acc = acc * jax.nn.sigmoid(acc) * _dot(xb, w2_ref[...])
    if has_res:
        acc = r_ref[...] + acc
    if has_out_norm:
        acc = acc * lax.rsqrt(jnp.mean(acc * acc, axis=-1, keepdims=True) + NORM_EPS) * og_ref[...]
    o_ref[...] = acc.reshape(o_ref.shape).astype(o_ref.dtype)
    if has_extra:
        next(it)[0] = _dot(xb, we_ref[...]) + be_ref[...]


def _mm_tiles(t, k, ncols, x_bytes, out_bytes, n_w, has_res, n_passes):
    best = None
    for tm in sorted({min(t, m) for m in (256, 512, 1024, 2048)}):
        if t % tm:
            continue
        for d in range(1, ncols // LANES + 1):
            tn = ncols // d
            if ncols % d or tn % LANES:
                continue
            vmem = 2 * (tm * k * x_bytes + k * tn * 2 * n_w + tm * tn * (out_bytes + 4 * has_res)
                        + tm * LANES * 4 + k * LANES * 2)
            if vmem > MM_VMEM_BUDGET_BYTES:
                continue
            col_tiles = d * n_passes
            steps = col_tiles * (t // tm)
            hbm = col_tiles * t * k * x_bytes + t * ncols * n_passes * (out_bytes + 4 * has_res)
            cost = (max(2.0 * t * k * ncols * n_passes * n_w / MXU_FLOPS, hbm / HBM_BYTES_PER_S)
                    + steps * STEP_OVERHEAD_S)
            if best is None or cost < best[0]:
                best = (cost, tm, tn)
    assert best is not None
    return best[1], best[2]


def _matmul(x, w, *, g=None, w2=None, res=None, extra=None, splits=1, out_g=None, out_dtype=F32):
    t, k = x.shape
    n = w.shape[1]
    ncols = n // splits
    tm, tn = _mm_tiles(t, k, ncols, x.dtype.itemsize, jnp.dtype(out_dtype).itemsize,
                       2 if w2 is not None else 1, res is not None, splits)
    cps = ncols // tn
    grid = (n // tn, t // tm)
    ops = [x]
    specs = [pl.BlockSpec((tm, k), lambda j, i: (i, 0))]
    if g is not None:
        ops.append(g.reshape(1, k))
        specs.append(pl.BlockSpec((1, k), lambda j, i: (0, 0)))
    ops.append(w)
    specs.append(pl.BlockSpec((k, tn), lambda j, i: (0, j)))
    if w2 is not None:
        ops.append(w2)
        specs.append(pl.BlockSpec((k, tn), lambda j, i: (0, j)))
    if res is not None:
        ops.append(res)
        specs.append(pl.BlockSpec((tm, tn), lambda j, i: (i, j)))
    if extra is not None:
        ops += [extra[0], extra[1].reshape(1, LANES)]
        specs += [pl.BlockSpec((k, LANES), lambda j, i: (0, 0)), pl.BlockSpec((1, LANES), lambda j, i: (0, 0))]
    if out_g is not None:
        assert tn == n and splits == 1
        ops.append(out_g.reshape(1, n))
        specs.append(pl.BlockSpec((1, n), lambda j, i: (0, 0)))
    if splits == 1:
        out_specs = [pl.BlockSpec((tm, tn), lambda j, i: (i, j))]
        out_shape = [jax.ShapeDtypeStruct((t, n), out_dtype)]
    else:
        out_specs = [pl.BlockSpec((1, tm, tn), lambda j, i: (j // cps, i, j % cps))]
        out_shape = [jax.ShapeDtypeStruct((splits, t, ncols), out_dtype)]
    if extra is not None:
        out_specs.append(pl.BlockSpec((1, tm, LANES), lambda j, i: (j, i, 0)))
        out_shape.append(jax.ShapeDtypeStruct((grid[0], t, LANES), F32))
    body = functools.partial(_mm_body, has_norm=g is not None, swiglu=w2 is not None,
                             has_res=res is not None, has_extra=extra is not None,
                             has_out_norm=out_g is not None)
    out = pl.pallas_call(
        body, grid=grid, in_specs=specs, out_specs=out_specs, out_shape=out_shape,
        compiler_params=_params(("arbitrary", "arbitrary")), name="matmul",
    )(*ops)
    return (out[0], out[1][0]) if extra is not None else out[0]


def _conv_body(x_ref, prev_ref, buf_ref, w_ref, b_ref, o_ref, *, width):
    ts = x_ref.shape[0]
    x = x_ref[...]
    hist = jnp.where(pl.program_id(1) == 0, buf_ref[0], prev_ref[...])
    xx = jnp.concatenate([hist, x], axis=0)
    w = w_ref[...]
    y = b_ref[...]
    for j in range(width):
        k = width - 1 - j
        win = x if k == 0 else pltpu.roll(xx, k, 0)[SUBLANES:, :]
        y = y + win * w[j:j + 1, :]
    o_ref[...] = (y * jax.nn.sigmoid(y)).astype(o_ref.dtype)


def _causal_conv_silu(x2, col0, chans, buf, w, b, bsz, seq, out_dtype):
    width = w.shape[0]
    assert width - 1 <= SUBLANES
    ts = math.gcd(seq, CONV_ROWS)
    nt = seq // ts
    tc = math.gcd(chans, CONV_COLS, col0)
    assert ts % SUBLANES == 0
    cb0 = col0 // tc
    rb = ts // SUBLANES
    if buf is None:
        buf8 = jnp.zeros((bsz, SUBLANES, chans), x2.dtype)
    else:
        buf8 = jnp.pad(buf.astype(x2.dtype), ((0, 0), (SUBLANES - (width - 1), 0), (0, 0)))
    return pl.pallas_call(
        functools.partial(_conv_body, width=width), grid=(bsz, nt, chans // tc),
        in_specs=[
            pl.BlockSpec((ts, tc), lambda bi, i, c: (bi * nt + i, cb0 + c)),
            pl.BlockSpec((SUBLANES, tc), lambda bi, i, c: (jnp.maximum((bi * nt + i) * rb - 1, 0), cb0 + c)),
            pl.BlockSpec((1, SUBLANES, tc), lambda bi, i, c: (bi, 0, c)),
            pl.BlockSpec((width, tc), lambda bi, i, c: (0, c)),
            pl.BlockSpec((1, tc), lambda bi, i, c: (0, c)),
        ],
        out_specs=pl.BlockSpec((ts, tc), lambda bi, i, c: (bi * nt + i, c)),
        out_shape=jax.ShapeDtypeStruct((bsz * seq, chans), out_dtype),
        compiler_params=_params(("parallel", "parallel", "parallel")), name="conv_silu",
    )(x2, x2, buf8, w, b.reshape(1, chans))


def _conv_state(x2, col0, chans, buf, bsz, seq, width):
    tail = x2.reshape(bsz, seq, -1)[:, max(seq - (width - 1), 0):, col0:col0 + chans]
    if seq >= width - 1:
        return tail
    if buf is None:
        buf = jnp.zeros((bsz, width - 1, chans), x2.dtype)
    return jnp.concatenate([buf.astype(x2.dtype), tail], axis=1)[:, -(width - 1):]


def _mlstm_head(q, k, v, o, gc, gr, ng, c, n, m_prev, *, lp, pad, dh):
    q = q * (dh ** -0.5)
    ii_c, ii_r = gc[:, 0:1], gr[0:1, :]
    ff_c, ff_r = -_softplus(-gc[:, 1:2]), -_softplus(-gr[1:2, :])
    if pad:
        ok_c = lax.broadcasted_iota(jnp.int32, (lp, 1), 0) >= pad
        ok_r = lax.broadcasted_iota(jnp.int32, (1, lp), 1) >= pad
        ii_c, ii_r = jnp.where(ok_c, ii_c, NEG), jnp.where(ok_r, ii_r, NEG)
        ff_c, ff_r = jnp.where(ok_c, ff_c, 0.0), jnp.where(ok_r, ff_r, 0.0)
    b_c = _cumsum_rows(ff_c)
    b_r = _cumsum_lanes(ff_r)
    t_i = lax.broadcasted_iota(jnp.int32, (lp, lp), 0)
    s_i = lax.broadcasted_iota(jnp.int32, (lp, lp), 1)
    log_d = jnp.where(s_i <= t_i, b_c + (ii_r - b_r), NEG)
    log_inter = b_c + m_prev
    m_t = jnp.maximum(log_inter, jnp.max(log_d, axis=1, keepdims=True))
    d_mat = jnp.exp(log_d - m_t)
    w_inter = jnp.exp(log_inter - m_t)
    qb, kb, vb = q.astype(BF16), k.astype(BF16), v.astype(BF16)
    s = _dot_nt(qb, kb) * d_mat
    num = _dot(s.astype(BF16), vb) + w_inter * _dot(qb, c.astype(BF16))
    den = jnp.sum(s, axis=1, keepdims=True) + w_inter * jnp.sum(q * n, axis=1, keepdims=True)
    h = num / jnp.maximum(jnp.abs(den), jnp.exp(-m_t))
    h = h * lax.rsqrt(jnp.mean(h * h, axis=1, keepdims=True) + NORM_EPS) * ng
    hg = jax.nn.sigmoid(o) * h[pad:, :]

    b_last = b_c[lp - 1:lp, :]
    m_last = m_t[lp - 1:lp, :]
    w_end = jnp.exp(b_last - b_c + ii_c - m_last)
    f_end = jnp.exp(b_last + m_prev - m_last)
    kw = k * w_end
    c_new = f_end * c + _dot_tn(kw.astype(BF16), vb)
    n_new = f_end * n + jnp.sum(kw, axis=0, keepdims=True)
    return hg, c_new, n_new, m_last


def _mlstm_body(*refs, lr, lp, has_state, has_prev, dh, hps, layer):
    it = iter(refs)
    q_ref, k_ref, v_ref, o_ref, gc_ref, gr_ref, ng_ref = (next(it) for _ in range(7))
    c0_ref, n0_ref, m0_ref = (next(it), next(it), next(it)) if has_state else (None, None, None)
    if has_prev:
        next(it)
    hg_ref, c1_ref, n1_ref, m1_ref, c_s, n_s, m_s = it
    ci = pl.program_id(2)
    pad = lp - lr

    @pl.when(ci == 0)
    def _():
        if has_state:
            c_s[...] = c0_ref[0, 0]
            n_s[...] = n0_ref[0]
            m_s[...] = m0_ref[0]
        else:
            c_s[...] = jnp.zeros_like(c_s)
            n_s[...] = jnp.zeros_like(n_s)
            m_s[...] = jnp.zeros_like(m_s)

    args = []
    for u in range(hps):
        cs = slice(u * dh, (u + 1) * dh)
        args.append((_pad_front(q_ref[:, cs].astype(F32), pad), _pad_front(k_ref[:, cs].astype(F32), pad),
                     _pad_front(v_ref[:, cs].astype(F32), pad), o_ref[:, cs], gc_ref[u, 0], gr_ref[u, 0],
                     ng_ref[:, cs], c_s[u], n_s[u], m_s[u]))
    outs = [_mlstm_head(*a, lp=lp, pad=pad, dh=dh) for a in args]
    for u, (hg, c_new, n_new, m_new) in enumerate(outs):
        hg_ref[:, u * dh:(u + 1) * dh] = hg.astype(hg_ref.dtype)
        c_s[u] = c_new
        n_s[u] = n_new
        m_s[u] = m_new

    @pl.when(ci == pl.num_programs(2) - 1)
    def _():
        if has_prev:
            c1_ref[0, 0] = c_s[...]
        else:
            for l in range(c1_ref.shape[0]):
                c1_ref[l, 0] = c_s[...] if l == layer else jnp.zeros_like(c_s)
        n1_ref[0] = n_s[...]
        m1_ref[0] = m_s[...]


def _mlstm_core(qk, vo, gates, norm_g, bsz, seq, heads, layer, n_layers, state, c_prev, out_dtype):
    inner = qk.shape[1] // 2
    dh = inner // heads
    hps = math.gcd(heads, ML_HEADS_PER_STEP)
    nhg = heads // hps
    lr = math.gcd(seq, ML_CHUNK)
    lp = lr if lr % SEQ_PAD == 0 else SEQ_PAD
    nc = seq // lr
    pad = lp - lr
    gi = gates[:, :heads].reshape(bsz, seq, heads)
    gf = gates[:, heads:2 * heads].reshape(bsz, seq, heads)
    g = jnp.stack([gi, gf], axis=-1)
    g = g.reshape(bsz, nc, lr, heads, 2)
    g = jnp.pad(g, ((0, 0), (0, 0), (pad, 0), (0, 0), (0, 0)))
    gc = jnp.transpose(g, (3, 0, 1, 2, 4)).reshape(heads, bsz, nc * lp, 2)
    gr = jnp.transpose(g, (3, 0, 4, 1, 2)).reshape(heads, bsz, 2, nc * lp)
    has_state = state is not None
    ops = [qk, qk, vo, vo, gc, gr, norm_g.reshape(1, inner)]
    specs = [
        pl.BlockSpec((lr, hps * dh), lambda b, h, c: (b * nc + c, h)),
        pl.BlockSpec((lr, hps * dh), lambda b, h, c: (b * nc + c, nhg + h)),
        pl.BlockSpec((lr, hps * dh), lambda b, h, c: (b * nc + c, h)),
        pl.BlockSpec((lr, hps * dh), lambda b, h, c: (b * nc + c, nhg + h)),
        pl.BlockSpec((hps, 1, lp, 2), lambda b, h, c: (h, b, c, 0)),
        pl.BlockSpec((hps, 1, 2, lp), lambda b, h, c: (h, b, 0, c)),
        pl.BlockSpec((1, hps * dh), lambda b, h, c: (0, h)),
    ]
    if has_state:
        c0, n0, m0 = state
        ops += [c0, n0.reshape(bsz, heads, 1, dh), m0.reshape(bsz, heads, 1, 1)]
        specs += [
            pl.BlockSpec((1, 1, hps, dh, dh), lambda b, h, c: (layer, b, h, 0, 0)),
            pl.BlockSpec((1, hps, 1, dh), lambda b, h, c: (b, h, 0, 0)),
            pl.BlockSpec((1, hps, 1, 1), lambda b, h, c: (b, h, 0, 0)),
        ]
    aliases = {}
    if c_prev is not None:
        aliases[len(ops)] = 1
        ops.append(c_prev)
        specs.append(pl.BlockSpec(memory_space=pl.ANY))
    body = functools.partial(_mlstm_body, lr=lr, lp=lp, has_state=has_state, has_prev=c_prev is not None,
                             dh=dh, hps=hps, layer=layer)
    if c_prev is not None:
        c1_spec = pl.BlockSpec((1, 1, hps, dh, dh), lambda b, h, c: (layer, b, h, 0, 0))
    else:
        c1_spec = pl.BlockSpec((n_layers, 1, hps, dh, dh), lambda b, h, c: (0, b, h, 0, 0))
    hg, c1, n1, m1 = pl.pallas_call(
        body, grid=(bsz, nhg, nc), in_specs=specs, input_output_aliases=aliases,
        out_specs=[
            pl.BlockSpec((lr, hps * dh), lambda b, h, c: (b * nc + c, h)),
            c1_spec,
            pl.BlockSpec((1, hps, 1, dh), lambda b, h, c: (b, h, 0, 0)),
            pl.BlockSpec((1, hps, 1, 1), lambda b, h, c: (b, h, 0, 0)),
        ],
        out_shape=[
            jax.ShapeDtypeStruct((bsz * seq, inner), out_dtype),
            jax.ShapeDtypeStruct((n_layers, bsz, heads, dh, dh), F32),
            jax.ShapeDtypeStruct((bsz, heads, 1, dh), F32),
            jax.ShapeDtypeStruct((bsz, heads, 1, 1), F32),
        ],
        scratch_shapes=[pltpu.VMEM((hps, dh, dh), F32), pltpu.VMEM((hps, 1, dh), F32),
                        pltpu.VMEM((hps, 1, 1), F32)],
        compiler_params=_params(("parallel", "parallel", "arbitrary")), name="mlstm_core",
    )(*ops)
    return hg, c1, n1.reshape(bsz, heads, dh), m1.reshape(bsz, heads)


def _ssd_group(x, bm, cm, z, dtc, dtr, pr, pc, ng, h, *, lp, pad, hpg, hd):
    dd_c = _softplus(dtc + pr[0:1, :])
    dd_r = _softplus(dtr + pc[:, 0:1])
    if pad:
        dd_c = jnp.where(lax.broadcasted_iota(jnp.int32, (lp, hpg), 0) >= pad, dd_c, 0.0)
        dd_r = jnp.where(lax.broadcasted_iota(jnp.int32, (hpg, lp), 1) >= pad, dd_r, 0.0)
    cum_c = _cumsum_rows(dd_c * pr[1:2, :])
    cum_r = _cumsum_lanes(dd_r * pc[:, 1:2])
    cb = _dot_nt(cm, bm)
    t_i = lax.broadcasted_iota(jnp.int32, (lp, lp), 0)
    s_i = lax.broadcasted_iota(jnp.int32, (lp, lp), 1)
    tri = s_i <= t_i
    head_of_lane = lax.broadcasted_iota(jnp.int32, (1, hpg * hd), 1) // hd
    y = jnp.zeros((lp, hpg * hd), F32)
    for r in range(hpg):
        seg = jnp.where(tri, cum_c[:, r:r + 1] - cum_r[r:r + 1, :], NEG)
        w = jnp.exp(seg) * cb * dd_r[r:r + 1, :]
        xr = jnp.where(head_of_lane == r, x, 0.0).astype(BF16)
        y = y + _dot(w.astype(BF16), xr)
    y = y + _expand_lanes(jnp.exp(cum_c), hd) * _dot_nt(cm, h.astype(BF16))
    y = y + _expand_lanes(pr[2:3, :], hd) * x
    yv = y[pad:, :] * (z * jax.nn.sigmoid(z))
    yv = yv * lax.rsqrt(jnp.mean(yv * yv, axis=1, keepdims=True) + NORM_EPS) * ng
    w_end = jnp.exp(cum_c[lp - 1:lp, :] - cum_c) * dd_c
    xw = x * _expand_lanes(w_end, hd)
    decay = _expand_rows(jnp.exp(cum_r[:, lp - 1:lp]), hd)
    return yv, decay * h + _dot_tn(xw.astype(BF16), bm)


def _ssd_body(*refs, lr, lp, has_state, hpg, hd, gps):
    if has_state:
        (x_ref, bm_ref, cm_ref, z_ref, dtc_ref, dtr_ref, pr_ref, pc_ref, ng_ref, h0_ref,
         y_ref, h1_ref, h_s) = refs
    else:
        (x_ref, bm_ref, cm_ref, z_ref, dtc_ref, dtr_ref, pr_ref, pc_ref, ng_ref,
         y_ref, h1_ref, h_s) = refs
    ci = pl.program_id(2)
    pad = lp - lr
    gw = hpg * hd
    ns = bm_ref.shape[1] // gps

    @pl.when(ci == 0)
    def _():
        if has_state:
            h_s[...] = h0_ref[0]
        else:
            h_s[...] = jnp.zeros_like(h_s)

    args = []
    for gi in range(gps):
        xs = slice(gi * gw, (gi + 1) * gw)
        bs = slice(gi * ns, (gi + 1) * ns)
        args.append((_pad_front(x_ref[:, xs], pad), _pad_front(bm_ref[:, bs], pad).astype(BF16),
                     _pad_front(cm_ref[:, bs], pad).astype(BF16), z_ref[:, xs], dtc_ref[gi, 0], dtr_ref[gi, 0],
                     pr_ref[gi], pc_ref[gi], ng_ref[:, xs], h_s[gi]))
    outs = [_ssd_group(*a, lp=lp, pad=pad, hpg=hpg, hd=hd) for a in args]
    for gi, (yv, h_new) in enumerate(outs):
        y_ref[:, gi * gw:(gi + 1) * gw] = yv.astype(y_ref.dtype)
        h_s[gi] = h_new

    @pl.when(ci == pl.num_programs(2) - 1)
    def _():
        h1_ref[0] = h_s[...]


def _ssd_core(zx, xbc, dt_raw, dt_bias, a_neg, d_skip, norm_g, bsz, seq, groups, nstate, hd, h0, out_dtype):
    inner = norm_g.shape[0]
    heads = inner // hd
    hpg = heads // groups
    gw = hpg * hd
    gps = math.gcd(groups, SSD_GROUPS_PER_STEP)
    ngs = groups // gps
    lr = math.gcd(seq, SSD_CHUNK)
    lp = lr if lr % SEQ_PAD == 0 else SEQ_PAD
    nc = seq // lr
    pad = lp - lr
    d = dt_raw[:, :heads].reshape(bsz, nc, lr, groups, hpg)
    d = jnp.pad(d, ((0, 0), (0, 0), (pad, 0), (0, 0), (0, 0)))
    dtc = jnp.transpose(d, (3, 0, 1, 2, 4)).reshape(groups, bsz, nc * lp, hpg)
    dtr = jnp.transpose(d, (3, 0, 4, 1, 2)).reshape(groups, bsz, hpg, nc * lp)
    par = jnp.stack([dt_bias, a_neg, d_skip]).astype(F32).reshape(3, groups, hpg)
    pr = jnp.transpose(par, (1, 0, 2))
    pc = jnp.transpose(par, (1, 2, 0))
    b_off = inner // (gps * nstate)
    c_off = b_off + ngs
    has_state = h0 is not None
    ops = [xbc, xbc, xbc, zx, dtc, dtr, pr, pc, norm_g.reshape(1, inner)]
    specs = [
        pl.BlockSpec((lr, gps * gw), lambda b, g, c: (b * nc + c, g)),
        pl.BlockSpec((lr, gps * nstate), lambda b, g, c: (b * nc + c, b_off + g)),
        pl.BlockSpec((lr, gps * nstate), lambda b, g, c: (b * nc + c, c_off + g)),
        pl.BlockSpec((lr, gps * gw), lambda b, g, c: (b * nc + c, g)),
        pl.BlockSpec((gps, 1, lp, hpg), lambda b, g, c: (g, b, c, 0)),
        pl.BlockSpec((gps, 1, hpg, lp), lambda b, g, c: (g, b, 0, c)),
        pl.BlockSpec((gps, 3, hpg), lambda b, g, c: (g, 0, 0)),
        pl.BlockSpec((gps, hpg, 3), lambda b, g, c: (g, 0, 0)),
        pl.BlockSpec((1, gps * gw), lambda b, g, c: (0, g)),
    ]
    if has_state:
        ops.append(h0.reshape(bsz, groups, gw, nstate))
        specs.append(pl.BlockSpec((1, gps, gw, nstate), lambda b, g, c: (b, g, 0, 0)))
    body = functools.partial(_ssd_body, lr=lr, lp=lp, has_state=has_state, hpg=hpg, hd=hd, gps=gps)
    y, h1 = pl.pallas_call(
        body, grid=(bsz, ngs, nc), in_specs=specs,
        out_specs=[
            pl.BlockSpec((lr, gps * gw), lambda b, g, c: (b * nc + c, g)),
            pl.BlockSpec((1, gps, gw, nstate), lambda b, g, c: (b, g, 0, 0)),
        ],
        out_shape=[
            jax.ShapeDtypeStruct((bsz * seq, inner), out_dtype),
            jax.ShapeDtypeStruct((bsz, groups, gw, nstate), F32),
        ],
        scratch_shapes=[pltpu.VMEM((gps, gw, nstate), F32)],
        compiler_params=_params(("parallel", "parallel", "arbitrary")), name="ssd_core",
    )(*ops)
    return y, h1.reshape(bsz, heads, hd, nstate)


def _sb_logs2(w):
    t = jnp.log2(1.0 + jnp.exp2(-jnp.abs(w)))
    lb = jnp.minimum(w, 0.0) - t
    return lb, lb - w


def _sbp_body(bias_ref, q_ref, k_ref, v_ref, o_ref, *, blk, hd, hps):
    hg = pl.program_id(1)
    qi = pl.program_id(2)
    scale = hd ** -0.5
    bias = [bias_ref[hg * hps + u] * LOG2E for u in range(hps)]
    qb = [q_ref[0, :, u * hd:(u + 1) * hd].astype(BF16) for u in range(hps)]
    j_i = lax.broadcasted_iota(jnp.int32, (2 * blk, blk), 0)
    s_i = lax.broadcasted_iota(jnp.int32, (2 * blk, blk), 1)
    tri2 = jnp.where((j_i % blk) > s_i, 1.0, 0.0).astype(BF16)
    t_i = lax.broadcasted_iota(jnp.int32, (blk, blk), 0)
    strict = lax.broadcasted_iota(jnp.int32, (blk, blk), 1) < t_i

    def rows(kb):
        return pl.ds(pl.multiple_of(kb * blk, blk), blk)

    def scores(kb, u):
        return _dot_nt(qb[u], k_ref[0, rows(kb), u * hd:(u + 1) * hd].astype(BF16))

    def weights(s, run, u, mask):
        lb, lr = _sb_logs2(s * (scale * LOG2E) + bias[u])
        if mask is not None:
            lr = jnp.where(mask, lr, 0.0)
        hi, lo = _split2(lr)
        local = _dot(jnp.concatenate([hi, lo], axis=1), tri2)
        a = jnp.exp2(lb + (run + local))
        if mask is not None:
            a = jnp.where(mask, a, 0.0)
        return a.astype(BF16), run + (local[:, 0:1] + lr[:, 0:1])

    def weighted_values(a, kb, u):
        return _dot(a, v_ref[0, rows(kb), u * hd:(u + 1) * hd].astype(BF16))

    def step(j, carry):
        kb = qi - 1 - j
        out = []
        for u in range(hps):
            run, acc, s, a_prev = carry[u]
            s_next = scores(jnp.maximum(kb - 1, 0), u)
            acc = acc + weighted_values(a_prev, kb + 1, u)
            a, run = weights(s, run, u, None)
            out.append((run, acc, s_next, a))
        return tuple(out)

    carry = []
    for u in range(hps):
        a, run = weights(scores(qi, u), jnp.zeros((blk, 1), F32), u, strict)
        carry.append((run, jnp.zeros((blk, hd), F32), scores(jnp.maximum(qi - 1, 0), u), a))
    carry = lax.fori_loop(0, qi, step, tuple(carry))
    for u in range(hps):
        acc = carry[u][1] + weighted_values(carry[u][3], 0, u)
        o_ref[:, u * hd:(u + 1) * hd] = acc.astype(o_ref.dtype)


def _sb_prompt(qkv, bias, bsz, seq, heads, hd, out_dtype):
    blk = math.gcd(seq, SB_BLOCK)
    nq = seq // blk
    hps = math.gcd(heads, SB_HEADS_PER_STEP)
    nhg = heads // hps
    body = functools.partial(_sbp_body, blk=blk, hd=hd, hps=hps)
    return pl.pallas_call(
        body, grid=(bsz, nhg, nq),
        in_specs=[
            pl.BlockSpec(memory_space=pltpu.SMEM),
            pl.BlockSpec((1, blk, hps * hd), lambda b, h, i: (0, b * nq + i, h)),
            pl.BlockSpec((1, seq, hps * hd), lambda b, h, i: (1, b, h)),
            pl.BlockSpec((1, seq, hps * hd), lambda b, h, i: (2, b, h)),
        ],
        out_specs=pl.BlockSpec((blk, hps * hd), lambda b, h, i: (b * nq + i, h)),
        out_shape=jax.ShapeDtypeStruct((bsz * seq, heads * hd), out_dtype),
        compiler_params=_params(("parallel", "parallel", "arbitrary")), name="sb_prompt",
    )(bias.astype(F32), qkv, qkv, qkv)


def _sbs_body(pt_ref, q_ref, kn_ref, vn_ref, *rest, sq, heads, hd, page, ppg):
    del pt_ref
    kp_refs, vp_refs = rest[:ppg], rest[ppg:2 * ppg]
    bias_ref, o_ref, run_s, acc_s, qbd_s = rest[2 * ppg:]
    p = pl.program_id(1)
    width = heads * hd
    cols = qbd_s.shape[0]
    scale = hd ** -0.5

    def own_head():
        row_i = lax.broadcasted_iota(jnp.int32, (cols, width), 0)
        lane_i = lax.broadcasted_iota(jnp.int32, (cols, width), 1)
        return (row_i // sq) == (lane_i // hd)

    def process(kks, vvs, mask):
        n = len(kks)
        j_i = lax.broadcasted_iota(jnp.int32, (page, 2 * page), 1)
        s_i = lax.broadcasted_iota(jnp.int32, (page, 2 * page), 0)
        tri2 = jnp.where((j_i % page) > s_i, 1.0, 0.0).astype(BF16)
        kcat = kks[0] if n == 1 else jnp.concatenate(kks, axis=0)
        z = _dot_nt(kcat, qbd_s[...]) * (scale * LOG2E) + bias_ref[...] * LOG2E
        lb, lr = _sb_logs2(z)
        if mask is not None:
            lr = jnp.where(mask, lr, 0.0)
        run = run_s[...]
        parts = []
        for i in range(n):
            rs = slice(i * page, (i + 1) * page)
            hi, lo = _split2(lr[rs])
            local = _dot(tri2, jnp.concatenate([hi, lo], axis=0))
            a = jnp.exp2(lb[rs] + (run + local))
            if mask is not None:
                a = jnp.where(mask, a, 0.0)
            parts.append(a.astype(BF16))
            run = run + (local[0:1, :] + lr[rs][0:1, :])
        run_s[...] = run
        acat = parts[0] if n == 1 else jnp.concatenate(parts, axis=0)
        vcat = vvs[0] if n == 1 else jnp.concatenate(vvs, axis=0)
        acc_s[...] += _dot_tn(acat, vcat)

    def load_page(ref):
        return jnp.concatenate([ref[0, 0, pl.ds(h, page, stride=heads), :] for h in range(heads)],
                               axis=1).astype(BF16)

    @pl.when(p == 0)
    def _():
        run_s[...] = jnp.zeros_like(run_s)
        acc_s[...] = jnp.zeros_like(acc_s)
        q = q_ref[0]
        tiled = jnp.concatenate([q] * heads + [jnp.zeros((cols - heads * sq, width), F32)], axis=0)
        qbd_s[...] = jnp.where(own_head(), tiled, 0.0).astype(BF16)
        key_i = lax.broadcasted_iota(jnp.int32, (page, cols), 0)
        qry_i = lax.broadcasted_iota(jnp.int32, (page, cols), 1) % sq
        zeros = jnp.zeros((page - sq, width), F32)
        process([jnp.concatenate([kn_ref[0], zeros], axis=0).astype(BF16)],
                [jnp.concatenate([vn_ref[0], zeros], axis=0).astype(BF16)], key_i < qry_i)

    @pl.when(p > 0)
    def _():
        process([load_page(r) for r in kp_refs], [load_page(r) for r in vp_refs], None)

    @pl.when(p == pl.num_programs(1) - 1)
    def _():
        acc = jnp.where(own_head(), acc_s[...], 0.0)
        out = acc[0:sq, :]
        for h in range(1, heads):
            out = out + acc[h * sq:(h + 1) * sq, :]
        o_ref[...] = out.astype(o_ref.dtype)


def _sb_sample(qkv, pool_k, pool_v, layer, page_table, bias, bsz, sq, heads, hd, out_dtype):
    width = heads * hd
    n_pages = page_table.shape[1]
    page = pool_k.shape[2]
    cols = SB_COLS
    ppg = math.gcd(n_pages, SB_PAGES_PER_STEP)
    assert heads * sq <= cols and sq <= page
    bias_cols = jnp.pad(jnp.repeat(bias.astype(F32), sq), (0, cols - heads * sq)).reshape(1, cols)
    pool_k = pool_k.reshape(pool_k.shape[0], pool_k.shape[1], page * heads, hd)
    pool_v = pool_v.reshape(pool_v.shape[0], pool_v.shape[1], page * heads, hd)

    def page_spec(i):
        def idx(b, s, pt):
            return (layer, pt[b, n_pages - 1 - (jnp.maximum(s - 1, 0) * ppg + i)], 0, 0)
        return pl.BlockSpec((1, 1, page * heads, hd), idx)

    body = functools.partial(_sbs_body, sq=sq, heads=heads, hd=hd, page=page, ppg=ppg)
    grid_spec = pltpu.PrefetchScalarGridSpec(
        num_scalar_prefetch=1, grid=(bsz, n_pages // ppg + 1),
        in_specs=[
            pl.BlockSpec((1, sq, width), lambda b, s, pt: (0, b, 0)),
            pl.BlockSpec((1, sq, width), lambda b, s, pt: (1, b, 0)),
            pl.BlockSpec((1, sq, width), lambda b, s, pt: (2, b, 0)),
            *[page_spec(i) for i in range(ppg)],
            *[page_spec(i) for i in range(ppg)],
            pl.BlockSpec((1, cols), lambda b, s, pt: (0, 0)),
        ],
        out_specs=pl.BlockSpec((sq, width), lambda b, s, pt: (b, 0)),
        scratch_shapes=[pltpu.VMEM((1, cols), F32), pltpu.VMEM((cols, width), F32),
                        pltpu.VMEM((cols, width), BF16)],
    )
    return pl.pallas_call(
        body, grid_spec=grid_spec,
        out_shape=jax.ShapeDtypeStruct((bsz * sq, width), out_dtype),
        compiler_params=_params(("parallel", "arbitrary")), name="sb_sample",
    )(page_table, qkv, qkv, qkv, *([pool_k] * ppg), *([pool_v] * ppg), bias_cols)


def _xa_body(x_ref, g_ref, wq_ref, mk_ref, mv_ref, wo_ref, o_ref, *, heads, hd):
    scale = hd ** -0.5

    x = x_ref[...]
    xn = (x * lax.rsqrt(jnp.mean(x * x, axis=-1, keepdims=True) + NORM_EPS) * g_ref[...]).astype(BF16)
    q = _dot(xn, wq_ref[...]).astype(BF16)
    outs = []
    for h in range(heads):
        cs = slice(h * hd, (h + 1) * hd)
        s = _dot_nt(q[:, cs], mk_ref[0, :, cs].astype(BF16)) * scale
        e = jnp.exp(s - jnp.max(s, axis=1, keepdims=True))
        pr = e / jnp.sum(e, axis=1, keepdims=True)
        outs.append(_dot(pr.astype(BF16), mv_ref[0, :, cs].astype(BF16)).astype(BF16))
    o_ref[...] = x + _dot(jnp.concatenate(outs, axis=1), wo_ref[...])


def _xa_cache_body(x_ref, g_ref, wq_ref, wo_ref, ck_hbm, cv_hbm, o_ref, kbuf, vbuf, sem, *, layer, heads, hd, bps, seq):
    step = pl.program_id(0)
    slot = step % 2

    def copies(st, sl):
        out = []
        for r in range(bps):
            for h in range(heads):
                src = (layer, st * bps + r, slice(None), h, slice(None))
                out.append(pltpu.make_async_copy(ck_hbm.at[src], kbuf.at[sl, r, h], sem.at[sl, 0, r, h]))
                out.append(pltpu.make_async_copy(cv_hbm.at[src], vbuf.at[sl, r, h], sem.at[sl, 1, r, h]))
        return out

    @pl.when(step == 0)
    def _():
        for c in copies(0, 0):
            c.start()

    @pl.when(step + 1 < pl.num_programs(0))
    def _():
        for c in copies(step + 1, 1 - slot):
            c.start()

    for c in copies(step, slot):
        c.wait()

    scale = hd ** -0.5
    x = x_ref[...]
    xn = (x * lax.rsqrt(jnp.mean(x * x, axis=-1, keepdims=True) + NORM_EPS) * g_ref[...]).astype(BF16)
    q = _dot(xn, wq_ref[...]).astype(BF16)
    rows = []
    for r in range(bps):
        outs = []
        for h in range(heads):
            s = _dot_nt(q[r * seq:(r + 1) * seq, h * hd:(h + 1) * hd], kbuf[slot, r, h].astype(BF16)) * scale
            e = jnp.exp(s - jnp.max(s, axis=1, keepdims=True))
            pr = e / jnp.sum(e, axis=1, keepdims=True)
            outs.append(_dot(pr.astype(BF16), vbuf[slot, r, h].astype(BF16)))
        rows.append(jnp.concatenate(outs, axis=1))
    o_ref[...] = x + _dot(jnp.concatenate(rows, axis=0).astype(BF16), wo_ref[...])


def _cross_attn_cached(x, g, wq, cache_k, cache_v, layer, wo, bsz, seq, mem_len, heads):
    d = x.shape[1]
    hd = d // heads
    bps = math.gcd(bsz, XA_ROWS_PER_STEP)
    body = functools.partial(_xa_cache_body, layer=layer, heads=heads, hd=hd, bps=bps, seq=seq)
    return pl.pallas_call(
        body, grid=(bsz // bps,),
        in_specs=[
            pl.BlockSpec((bps * seq, d), lambda b: (b, 0)),
            pl.BlockSpec((1, d), lambda b: (0, 0)),
            pl.BlockSpec((d, d), lambda b: (0, 0)),
            pl.BlockSpec((d, d), lambda b: (0, 0)),
            pl.BlockSpec(memory_space=pl.ANY),
            pl.BlockSpec(memory_space=pl.ANY),
        ],
        out_specs=pl.BlockSpec((bps * seq, d), lambda b: (b, 0)),
        out_shape=jax.ShapeDtypeStruct((bsz * seq, d), F32),
        scratch_shapes=[pltpu.VMEM((2, bps, heads, mem_len, hd), cache_k.dtype),
                        pltpu.VMEM((2, bps, heads, mem_len, hd), cache_v.dtype),
                        pltpu.SemaphoreType.DMA((2, 2, bps, heads))],
        compiler_params=_params(("arbitrary",)), name="cross_attn_cached",
    )(x, g.reshape(1, d), wq, wo, cache_k, cache_v)


def _cross_attn_layer(x, g, wq, mk, mv, mk_idx, mv_idx, wo, bsz, seq, mem_len, heads):
    d = x.shape[1]
    tq = math.gcd(seq, XA_QBLOCK)
    nq = seq // tq
    body = functools.partial(_xa_body, heads=heads, hd=d // heads)

    return pl.pallas_call(
        body, grid=(bsz, nq),
        in_specs=[
            pl.BlockSpec((tq, d), lambda b, i: (b * nq + i, 0)),
            pl.BlockSpec((1, d), lambda b, i: (0, 0)),
            pl.BlockSpec((d, d), lambda b, i: (0, 0)),
            pl.BlockSpec((1, mem_len, d), lambda b, i: (mk_idx, b, 0)),
            pl.BlockSpec((1, mem_len, d), lambda b, i: (mv_idx, b, 0)),
            pl.BlockSpec((d, d), lambda b, i: (0, 0)),
        ],
        out_specs=pl.BlockSpec((tq, d), lambda b, i: (b * nq + i, 0)),
        out_shape=jax.ShapeDtypeStruct((bsz * seq, d), F32),
        compiler_params=_params(("parallel", "parallel")), name="cross_attn",
    )(x, g.reshape(1, d), wq, mk, mv, wo)


def _prep_weights(p):
    ml_inner = p['ml_w_up'].shape[2]
    ssd_main = p['ssd_w_in'].shape[2] - p['ssd_dt_bias'].shape[1]
    ffn_hidden = p['ffn_w_out'].shape[1]

    def per_layer(a, lo=None, hi=None, pad_to=None, dtype=BF16):
        out = []
        for j in range(a.shape[0]):
            m = a[j, :, lo:hi] if a.ndim == 3 else a[j, lo:hi]
            if pad_to is not None:
                m = jnp.pad(m, [(0, 0)] * (m.ndim - 1) + [(0, pad_to - m.shape[-1])])
            out.append(m.astype(dtype))
        return out

    return {
        'ml_w_up': per_layer(p['ml_w_up']),
        'ml_w_qk': per_layer(p['ml_w_qk']),
        'ml_w_vo': per_layer(p['ml_w_vog'], 0, 2 * ml_inner),
        'ml_w_g': per_layer(p['ml_w_vog'], 2 * ml_inner, None, LANES),
        'ml_b_g': per_layer(p['ml_b_gate'], None, None, LANES, F32),
        'ml_w_down': per_layer(p['ml_w_down']),
        'sb_w_qkv': per_layer(p['sb_w_qkv']),
        'sb_w_o': per_layer(p['sb_w_o']),
        'ssd_w_main': per_layer(p['ssd_w_in'], 0, ssd_main),
        'ssd_w_dt': per_layer(p['ssd_w_in'], ssd_main, None, LANES),
        'ssd_w_out': per_layer(p['ssd_w_out']),
        'ssd_a': -jnp.exp(p['ssd_a_log'].astype(F32)),
        'xa_w_q': per_layer(p['xa_w_q']),
        'xa_w_kv': per_layer(p['xa_w_kv']),
        'xa_w_o': per_layer(p['xa_w_o']),
        'ffn_w_gate': per_layer(p['ffn_w_in'], 0, ffn_hidden),
        'ffn_w_up': per_layer(p['ffn_w_in'], ffn_hidden, None),
        'ffn_w_out': per_layer(p['ffn_w_out']),
    }


def _trunk(x3, mem_kv, ml_states, sb_pools, ssd_states, page_table, p, w, dims, act_dtype):
    bsz, seq, d = x3.shape
    depth = p['norm_mix'].shape[0]
    ml_heads, sb_heads, xa_heads, ssd_groups, ssd_state, ssd_hd, mem_len = dims
    x = x3.reshape(bsz * seq, d)
    ml_new, sb_new, ssd_new = [], [], []
    n_ml = n_sb = n_ssd = 0
    n_ml_layers = p['ml_w_up'].shape[0]
    ml_c = None
    for layer in range(depth):
        kind = layer % 3
        if kind == 0:
            j = n_ml
            n_ml += 1
            inner = p['ml_w_up'].shape[2]
            width = p['ml_conv_w'].shape[1]
            xm = _matmul(x, w['ml_w_up'][j], g=p['norm_mix'][layer])
            st = None if ml_states is None else ml_states[j]
            buf = None if st is None else st[3]
            xc = _causal_conv_silu(xm, 0, inner, buf, p['ml_conv_w'][j], p['ml_conv_b'][j], bsz, seq, act_dtype)
            qk = _matmul(xc, w['ml_w_qk'][j])
            vo, gates = _matmul(xm, w['ml_w_vo'][j], extra=(w['ml_w_g'][j], w['ml_b_g'][j]))
            hg, ml_c, n1, m1 = _mlstm_core(qk, vo, gates, p['ml_norm'][j], bsz, seq, ml_heads, j, n_ml_layers,
                                           None if st is None else st[:3], ml_c, act_dtype)
            x = _matmul(hg, w['ml_w_down'][j], res=x)
            ml_new.append((n1, m1, _conv_state(xm, 0, inner, buf, bsz, seq, width)))
        elif kind == 1:
            j = n_sb
            n_sb += 1
            hd = d // sb_heads
            qkv = _matmul(x, w['sb_w_qkv'][j], g=p['norm_mix'][layer], splits=3)
            if sb_pools is None:
                o = _sb_prompt(qkv, p['sb_bias'][j], bsz, seq, sb_heads, hd, act_dtype)
            else:
                o = _sb_sample(qkv, sb_pools[0], sb_pools[1], j, page_table, p['sb_bias'][j],
                               bsz, seq, sb_heads, hd, act_dtype)
            x = _matmul(o, w['sb_w_o'][j], res=x)
            shp = (bsz, seq, sb_heads, hd)
            sb_new.append((qkv[1].reshape(shp), qkv[2].reshape(shp)))
        else:
            j = n_ssd
            n_ssd += 1
            inner = p['ssd_norm'].shape[1]
            conv_ch = p['ssd_conv_w'].shape[2]
            width = p['ssd_conv_w'].shape[1]
            zx, dt_raw = _matmul(x, w['ssd_w_main'][j], g=p['norm_mix'][layer],
                                 extra=(w['ssd_w_dt'][j], jnp.zeros((LANES,), F32)))
            st = None if ssd_states is None else ssd_states[j]
            buf = None if st is None else st[1]
            xbc = _causal_conv_silu(zx, inner, conv_ch, buf, p['ssd_conv_w'][j], p['ssd_conv_b'][j],
                                    bsz, seq, F32)
            y, h1 = _ssd_core(zx, xbc, dt_raw, p['ssd_dt_bias'][j], w['ssd_a'][j],
                              p['ssd_d'][j], p['ssd_norm'][j], bsz, seq, ssd_groups, ssd_state, ssd_hd,
                              None if st is None else st[0], act_dtype)
            x = _matmul(y, w['ssd_w_out'][j], res=x)
            ssd_new.append((h1, _conv_state(zx, inner, conv_ch, buf, bsz, seq, width)))
        mk, mv, mk_idx, mv_idx = mem_kv[layer]
        if mk.ndim == 5:
            x = _cross_attn_cached(x, p['norm_xa'][layer], w['xa_w_q'][layer], mk, mv, mk_idx,
                                   w['xa_w_o'][layer], bsz, seq, mem_len, xa_heads)
        else:
            x = _cross_attn_layer(x, p['norm_xa'][layer], w['xa_w_q'][layer], mk, mv, mk_idx, mv_idx,
                                  w['xa_w_o'][layer], bsz, seq, mem_len, xa_heads)
        hid = _matmul(x, w['ffn_w_gate'][layer], g=p['norm_ffn'][layer], w2=w['ffn_w_up'][layer],
                      out_dtype=act_dtype)
        x = _matmul(hid, w['ffn_w_out'][layer], res=x, out_g=p['norm_final'] if layer == depth - 1 else None)
    return x.reshape(bsz, seq, d), ml_c, ml_new, sb_new, ssd_new


def _stack(items, idx):
    return jnp.stack([it[idx] for it in items])


def kernel(x_prompt, x_sample, cache_mem_k, cache_mem_v, cache_sb_k, cache_sb_v, state_ml_c, state_ml_n, state_ml_m, state_ml_conv, state_ssd_h, state_ssd_conv, page_table, mem_prompt, norm_mix, norm_xa, norm_ffn, norm_mem, norm_final, ml_w_up, ml_conv_w, ml_conv_b, ml_w_qk, ml_w_vog, ml_b_gate, ml_norm, ml_w_down, sb_w_qkv, sb_bias, sb_w_o, ssd_w_in, ssd_conv_w, ssd_conv_b, ssd_dt_bias, ssd_a_log, ssd_d, ssd_norm, ssd_w_out, xa_w_q, xa_w_kv, xa_w_o, ffn_w_in, ffn_w_out):
    p = dict(norm_mix=norm_mix, norm_xa=norm_xa, norm_ffn=norm_ffn, norm_final=norm_final,
             ml_w_up=ml_w_up, ml_conv_w=ml_conv_w, ml_conv_b=ml_conv_b, ml_w_qk=ml_w_qk,
             ml_w_vog=ml_w_vog, ml_b_gate=ml_b_gate, ml_norm=ml_norm, ml_w_down=ml_w_down,
             sb_w_qkv=sb_w_qkv, sb_bias=sb_bias, sb_w_o=sb_w_o,
             ssd_w_in=ssd_w_in, ssd_conv_w=ssd_conv_w, ssd_conv_b=ssd_conv_b, ssd_dt_bias=ssd_dt_bias,
             ssd_a_log=ssd_a_log, ssd_d=ssd_d, ssd_norm=ssd_norm, ssd_w_out=ssd_w_out,
             xa_w_q=xa_w_q, xa_w_kv=xa_w_kv, xa_w_o=xa_w_o, ffn_w_in=ffn_w_in, ffn_w_out=ffn_w_out)
    w = _prep_weights(p)
    depth = norm_mix.shape[0]
    d = x_prompt.shape[2]
    bp, mem_len = mem_prompt.shape[0], mem_prompt.shape[1]
    ml_heads = state_ml_c.shape[2]
    sb_heads = cache_sb_k.shape[3]
    xa_heads = cache_mem_k.shape[3]
    ssd_hd, ssd_state = state_ssd_h.shape[3], state_ssd_h.shape[4]
    ssd_groups = (ssd_conv_w.shape[2] - ssd_norm.shape[1]) // (2 * ssd_state)
    n_ml, n_ssd = state_ml_c.shape[0], state_ssd_h.shape[0]
    dims = (ml_heads, sb_heads, xa_heads, ssd_groups, ssd_state, ssd_hd, mem_len)

    mem2 = mem_prompt.reshape(bp * mem_len, d)
    mem_kv_p = [_matmul(mem2, w['xa_w_kv'][l], g=norm_mem[l], splits=2) for l in range(depth)]
    y_prompt, ml_c_p, ml_p, sb_p, ssd_p = _trunk(x_prompt, [(kv, kv, 0, 1) for kv in mem_kv_p], None, None, None,
                                         None, p, w, dims, BF16)
    shp = (bp, mem_len, xa_heads, d // xa_heads)
    mem_k_p = jnp.stack([kv[0].reshape(shp) for kv in mem_kv_p])
    mem_v_p = jnp.stack([kv[1].reshape(shp) for kv in mem_kv_p])

    mem_kv_s = [(cache_mem_k, cache_mem_v, l, l) for l in range(depth)]
    ml_cache = [(state_ml_c, state_ml_n[j], state_ml_m[j], state_ml_conv[j]) for j in range(n_ml)]
    ssd_cache = [(state_ssd_h[j], state_ssd_conv[j]) for j in range(n_ssd)]
    y_sample, ml_c_s, ml_s, sb_s, ssd_s = _trunk(x_sample, mem_kv_s, ml_cache, (cache_sb_k, cache_sb_v), ssd_cache,
                                         page_table, p, w, dims, F32)

    return (y_prompt, y_sample, mem_k_p, mem_v_p,
            _stack(sb_p, 0), _stack(sb_p, 1),
            ml_c_p, _stack(ml_p, 0), _stack(ml_p, 1), _stack(ml_p, 2),
            _stack(ssd_p, 0), _stack(ssd_p, 1),
            _stack(sb_s, 0), _stack(sb_s, 1),
            ml_c_s, _stack(ml_s, 0), _stack(ml_s, 1), _stack(ml_s, 2),
            _stack(ssd_s, 0), _stack(ssd_s, 1))
```
